```python
import math
import jax, jax.numpy as jnp
from jax import lax
import numpy as np

D_MODEL = 1024
BATCH = 4
SEQ = 4096
DEPTH = 2

HEAD_DIM = 64
HA = 8
DIL_PATTERNS = ((128, 1), (512, 4), (2048, 16))
HB = 4
C_CONV = 512
CONV_K = 31
D_FF = 2816
N_BUCKETS = 32
REL_MAX_DIST = 2048
BLK = 128
MIX_W = 512
A_W = HA * HEAD_DIM
B_W = HB * 2 * HEAD_DIM
IN_W = 3 * A_W + 3 * B_W + 2 * C_CONV
N_BRANCH = 3
EPS = 1e-6
NEG = -1e30

kernel_name = "hybrid_gated_dilated_diff_conformer_block"


def rmsnorm(x, g):
    xf = x.astype(jnp.float32)
    y = xf * lax.rsqrt(jnp.mean(xf * xf, axis=-1, keepdims=True) + EPS)
    return (y * g.astype(jnp.float32)).astype(x.dtype)


def layernorm(x, g, b):
    xf = x.astype(jnp.float32)
    mu = jnp.mean(xf, axis=-1, keepdims=True)
    var = jnp.mean(jnp.square(xf - mu), axis=-1, keepdims=True)
    y = (xf - mu) * lax.rsqrt(var + EPS) * g.astype(jnp.float32) + b.astype(jnp.float32)
    return y.astype(x.dtype)


def modulate(h, shift, scale):
    return h * (1.0 + scale[:, None, :]) + shift[:, None, :]


def swiglu(h, w_up, w_down):
    gate, up = jnp.split(h @ w_up, 2, axis=-1)
    return (jax.nn.silu(gate) * up) @ w_down


def t5_bucket(dist):
    max_exact = N_BUCKETS // 2
    d = jnp.maximum(dist.astype(jnp.float32), 1.0)
    large = max_exact + (jnp.log(d / max_exact) / math.log(REL_MAX_DIST / max_exact)
                         * (N_BUCKETS - max_exact)).astype(jnp.int32)
    large = jnp.minimum(large, N_BUCKETS - 1)
    return jnp.where(dist < max_exact, dist, large)


def dilated_window_attn(q, k, v, window, dilation, bias_table):
    b, s, h, hd = q.shape
    steps = window // dilation
    L = s // dilation
    nb = -(-L // BLK)
    Lp = nb * BLK

    def to_sub(t):
        t = t.reshape(b, L, dilation, h, hd).transpose(0, 2, 3, 1, 4)
        t = jnp.pad(t, ((0, 0), (0, 0), (0, 0), (0, Lp - L), (0, 0)))
        return t.reshape(b, dilation, h, nb, BLK, hd)

    def with_prev(t):
        prev = jnp.pad(t[:, :, :, :-1], ((0, 0), (0, 0), (0, 0), (1, 0), (0, 0), (0, 0)))
        return jnp.concatenate([prev, t], axis=-2)

    qs = to_sub(q)
    kw = with_prev(to_sub(k))
    vw = with_prev(to_sub(v))
    logits = jnp.einsum('bdhnqc,bdhnkc->bdhnqk', qs, kw).astype(jnp.float32) / math.sqrt(hd)
    i = jnp.arange(BLK)[:, None]
    j = jnp.arange(2 * BLK)[None, :]
    rel = i + BLK - j
    valid = ((rel >= 0) & (rel <= steps))[None] & ((jnp.arange(nb)[:, None, None] > 0) | (j >= BLK)[None])
    bias = jnp.moveaxis(bias_table[t5_bucket(jnp.maximum(rel, 0) * dilation)], -1, 0)
    logits = jnp.where(valid, logits + bias[:, None].astype(jnp.float32), NEG)
    m = jnp.max(logits, axis=-1, keepdims=True)
    p = jnp.exp(logits - m)
    den = jnp.sum(p, axis=-1, keepdims=True)
    o = jnp.einsum('bdhnqk,bdhnkc->bdhnqc', p.astype(vw.dtype), vw).astype(jnp.float32) / den
    lse = (m + jnp.log(den))[..., 0]
    o = o.reshape(b, dilation, h, Lp, hd)[:, :, :, :L].transpose(0, 3, 1, 2, 4).reshape(b, s, h, hd)
    lse = lse.reshape(b, dilation, h, Lp)[:, :, :, :L].transpose(0, 3, 1, 2).reshape(b, s, h)
    return o, lse


def dilated_mixer(q, k, v, bias_table):
    outs, lses = [], []
    for window, dilation in DIL_PATTERNS:
        o, lse = dilated_window_attn(q, k, v, window, dilation, bias_table)
        outs.append(o)
        lses.append(lse)
    w = jax.nn.softmax(jnp.stack(lses), axis=0)
    o = jnp.sum(w[..., None] * jnp.stack(outs), axis=0)
    return o.astype(q.dtype)


def diff_attention(q, k, v, lam, bias_table):
    b, s, h, _, hd = q.shape
    nb = s // BLK
    qb = q.reshape(b, nb, BLK, h, 2, hd).transpose(1, 0, 3, 4, 2, 5)
    kt = k.transpose(0, 2, 3, 1, 4)
    vt = v.transpose(0, 2, 1, 3)
    kpos = jnp.arange(s)
    scale = 1.0 / math.sqrt(hd)

    def block(args):
        n, qblk = args
        qpos = n * BLK + jnp.arange(BLK)
        dist = qpos[:, None] - kpos[None, :]
        logits = jnp.einsum('bhcqd,bhckd->bhcqk', qblk, kt).astype(jnp.float32) * scale
        bias = jnp.moveaxis(bias_table[t5_bucket(jnp.maximum(dist, 0))], -1, 0)
        logits = jnp.where(dist >= 0, logits + bias[None, :, None].astype(jnp.float32), NEG)
        p = jax.nn.softmax(logits, axis=-1)
        a = p[:, :, 0] - lam * p[:, :, 1]
        return jnp.einsum('bhqk,bhkd->bhqd', a.astype(vt.dtype), vt)

    out = lax.map(block, (jnp.arange(nb), qb))
    return out.transpose(1, 0, 3, 2, 4).reshape(b, s, h, 2 * hd)


def conv_module(u, conv_w, conv_b, ln_g, ln_b):
    u1, u2 = jnp.split(u, 2, axis=-1)
    u = u1 * jax.nn.sigmoid(u2)
    y = lax.conv_general_dilated(u, conv_w[:, None, :].astype(u.dtype), window_strides=(1,),
                                 padding=[(CONV_K - 1, 0)],
                                 dimension_numbers=('NWC', 'WIO', 'NWC'),
                                 feature_group_count=u.shape[-1]) + conv_b
    return jax.nn.silu(layernorm(y, ln_g, ln_b))


def token_mixer(h, l, rel_bias, w_in, qk_gain, lambda_vec, subln_g, conv_w, conv_b,
                conv_ln_g, conv_ln_b, w_branch, w_gate, b_gate, w_out):
    b, s, d = h.shape
    proj = h @ w_in
    qa, ka, va, qb, kb, vb, u = jnp.split(
        proj, [A_W, 2 * A_W, 3 * A_W, 3 * A_W + B_W, 3 * A_W + 2 * B_W, 3 * A_W + 3 * B_W], axis=-1)
    qa = rmsnorm(qa.reshape(b, s, HA, HEAD_DIM), qk_gain[0])
    ka = rmsnorm(ka.reshape(b, s, HA, HEAD_DIM), qk_gain[1])
    va = va.reshape(b, s, HA, HEAD_DIM)
    y_a = dilated_mixer(qa, ka, va, rel_bias[:, :HA]).reshape(b, s, A_W)
    lam_init = 0.8 - 0.6 * math.exp(-0.3 * l)
    lv = lambda_vec.astype(jnp.float32)
    lam = jnp.exp(jnp.sum(lv[0] * lv[1])) - jnp.exp(jnp.sum(lv[2] * lv[3])) + lam_init
    qb = rmsnorm(qb.reshape(b, s, HB, 2, HEAD_DIM), qk_gain[2:4])
    kb = rmsnorm(kb.reshape(b, s, HB, 2, HEAD_DIM), qk_gain[4:6])
    vb = vb.reshape(b, s, HB, 2 * HEAD_DIM)
    ob = diff_attention(qb, kb, vb, lam, rel_bias[:, HA:])
    y_b = (rmsnorm(ob, subln_g) * (1.0 - lam_init)).reshape(b, s, B_W)
    y_c = conv_module(u, conv_w, conv_b, conv_ln_g, conv_ln_b)
    ys = jnp.einsum('gbsm,gmd->bsgd', jnp.stack([y_a, y_b, y_c]), w_branch)
    gates = jax.nn.sigmoid(h @ w_gate + b_gate).reshape(b, s, N_BRANCH, d)
    return jnp.sum(gates * ys, axis=2) @ w_out


def setup_inputs(seed: int = 0) -> dict:
    key = jax.random.key(seed)
    ks = jax.random.split(key, 20)

    def nrm(k, shape, scale):
        return jax.random.normal(k, shape, jnp.float32) * scale

    D = D_MODEL
    return {
        "x": nrm(ks[0], (BATCH, SEQ, D), 1.0),
        "c": nrm(ks[1], (BATCH, D), 1.0),
        "rel_bias": nrm(ks[2], (N_BUCKETS, HA + HB), 0.2),
        "w_ada": nrm(ks[3], (DEPTH, D, 9 * D), 0.5 * D ** -0.5),
        "b_ada": nrm(ks[4], (DEPTH, 9 * D), 0.02),
        "norm_g": 1.0 + nrm(ks[5], (DEPTH, 3, D), 0.02),
        "w_ffn_in": nrm(ks[6], (DEPTH, 2, D, 2 * D_FF), D ** -0.5),
        "w_ffn_out": nrm(ks[7], (DEPTH, 2, D_FF, D), D_FF ** -0.5),
        "w_in": nrm(ks[8], (DEPTH, D, IN_W), D ** -0.5),
        "qk_gain": 1.0 + nrm(ks[9], (DEPTH, 6, HEAD_DIM), 0.02),
        "lambda_vec": nrm(ks[10], (DEPTH, 4, HEAD_DIM), 0.1),
        "subln_g": 1.0 + nrm(ks[11], (DEPTH, 2 * HEAD_DIM), 0.02),
        "conv_w": nrm(ks[12], (DEPTH, CONV_K, C_CONV), CONV_K ** -0.5),
        "conv_b": nrm(ks[13], (DEPTH, C_CONV), 0.02),
        "conv_ln_g": 1.0 + nrm(ks[14], (DEPTH, C_CONV), 0.02),
        "conv_ln_b": nrm(ks[15], (DEPTH, C_CONV), 0.02),
        "w_branch": nrm(ks[16], (DEPTH, N_BRANCH, MIX_W, D), MIX_W ** -0.5),
        "w_gate": nrm(ks[17], (DEPTH, D, N_BRANCH * D), D ** -0.5),
        "b_gate": nrm(ks[18], (DEPTH, N_BRANCH * D), 0.02),
        "w_out": nrm(ks[19], (DEPTH, D, D), D ** -0.5),
    }


def reference(x, c, rel_bias, w_ada, b_ada, norm_g, w_ffn_in, w_ffn_out, w_in, qk_gain,
              lambda_vec, subln_g, conv_w, conv_b, conv_ln_g, conv_ln_b, w_branch, w_gate,
              b_gate, w_out):
    b, s, d = x.shape
    for l in range(DEPTH):
        mod = (jnp.einsum('bd,de->be', jax.nn.silu(c), w_ada[l]) + b_ada[l]).reshape(b, 9, d)
        h = modulate(rmsnorm(x, norm_g[l, 0]), mod[:, 0], mod[:, 1])
        x = x + 0.5 * mod[:, 2][:, None] * swiglu(h, w_ffn_in[l, 0], w_ffn_out[l, 0])
        h = modulate(rmsnorm(x, norm_g[l, 1]), mod[:, 3], mod[:, 4])
        y = token_mixer(h, l, rel_bias, w_in[l], qk_gain[l], lambda_vec[l], subln_g[l],
                        conv_w[l], conv_b[l], conv_ln_g[l], conv_ln_b[l], w_branch[l],
                        w_gate[l], b_gate[l], w_out[l])
        x = x + mod[:, 5][:, None] * y
        h = modulate(rmsnorm(x, norm_g[l, 2]), mod[:, 6], mod[:, 7])
        x = x + 0.5 * mod[:, 8][:, None] * swiglu(h, w_ffn_in[l, 1], w_ffn_out[l, 1])
    return x
```

```python
import functools
import math

import numpy as np
import jax
import jax.numpy as jnp
from jax import lax
from jax.experimental import pallas as pl
from jax.experimental.pallas import tpu as pltpu

D_MODEL = 1024
DEPTH = 2
HEAD_DIM = 64
HA = 8
DIL_PATTERNS = ((128, 1), (512, 4), (2048, 16))
HB = 4
C_CONV = 512
CONV_K = 31
D_FF = 2816
N_BUCKETS = 32
REL_MAX_DIST = 2048
BLK = 128
MIX_W = 512
N_BRANCH = 3
EPS = 1e-6
NEG = -1e30

LANES = 128
SUBLANES = 8
VMEM_LIMIT = 56 * 1024 * 1024

F32 = jnp.float32
BF16 = jnp.bfloat16


def _cparams(sem):
    return pltpu.CompilerParams(dimension_semantics=sem, vmem_limit_bytes=VMEM_LIMIT)


def _sigmoid(x):
    return 1.0 / (1.0 + jnp.exp(-x))


def _resident(shape):
    nd = len(shape)
    return pl.BlockSpec(shape, lambda *_: (0,) * nd, pipeline_mode=pl.Buffered(1))


def _norm_mod(x, g, shift, scale):
    ms = jnp.mean(x * x, axis=-1, keepdims=True)
    y = x * lax.rsqrt(ms + EPS) * g
    return y * (1.0 + scale) + shift


ADA_TN = 1152


def _ada_kernel(c_ref, w_ref, b_ref, o_ref):
    c = c_ref[...]
    a = c * _sigmoid(c)
    o_ref[0] = jnp.dot(a, w_ref[0], preferred_element_type=F32,
                       precision=lax.Precision.HIGHEST) + b_ref[0]


def _ada_mod(c, w_ada, b_ada):
    b, d = c.shape
    rows = 8
    c_pad = jnp.pad(c, ((0, rows - b), (0, 0)))
    n = w_ada.shape[-1]
    out = pl.pallas_call(
        _ada_kernel,
        grid=(DEPTH, n // ADA_TN),
        in_specs=[
            pl.BlockSpec((rows, d), lambda l, j: (0, 0)),
            pl.BlockSpec((1, d, ADA_TN), lambda l, j: (l, 0, j)),
            pl.BlockSpec((1, 1, ADA_TN), lambda l, j: (l, 0, j)),
        ],
        out_specs=pl.BlockSpec((1, rows, ADA_TN), lambda l, j: (l, 0, j)),
        out_shape=jax.ShapeDtypeStruct((DEPTH, rows, n), F32),
        compiler_params=_cparams(("arbitrary", "arbitrary")),
        name="ada_mod",
    )(c_pad, w_ada, b_ada.reshape(DEPTH, 1, n))
    return out[:, :b].reshape(DEPTH, b, 9, d)


FFN_TM = 512
FFN_TF = 256


def _ffn_kernel(x_ref, mod_ref, g_ref, wup_ref, wdn_ref, o_ref, act_ref, *, k0):
    x = x_ref[0]
    h = _norm_mod(x, g_ref[...], mod_ref[0, k0:k0 + 1, :], mod_ref[0, k0 + 1:k0 + 2, :])
    hb = h.astype(BF16)
    for j in range(D_FF // FFN_TF):
        lo = j * FFN_TF
        gate = jnp.dot(hb, wup_ref[:, lo:lo + FFN_TF], preferred_element_type=F32)
        up = jnp.dot(hb, wup_ref[:, D_FF + lo:D_FF + lo + FFN_TF], preferred_element_type=F32)
        act_ref[:, lo:lo + FFN_TF] = (gate * _sigmoid(gate) * up).astype(BF16)
    y = jnp.dot(act_ref[...], wdn_ref[...], preferred_element_type=F32)
    o_ref[0] = x + (0.5 * mod_ref[0, k0 + 2:k0 + 3, :]) * y


def _ffn(x, mod_l, g, w_up, w_dn, k0):
    b, s, d = x.shape
    tok = pl.BlockSpec((1, FFN_TM, d), lambda i, j: (i, j, 0))
    return pl.pallas_call(
        functools.partial(_ffn_kernel, k0=k0),
        grid=(b, s // FFN_TM),
        in_specs=[
            tok,
            pl.BlockSpec((1, 9, d), lambda i, j: (i, 0, 0)),
            _resident((1, d)),
            _resident((d, 2 * D_FF)),
            _resident((D_FF, d)),
        ],
        out_specs=tok,
        out_shape=jax.ShapeDtypeStruct(x.shape, F32),
        scratch_shapes=[pltpu.VMEM((FFN_TM, D_FF), BF16)],
        compiler_params=_cparams(("arbitrary", "arbitrary")),
        name="ffn",
    )(x, mod_l, g.reshape(1, d), w_up, w_dn)


PROJ_TM = 512


def _head_rmsnorm(acc, gain):
    rows = acc.shape[0]
    low = lax.broadcasted_iota(jnp.int32, (rows, LANES), 1) < HEAD_DIM
    outs = []
    for c in range(acc.shape[1] // LANES):
        xc = acc[:, c * LANES:(c + 1) * LANES]
        sq = xc * xc
        s_lo = jnp.sum(jnp.where(low, sq, 0.0), axis=-1, keepdims=True)
        s_hi = jnp.sum(jnp.where(low, 0.0, sq), axis=-1, keepdims=True)
        r_lo = lax.rsqrt(s_lo * (1.0 / HEAD_DIM) + EPS)
        r_hi = lax.rsqrt(s_hi * (1.0 / HEAD_DIM) + EPS)
        outs.append(xc * jnp.where(low, r_lo, r_hi) * gain[:, c * LANES:(c + 1) * LANES])
    return jnp.concatenate(outs, axis=-1)


def _proj_kernel(x_ref, mod_ref, g_ref, w_ref, gain_ref,
                 qa_ref, ka_ref, va_ref, qb_ref, kb_ref, vb_ref, u_ref):
    x = x_ref[0]
    h = _norm_mod(x, g_ref[...], mod_ref[0, 3:4, :], mod_ref[0, 4:5, :])
    hb = h.astype(BF16)
    w = MIX_W

    def col(j):
        return jnp.dot(hb, w_ref[:, j * w:(j + 1) * w], preferred_element_type=F32)

    q_scale = 1.0 / math.sqrt(HEAD_DIM)
    qa_ref[0] = (_head_rmsnorm(col(0), gain_ref[0:1, :]) * q_scale).astype(BF16)
    ka_ref[0] = _head_rmsnorm(col(1), gain_ref[1:2, :]).astype(BF16)
    va_ref[0] = col(2).astype(BF16)
    qb_ref[0] = (_head_rmsnorm(col(3), gain_ref[2:3, :]) * q_scale).astype(BF16)
    kb_ref[0] = _head_rmsnorm(col(4), gain_ref[3:4, :]).astype(BF16)
    vb_ref[0] = col(5).astype(BF16)
    u_ref[0] = col(6) * _sigmoid(col(7))


def _proj(x, mod_l, g, w_in, gains):
    b, s, d = x.shape
    tok_in = pl.BlockSpec((1, PROJ_TM, d), lambda i, j: (i, j, 0))
    tok_out = pl.BlockSpec((1, PROJ_TM, MIX_W), lambda i, j: (i, j, 0))
    bf = jax.ShapeDtypeStruct((b, s, MIX_W), BF16)
    return pl.pallas_call(
        _proj_kernel,
        grid=(b, s // PROJ_TM),
        in_specs=[
            tok_in,
            pl.BlockSpec((1, 9, d), lambda i, j: (i, 0, 0)),
            _resident((1, d)),
            _resident(w_in.shape),
            _resident(gains.shape),
        ],
        out_specs=[tok_out] * 7,
        out_shape=[bf] * 6 + [jax.ShapeDtypeStruct((b, s, MIX_W), F32)],
        compiler_params=_cparams(("arbitrary", "arbitrary")),
        name="proj",
    )(x, mod_l, g.reshape(1, d), w_in, gains)


def _bucket_thresholds():
    max_exact = N_BUCKETS // 2
    d = np.arange(0, 2 * REL_MAX_DIST + 2)
    df = np.maximum(d.astype(np.float32), np.float32(1.0))
    large = max_exact + (np.log(df / np.float32(max_exact))
                         / np.float32(math.log(REL_MAX_DIST / max_exact))
                         * np.float32(N_BUCKETS - max_exact)).astype(np.int32)
    bucket = np.where(d < max_exact, d, np.minimum(large, N_BUCKETS - 1))
    return [int(np.argmax(bucket >= b)) for b in range(1, N_BUCKETS)]


_THRESHOLDS = _bucket_thresholds()


def _bias_of_dist(dist, tab_ref, head):
    val = jnp.full(dist.shape, tab_ref[0, head], F32)
    for b in range(1, N_BUCKETS):
        val = jnp.where(dist >= _THRESHOLDS[b - 1], tab_ref[b, head], val)
    return val


DIFF_T = 256


def _bias_diff_kernel(tab_ref, o_ref):
    h = pl.program_id(0)
    delta = pl.program_id(1)
    i = lax.broadcasted_iota(jnp.int32, (DIFF_T, DIFF_T), 0)
    j = lax.broadcasted_iota(jnp.int32, (DIFF_T, DIFF_T), 1)
    dist = jnp.maximum(delta * DIFF_T + i - j, 0)
    o_ref[0, 0] = _bias_of_dist(dist, tab_ref, HA + h)


def _bias_dil_kernel(tab_ref, o_ref):
    p = pl.program_id(0)
    h = pl.program_id(1)
    dil = jnp.where(p == 0, DIL_PATTERNS[0][1],
                    jnp.where(p == 1, DIL_PATTERNS[1][1], DIL_PATTERNS[2][1]))
    i = lax.broadcasted_iota(jnp.int32, (BLK, 2 * BLK), 0)
    j = lax.broadcasted_iota(jnp.int32, (BLK, 2 * BLK), 1)
    rel = i + BLK - j
    bias = _bias_of_dist(jnp.maximum(rel, 0) * dil, tab_ref, h)
    band = (rel >= 0) & (rel <= BLK)
    o_ref[0, 0, 0] = jnp.where(band & (j >= BLK), bias, NEG)
    o_ref[0, 1, 0] = jnp.where(band, bias, NEG)


def _bias_tiles(rel_bias, seq):
    smem = pl.BlockSpec(memory_space=pltpu.SMEM)
    n_delta = seq // DIFF_T
    bias_b = pl.pallas_call(
        _bias_diff_kernel,
        grid=(HB, n_delta),
        in_specs=[smem],
        out_specs=pl.BlockSpec((1, 1, DIFF_T, DIFF_T), lambda h, t: (h, t, 0, 0)),
        out_shape=jax.ShapeDtypeStruct((HB, n_delta, DIFF_T, DIFF_T), F32),
        compiler_params=_cparams(("arbitrary", "arbitrary")),
        name="bias_diff",
    )(rel_bias)
    n_pat = len(DIL_PATTERNS)
    bias_a = pl.pallas_call(
        _bias_dil_kernel,
        grid=(n_pat, HA),
        in_specs=[smem],
        out_specs=pl.BlockSpec((1, 2, 1, BLK, 2 * BLK), lambda p, h: (p, 0, h, 0, 0)),
        out_shape=jax.ShapeDtypeStruct((n_pat, 2, HA, BLK, 2 * BLK), F32),
        compiler_params=_cparams(("arbitrary", "arbitrary")),
        name="bias_dil",
    )(rel_bias)
    return bias_a, bias_b


def _dil_block(q_ref, kp_ref, kc_ref, vp_ref, vc_ref, bias_ref):
    n = pl.program_id(2)
    var = jnp.where(n == 0, 0, 1)
    low = lax.broadcasted_iota(jnp.int32, (BLK, LANES), 1) < HEAD_DIM
    outs = []
    for hp in range(HA // 2):
        sl = slice(hp * LANES, (hp + 1) * LANES)
        q = q_ref[0, :, sl]
        k = jnp.concatenate([kp_ref[0, :, sl], kc_ref[0, :, sl]], axis=0)
        v = jnp.concatenate([vp_ref[0, :, sl], vc_ref[0, :, sl]], axis=0)
        o_pair, l_pair = [], []
        for sub in range(2):
            keep = low if sub == 0 else jnp.logical_not(low)
            qm = jnp.where(keep, q, jnp.zeros_like(q))
            s = lax.dot_general(qm, k, (((1,), (1,)), ((), ())), preferred_element_type=F32)
            s = s + bias_ref[var, 2 * hp + sub]
            m = jnp.max(s, axis=-1, keepdims=True)
            p = jnp.exp(s - m)
            den = jnp.sum(p, axis=-1, keepdims=True)
            o = jnp.dot(p.astype(BF16), v, preferred_element_type=F32) / den
            o_pair.append(o)
            l_pair.append(m + jnp.log(den))
        outs.append((jnp.where(low, o_pair[0], o_pair[1]),
                     jnp.where(low, l_pair[0], l_pair[1])))
    return outs


def _dil_partial_kernel(q_ref, kp_ref, kc_ref, vp_ref, vc_ref, bias_ref, o_ref, l_ref):
    for hp, (o, l) in enumerate(_dil_block(q_ref, kp_ref, kc_ref, vp_ref, vc_ref, bias_ref)):
        sl = slice(hp * LANES, (hp + 1) * LANES)
        o_ref[0, :, sl] = o
        l_ref[0, :, sl] = l


def _dil_final_kernel(q_ref, kp_ref, kc_ref, vp_ref, vc_ref, bias_ref,
                      o1_ref, l1_ref, o2_ref, l2_ref, y_ref):
    for hp, (o3, l3) in enumerate(_dil_block(q_ref, kp_ref, kc_ref, vp_ref, vc_ref, bias_ref)):
        sl = slice(hp * LANES, (hp + 1) * LANES)
        l1 = l1_ref[0, :, sl]
        l2 = l2_ref[0, :, sl]
        mx = jnp.maximum(jnp.maximum(l1, l2), l3)
        e1 = jnp.exp(l1 - mx)
        e2 = jnp.exp(l2 - mx)
        e3 = jnp.exp(l3 - mx)
        num = e1 * o1_ref[0, :, sl] + e2 * o2_ref[0, :, sl] + e3 * o3
        y_ref[0, :, sl] = (num / (e1 + e2 + e3)).astype(y_ref.dtype)


def _dilated_mixer(qa, ka, va, bias_a):
    b, s, w = qa.shape
    prev = []
    y = None
    for p, (_, dil) in enumerate(DIL_PATTERNS):
        sub_len = s // dil
        nb = sub_len // BLK
        view = (b, sub_len, dil * w)
        cur = pl.BlockSpec((1, BLK, w), lambda i, r, n: (i, n, r))
        prv = pl.BlockSpec((1, BLK, w), lambda i, r, n: (i, jnp.maximum(n - 1, 0), r))
        bias_spec = pl.BlockSpec((None, 2, HA, BLK, 2 * BLK), lambda i, r, n: (p, 0, 0, 0, 0))
        args = [qa.reshape(view), ka.reshape(view), ka.reshape(view),
                va.reshape(view), va.reshape(view), bias_a]
        specs = [cur, prv, cur, prv, cur, bias_spec]
        f32_out = jax.ShapeDtypeStruct(view, F32)
        last = p == len(DIL_PATTERNS) - 1
        if last:
            for o_prev, l_prev in prev:
                args += [o_prev.reshape(view), l_prev.reshape(view)]
                specs += [cur, cur]
        out = pl.pallas_call(
            _dil_final_kernel if last else _dil_partial_kernel,
            grid=(b, dil, nb),
            in_specs=specs,
            out_specs=cur if last else [cur, cur],
            out_shape=jax.ShapeDtypeStruct(view, BF16) if last else [f32_out, f32_out],
            compiler_params=_cparams(("arbitrary", "arbitrary", "arbitrary")),
            name=f"dilated_d{dil}",
        )(*args)
        if last:
            y = out.reshape(b, s, w)
        else:
            prev.append((out[0].reshape(b, s, w), out[1].reshape(b, s, w)))
    return y


def _diff_kernel(q_ref, k_ref, v_ref, bias_ref, lam_ref, sg_ref, o_ref,
                 m_ref, l_ref, acc_ref, *, lam_init):
    qi = pl.program_id(2)
    t = DIFF_T
    q = q_ref[0]
    low = lax.broadcasted_iota(jnp.int32, (t, LANES), 1) < HEAD_DIM
    zero = jnp.zeros_like(q)
    qc = (jnp.where(low, q, zero), jnp.where(low, zero, q))

    m_ref[...] = jnp.full(m_ref.shape, NEG, F32)
    l_ref[...] = jnp.zeros(l_ref.shape, F32)
    acc_ref[...] = jnp.zeros(acc_ref.shape, F32)

    def step(kt, bias, causal):
        off = pl.multiple_of(kt * t, t)
        k = k_ref[0, pl.ds(off, t), :]
        v = v_ref[0, pl.ds(off, t), :]
        for c in range(2):
            s = lax.dot_general(qc[c], k, (((1,), (1,)), ((), ())), preferred_element_type=F32)
            s = s + bias
            if causal:
                row = lax.broadcasted_iota(jnp.int32, (t, t), 0)
                colj = lax.broadcasted_iota(jnp.int32, (t, t), 1)
                s = jnp.where(row >= colj, s, NEG)
            m_old = m_ref[c]
            m_new = jnp.maximum(m_old, jnp.max(s, axis=-1, keepdims=True))
            alpha = jnp.exp(m_old - m_new)
            p = jnp.exp(s - m_new)
            l_ref[c] = alpha * l_ref[c] + jnp.sum(p, axis=-1, keepdims=True)
            acc_ref[c] = alpha * acc_ref[c] + jnp.dot(p.astype(BF16), v, preferred_element_type=F32)
            m_ref[c] = m_new

    def body(kt, carry):
        step(kt, bias_ref[0, qi - kt], False)
        return carry

    lax.fori_loop(0, qi, body, 0)
    step(qi, bias_ref[0, 0], True)

    lv = lam_ref[...]
    s01 = jnp.sum(lv[0:1] * lv[1:2], axis=-1, keepdims=True)
    s23 = jnp.sum(lv[2:3] * lv[3:4], axis=-1, keepdims=True)
    lam = jnp.exp(s01) - jnp.exp(s23) + lam_init
    ob = acc_ref[0] / l_ref[0] - lam * (acc_ref[1] / l_ref[1])
    ms = jnp.mean(ob * ob, axis=-1, keepdims=True)
    y = ob * lax.rsqrt(ms + EPS) * sg_ref[...]
    o_ref[0] = (y * (1.0 - lam_init)).astype(o_ref.dtype)


def _diff_attention(qb, kb, vb, bias_b, lambda_vec, subln_g, layer):
    b, s, w = qb.shape
    t = DIFF_T
    lam_init = 0.8 - 0.6 * math.exp(-0.3 * layer)
    tile = pl.BlockSpec((1, t, LANES), lambda i, h, n: (i, n, h))
    full = pl.BlockSpec((1, s, LANES), lambda i, h, n: (i, 0, h))
    return pl.pallas_call(
        functools.partial(_diff_kernel, lam_init=lam_init),
        grid=(b, HB, s // t),
        in_specs=[
            tile, full, full,
            pl.BlockSpec((1, s // t, t, t), lambda i, h, n: (h, 0, 0, 0)),
            pl.BlockSpec(lambda_vec.shape, lambda i, h, n: (0, 0)),
            pl.BlockSpec((1, 2 * HEAD_DIM), lambda i, h, n: (0, 0)),
        ],
        out_specs=tile,
        out_shape=jax.ShapeDtypeStruct((b, s, w), BF16),
        scratch_shapes=[pltpu.VMEM((2, t, 1), F32), pltpu.VMEM((2, t, 1), F32),
                        pltpu.VMEM((2, t, 2 * HEAD_DIM), F32)],
        compiler_params=_cparams(("arbitrary", "arbitrary", "arbitrary")),
        name="diff_attn",
    )(qb, kb, vb, bias_b, lambda_vec, subln_g.reshape(1, 2 * HEAD_DIM))


CONV_TS = 512
CONV_HALO = 32
CONV_ROWS = 64


def _conv_kernel(prev_ref, cur_ref, w_ref, cb_ref, lg_ref, lb_ref, o_ref, sh_ref, y_ref):
    j = pl.program_id(1)
    halo = prev_ref[0]
    total = CONV_HALO + CONV_TS
    sh_ref[0, 0:CONV_HALO, :] = jnp.where(j == 0, jnp.zeros_like(halo), halo)
    sh_ref[0, CONV_HALO:, :] = cur_ref[0]
    for b in range(1, SUBLANES):
        sh_ref[b, SUBLANES:, :] = sh_ref[0, SUBLANES - b:total - b, :]

    def chunk(r, carry):
        r0 = pl.multiple_of(r * CONV_ROWS, CONV_ROWS)
        for c in range(C_CONV // LANES):
            sl = slice(c * LANES, (c + 1) * LANES)
            acc = jnp.zeros((CONV_ROWS, LANES), F32)
            for tap in range(CONV_K):
                back = CONV_K - 1 - tap
                a, b = divmod(back, SUBLANES)
                rows = pl.ds(r0 + CONV_HALO - a * SUBLANES, CONV_ROWS)
                acc = acc + sh_ref[b, rows, sl] * w_ref[tap:tap + 1, sl]
            y_ref[pl.ds(r0, CONV_ROWS), sl] = acc
        return carry

    lax.fori_loop(0, CONV_TS // CONV_ROWS, chunk, 0)
    y = y_ref[...] + cb_ref[...]
    mu = jnp.mean(y, axis=-1, keepdims=True)
    var = jnp.mean(jnp.square(y - mu), axis=-1, keepdims=True)
    z = (y - mu) * lax.rsqrt(var + EPS) * lg_ref[...] + lb_ref[...]
    o_ref[0] = (z * _sigmoid(z)).astype(o_ref.dtype)


def _conv_module(u, conv_w, conv_b, ln_g, ln_b):
    b, s, c = u.shape
    per_tile = CONV_TS // CONV_HALO
    vec = pl.BlockSpec((1, c), lambda i, j: (0, 0))
    return pl.pallas_call(
        _conv_kernel,
        grid=(b, s // CONV_TS),
        in_specs=[
            pl.BlockSpec((1, CONV_HALO, c), lambda i, j: (i, jnp.maximum(j * per_tile - 1, 0), 0)),
            pl.BlockSpec((1, CONV_TS, c), lambda i, j: (i, j, 0)),
            pl.BlockSpec((CONV_K, c), lambda i, j: (0, 0)),
            vec, vec, vec,
        ],
        out_specs=pl.BlockSpec((1, CONV_TS, c), lambda i, j: (i, j, 0)),
        out_shape=jax.ShapeDtypeStruct((b, s, c), BF16),
        scratch_shapes=[pltpu.VMEM((SUBLANES, CONV_HALO + CONV_TS, c), F32),
                        pltpu.VMEM((CONV_TS, c), F32)],
        compiler_params=_cparams(("arbitrary", "arbitrary")),
        name="conv_module",
    )(u, u, conv_w, conv_b.reshape(1, c), ln_g.reshape(1, c), ln_b.reshape(1, c))


MERGE_TM = 512


def _merge_kernel(x_ref, mod_ref, g_ref, ya_ref, yb_ref, yc_ref,
                  wg_ref, bg_ref, wb_ref, wo_ref, o_ref):
    x = x_ref[0]
    d = x.shape[-1]
    h = _norm_mod(x, g_ref[...], mod_ref[0, 3:4, :], mod_ref[0, 4:5, :])
    hb = h.astype(BF16)
    mixed = jnp.zeros(x.shape, F32)
    for i, y_ref in enumerate((ya_ref, yb_ref, yc_ref)):
        z = jnp.dot(hb, wg_ref[:, i * d:(i + 1) * d], preferred_element_type=F32)
        gate = _sigmoid(z + bg_ref[:, i * d:(i + 1) * d])
        mixed = mixed + gate * jnp.dot(y_ref[0], wb_ref[i], preferred_element_type=F32)
    y = jnp.dot(mixed.astype(BF16), wo_ref[...], preferred_element_type=F32)
    o_ref[0] = x + mod_ref[0, 5:6, :] * y


def _merge(x, mod_l, g, y_a, y_b, y_c, w_gate, b_gate, w_branch, w_out):
    b, s, d = x.shape
    tok = pl.BlockSpec((1, MERGE_TM, d), lambda i, j: (i, j, 0))
    br = pl.BlockSpec((1, MERGE_TM, MIX_W), lambda i, j: (i, j, 0))
    return pl.pallas_call(
        _merge_kernel,
        grid=(b, s // MERGE_TM),
        in_specs=[
            tok,
            pl.BlockSpec((1, 9, d), lambda i, j: (i, 0, 0)),
            _resident((1, d)),
            br, br, br,
            _resident(w_gate.shape),
            _resident((1, N_BRANCH * d)),
            _resident(w_branch.shape),
            _resident(w_out.shape),
        ],
        out_specs=tok,
        out_shape=jax.ShapeDtypeStruct(x.shape, F32),
        compiler_params=_cparams(("arbitrary", "arbitrary")),
        name="merge",
    )(x, mod_l, g.reshape(1, d), y_a, y_b, y_c, w_gate, b_gate.reshape(1, N_BRANCH * d),
      w_branch, w_out)


def _qk_gain_rows(qk_gain):
    ga_q = jnp.tile(qk_gain[0], HA)
    ga_k = jnp.tile(qk_gain[1], HA)
    gb_q = jnp.tile(jnp.concatenate([qk_gain[2], qk_gain[3]]), HB)
    gb_k = jnp.tile(jnp.concatenate([qk_gain[4], qk_gain[5]]), HB)
    return jnp.stack([ga_q, ga_k, gb_q, gb_k])


def kernel(x, c, rel_bias, w_ada, b_ada, norm_g, w_ffn_in, w_ffn_out, w_in, qk_gain, lambda_vec,
           subln_g, conv_w, conv_b, conv_ln_g, conv_ln_b, w_branch, w_gate, b_gate, w_out):
    b, s, d = x.shape
    mod = _ada_mod(c, w_ada, b_ada)
    bias_a, bias_b = _bias_tiles(rel_bias, s)
    for l in range(DEPTH):
        mod_l = mod[l]
        x = _ffn(x, mod_l, norm_g[l, 0], w_ffn_in[l, 0].astype(BF16),
                 w_ffn_out[l, 0].astype(BF16), 0)
        qa, ka, va, qb, kb, vb, u = _proj(x, mod_l, norm_g[l, 1], w_in[l].astype(BF16),
                                          _qk_gain_rows(qk_gain[l]))
        y_a = _dilated_mixer(qa, ka, va, bias_a)
        y_b = _diff_attention(qb, kb, vb, bias_b, lambda_vec[l], subln_g[l], l)
        y_c = _conv_module(u, conv_w[l], conv_b[l], conv_ln_g[l], conv_ln_b[l])
        x = _merge(x, mod_l, norm_g[l, 1], y_a, y_b, y_c, w_gate[l].astype(BF16), b_gate[l],
                   w_branch[l].astype(BF16), w_out[l].astype(BF16))
        x = _ffn(x, mod_l, norm_g[l, 2], w_ffn_in[l, 1].astype(BF16),
                 w_ffn_out[l, 1].astype(BF16), 6)
    return x
```

```python
import functools
import math

import numpy as np
import jax
import jax.numpy as jnp
from jax import lax
from jax.experimental import pallas as pl
from jax.experimental.pallas import tpu as pltpu

D_MODEL = 1024
DEPTH = 2
HEAD_DIM = 64
HA = 8
DIL_PATTERNS = ((128, 1), (512, 4), (2048, 16))
HB = 4
C_CONV = 512
CONV_K = 31
D_FF = 2816
N_BUCKETS = 32
REL_MAX_DIST = 2048
BLK = 128
MIX_W = 512
N_BRANCH = 3
EPS = 1e-6
NEG = -1e30
LOG2E = math.log2(math.e)

LANES = 128
SUBLANES = 8
VMEM_LIMIT = 56 * 1024 * 1024

F32 = jnp.float32
BF16 = jnp.bfloat16


def _cparams(sem):
    return pltpu.CompilerParams(dimension_semantics=sem, vmem_limit_bytes=VMEM_LIMIT)


def _sigmoid(x):
    return 1.0 / (1.0 + jnp.exp(-x))


def _resident(shape):
    nd = len(shape)
    return pl.BlockSpec(shape, lambda *_: (0,) * nd, pipeline_mode=pl.Buffered(1))


def _norm_mod(x, g, shift, scale):
    ms = jnp.mean(x * x, axis=-1, keepdims=True)
    y = x * lax.rsqrt(ms + EPS) * g
    return y * (1.0 + scale) + shift


ADA_TN = 1152


def _ada_kernel(c_ref, w_ref, b_ref, o_ref):
    c = c_ref[...]
    a = c * _sigmoid(c)
    o_ref[0] = jnp.dot(a, w_ref[0], preferred_element_type=F32,
                       precision=lax.Precision.HIGHEST) + b_ref[0]


def _ada_mod(c, w_ada, b_ada):
    b, d = c.shape
    rows = 8
    c_pad = jnp.pad(c, ((0, rows - b), (0, 0)))
    n = w_ada.shape[-1]
    out = pl.pallas_call(
        _ada_kernel,
        grid=(DEPTH, n // ADA_TN),
        in_specs=[
            pl.BlockSpec((rows, d), lambda l, j: (0, 0)),
            pl.BlockSpec((1, d, ADA_TN), lambda l, j: (l, 0, j)),
            pl.BlockSpec((1, 1, ADA_TN), lambda l, j: (l, 0, j)),
        ],
        out_specs=pl.BlockSpec((1, rows, ADA_TN), lambda l, j: (l, 0, j)),
        out_shape=jax.ShapeDtypeStruct((DEPTH, rows, n), F32),
        compiler_params=_cparams(("arbitrary", "arbitrary")),
        name="ada_mod",
    )(c_pad, w_ada, b_ada.reshape(DEPTH, 1, n))
    return out[:, :b].reshape(DEPTH, b, 9, d)


FFN_TM = 512
FFN_TF = 256


def _ffn_kernel(x_ref, mod_ref, g_ref, wup_ref, wdn_ref, o_ref, act_ref, *, k0):
    x = x_ref[0]
    h = _norm_mod(x, g_ref[...], mod_ref[0, k0:k0 + 1, :], mod_ref[0, k0 + 1:k0 + 2, :])
    hb = h.astype(BF16)
    for j in range(D_FF // FFN_TF):
        lo = j * FFN_TF
        gate = jnp.dot(hb, wup_ref[:, lo:lo + FFN_TF], preferred_element_type=F32)
        up = jnp.dot(hb, wup_ref[:, D_FF + lo:D_FF + lo + FFN_TF], preferred_element_type=F32)
        act_ref[:, lo:lo + FFN_TF] = (gate * _sigmoid(gate) * up).astype(BF16)
    y = jnp.dot(act_ref[...], wdn_ref[...], preferred_element_type=F32)
    o_ref[0] = x + (0.5 * mod_ref[0, k0 + 2:k0 + 3, :]) * y


def _ffn(x, mod_l, g, w_up, w_dn, k0):
    b, s, d = x.shape
    tok = pl.BlockSpec((1, FFN_TM, d), lambda i, j: (i, j, 0))
    return pl.pallas_call(
        functools.partial(_ffn_kernel, k0=k0),
        grid=(b, s // FFN_TM),
        in_specs=[
            tok,
            pl.BlockSpec((1, 9, d), lambda i, j: (i, 0, 0)),
            _resident((1, d)),
            _resident((d, 2 * D_FF)),
            _resident((D_FF, d)),
        ],
        out_specs=tok,
        out_shape=jax.ShapeDtypeStruct(x.shape, F32),
        scratch_shapes=[pltpu.VMEM((FFN_TM, D_FF), BF16)],
        compiler_params=_cparams(("arbitrary", "arbitrary")),
        name="ffn",
    )(x, mod_l, g.reshape(1, d), w_up, w_dn)


PROJ_TM = 512


def _head_rmsnorm(acc, gain):
    rows = acc.shape[0]
    low = lax.broadcasted_iota(jnp.int32, (rows, LANES), 1) < HEAD_DIM
    outs = []
    for c in range(acc.shape[1] // LANES):
        xc = acc[:, c * LANES:(c + 1) * LANES]
        sq = xc * xc
        s_lo = jnp.sum(jnp.where(low, sq, 0.0), axis=-1, keepdims=True)
        s_hi = jnp.sum(jnp.where(low, 0.0, sq), axis=-1, keepdims=True)
        r_lo = lax.rsqrt(s_lo * (1.0 / HEAD_DIM) + EPS)
        r_hi = lax.rsqrt(s_hi * (1.0 / HEAD_DIM) + EPS)
        outs.append(xc * jnp.where(low, r_lo, r_hi) * gain[:, c * LANES:(c + 1) * LANES])
    return jnp.concatenate(outs, axis=-1)


def _proj_kernel(x_ref, mod_ref, g_ref, w_ref, gain_ref,
                 qa_ref, ka_ref, va_ref, qb_ref, kb_ref, vbt_ref, u_ref):
    x = x_ref[0]
    h = _norm_mod(x, g_ref[...], mod_ref[0, 3:4, :], mod_ref[0, 4:5, :])
    hb = h.astype(BF16)
    w = MIX_W

    def col(j):
        return jnp.dot(hb, w_ref[:, j * w:(j + 1) * w], preferred_element_type=F32)

    q_scale = 1.0 / math.sqrt(HEAD_DIM)
    qa_ref[0] = (_head_rmsnorm(col(0), gain_ref[0:1, :]) * q_scale).astype(BF16)
    ka_ref[0] = _head_rmsnorm(col(1), gain_ref[1:2, :]).astype(BF16)
    va_ref[0] = col(2).astype(BF16)
    qb_ref[0] = (_head_rmsnorm(col(3), gain_ref[2:3, :]) * (q_scale * LOG2E)).astype(BF16)
    kb_ref[0] = _head_rmsnorm(col(4), gain_ref[3:4, :]).astype(BF16)
    vbt_ref[0] = col(5).T.astype(BF16)
    u_ref[0] = col(6) * _sigmoid(col(7))


def _proj(x, mod_l, g, w_in, gains):
    b, s, d = x.shape
    tok_in = pl.BlockSpec((1, PROJ_TM, d), lambda i, j: (i, j, 0))
    tok_out = pl.BlockSpec((1, PROJ_TM, MIX_W), lambda i, j: (i, j, 0))
    bf = jax.ShapeDtypeStruct((b, s, MIX_W), BF16)
    tr_out = pl.BlockSpec((1, MIX_W, PROJ_TM), lambda i, j: (i, 0, j))
    return pl.pallas_call(
        _proj_kernel,
        grid=(b, s // PROJ_TM),
        in_specs=[
            tok_in,
            pl.BlockSpec((1, 9, d), lambda i, j: (i, 0, 0)),
            _resident((1, d)),
            _resident(w_in.shape),
            _resident(gains.shape),
        ],
        out_specs=[tok_out] * 5 + [tr_out, tok_out],
        out_shape=[bf] * 5 + [jax.ShapeDtypeStruct((b, MIX_W, s), BF16),
                              jax.ShapeDtypeStruct((b, s, MIX_W), F32)],
        compiler_params=_cparams(("arbitrary", "arbitrary")),
        name="proj",
    )(x, mod_l, g.reshape(1, d), w_in, gains)


def _bucket_thresholds():
    max_exact = N_BUCKETS // 2
    d = np.arange(0, 2 * REL_MAX_DIST + 2)
    df = np.maximum(d.astype(np.float32), np.float32(1.0))
    large = max_exact + (np.log(df / np.float32(max_exact))
                         / np.float32(math.log(REL_MAX_DIST / max_exact))
                         * np.float32(N_BUCKETS - max_exact)).astype(np.int32)
    bucket = np.where(d < max_exact, d, np.minimum(large, N_BUCKETS - 1))
    return [int(np.argmax(bucket >= b)) for b in range(1, N_BUCKETS)]


_THRESHOLDS = _bucket_thresholds()


def _bias_of_dist(dist, tab_ref, head):
    val = jnp.full(dist.shape, tab_ref[0, head], F32)
    for b in range(1, N_BUCKETS):
        val = jnp.where(dist >= _THRESHOLDS[b - 1], tab_ref[b, head], val)
    return val


DIFF_T = 256


def _diff_bias_tiles(seq):
    first_const = -(-(_THRESHOLDS[-1] + DIFF_T - 1) // DIFF_T)
    return min(seq // DIFF_T, first_const + 1)


def _bias_diff_kernel(tab_ref, o_ref):
    h = pl.program_id(0)
    delta = pl.program_id(1)
    j = lax.broadcasted_iota(jnp.int32, (DIFF_T, DIFF_T), 0)
    i = lax.broadcasted_iota(jnp.int32, (DIFF_T, DIFF_T), 1)
    dist = jnp.maximum(delta * DIFF_T + i - j, 0)
    o_ref[0, 0] = _bias_of_dist(dist, tab_ref, HA + h) * LOG2E


def _bias_dil_kernel(tab_ref, o_ref):
    p = pl.program_id(0)
    h = pl.program_id(1)
    dil = jnp.where(p == 0, DIL_PATTERNS[0][1],
                    jnp.where(p == 1, DIL_PATTERNS[1][1], DIL_PATTERNS[2][1]))
    i = lax.broadcasted_iota(jnp.int32, (BLK, 2 * BLK), 0)
    j = lax.broadcasted_iota(jnp.int32, (BLK, 2 * BLK), 1)
    rel = i + BLK - j
    bias = _bias_of_dist(jnp.maximum(rel, 0) * dil, tab_ref, h)
    band = (rel >= 0) & (rel <= BLK)
    o_ref[0, 0, 0] = jnp.where(band & (j >= BLK), bias, NEG)
    o_ref[0, 1, 0] = jnp.where(band, bias, NEG)


def _bias_tiles(rel_bias, seq):
    smem = pl.BlockSpec(memory_space=pltpu.SMEM)
    n_delta = _diff_bias_tiles(seq)
    bias_b = pl.pallas_call(
        _bias_diff_kernel,
        grid=(HB, n_delta),
        in_specs=[smem],
        out_specs=pl.BlockSpec((1, 1, DIFF_T, DIFF_T), lambda h, t: (h, t, 0, 0)),
        out_shape=jax.ShapeDtypeStruct((HB, n_delta, DIFF_T, DIFF_T), F32),
        compiler_params=_cparams(("arbitrary", "arbitrary")),
        name="bias_diff",
    )(rel_bias)
    n_pat = len(DIL_PATTERNS)
    bias_a = pl.pallas_call(
        _bias_dil_kernel,
        grid=(n_pat, HA),
        in_specs=[smem],
        out_specs=pl.BlockSpec((1, 2, 1, BLK, 2 * BLK), lambda p, h: (p, 0, h, 0, 0)),
        out_shape=jax.ShapeDtypeStruct((n_pat, 2, HA, BLK, 2 * BLK), F32),
        compiler_params=_cparams(("arbitrary", "arbitrary")),
        name="bias_dil",
    )(rel_bias)
    return bias_a, bias_b


def _dil_block(q_ref, kp_ref, kc_ref, vp_ref, vc_ref, bias_ref):
    n = pl.program_id(2)
    var = jnp.where(n == 0, 0, 1)
    low = lax.broadcasted_iota(jnp.int32, (BLK, LANES), 1) < HEAD_DIM
    outs = []
    for hp in range(HA // 2):
        sl = slice(hp * LANES, (hp + 1) * LANES)
        q = q_ref[0, :, sl]
        k = jnp.concatenate([kp_ref[0, :, sl], kc_ref[0, :, sl]], axis=0)
        v = jnp.concatenate([vp_ref[0, :, sl], vc_ref[0, :, sl]], axis=0)
        o_pair, l_pair = [], []
        for sub in range(2):
            keep = low if sub == 0 else jnp.logical_not(low)
            qm = jnp.where(keep, q, jnp.zeros_like(q))
            s = lax.dot_general(qm, k, (((1,), (1,)), ((), ())), preferred_element_type=F32)
            s = s + bias_ref[var, 2 * hp + sub]
            m = jnp.max(s, axis=-1, keepdims=True)
            p = jnp.exp(s - m)
            den = jnp.sum(p, axis=-1, keepdims=True)
            o = jnp.dot(p.astype(BF16), v, preferred_element_type=F32) / den
            o_pair.append(o)
            l_pair.append(m + jnp.log(den))
        outs.append((jnp.where(low, o_pair[0], o_pair[1]),
                     jnp.where(low, l_pair[0], l_pair[1])))
    return outs


def _dil_partial_kernel(q_ref, kp_ref, kc_ref, vp_ref, vc_ref, bias_ref, o_ref, l_ref):
    for hp, (o, l) in enumerate(_dil_block(q_ref, kp_ref, kc_ref, vp_ref, vc_ref, bias_ref)):
        sl = slice(hp * LANES, (hp + 1) * LANES)
        o_ref[0, :, sl] = o
        l_ref[0, :, sl] = l


def _dil_final_kernel(q_ref, kp_ref, kc_ref, vp_ref, vc_ref, bias_ref,
                      o1_ref, l1_ref, o2_ref, l2_ref, y_ref):
    for hp, (o3, l3) in enumerate(_dil_block(q_ref, kp_ref, kc_ref, vp_ref, vc_ref, bias_ref)):
        sl = slice(hp * LANES, (hp + 1) * LANES)
        l1 = l1_ref[0, :, sl]
        l2 = l2_ref[0, :, sl]
        mx = jnp.maximum(jnp.maximum(l1, l2), l3)
        e1 = jnp.exp(l1 - mx)
        e2 = jnp.exp(l2 - mx)
        e3 = jnp.exp(l3 - mx)
        num = e1 * o1_ref[0, :, sl] + e2 * o2_ref[0, :, sl] + e3 * o3
        y_ref[0, :, sl] = (num / (e1 + e2 + e3)).astype(y_ref.dtype)


def _dilated_mixer(qa, ka, va, bias_a):
    b, s, w = qa.shape
    prev = []
    y = None
    for p, (_, dil) in enumerate(DIL_PATTERNS):
        sub_len = s // dil
        nb = sub_len // BLK
        view = (b, sub_len, dil * w)
        cur = pl.BlockSpec((1, BLK, w), lambda i, r, n: (i, n, r))
        prv = pl.BlockSpec((1, BLK, w), lambda i, r, n: (i, jnp.maximum(n - 1, 0), r))
        bias_spec = pl.BlockSpec((None, 2, HA, BLK, 2 * BLK), lambda i, r, n: (p, 0, 0, 0, 0))
        args = [qa.reshape(view), ka.reshape(view), ka.reshape(view),
                va.reshape(view), va.reshape(view), bias_a]
        specs = [cur, prv, cur, prv, cur, bias_spec]
        f32_out = jax.ShapeDtypeStruct(view, F32)
        last = p == len(DIL_PATTERNS) - 1
        if last:
            for o_prev, l_prev in prev:
                args += [o_prev.reshape(view), l_prev.reshape(view)]
                specs += [cur, cur]
        out = pl.pallas_call(
            _dil_final_kernel if last else _dil_partial_kernel,
            grid=(b, dil, nb),
            in_specs=specs,
            out_specs=cur if last else [cur, cur],
            out_shape=jax.ShapeDtypeStruct(view, BF16) if last else [f32_out, f32_out],
            compiler_params=_cparams(("arbitrary", "arbitrary", "arbitrary")),
            name=f"dilated_d{dil}",
        )(*args)
        if last:
            y = out.reshape(b, s, w)
        else:
            prev.append((out[0].reshape(b, s, w), out[1].reshape(b, s, w)))
    return y


DIFF_TQ = 2 * DIFF_T
DIFF_ONES = 16


def _diff_kernel(q_ref, k_ref, vt_ref, bias_ref, lam_ref, sg_ref, o_ref,
                 qs_ref, m_ref, acc_ref, s0_ref, s1_ref, *, lam_init, n_bias):
    qi = pl.program_id(2)
    t = DIFF_T
    dv = 2 * HEAD_DIM
    n_blk = 2 * DIFF_TQ // t
    q = q_ref[0]
    low = lax.broadcasted_iota(jnp.int32, q.shape, 1) < HEAD_DIM
    zero = jnp.zeros_like(q)
    qs_ref[0:DIFF_TQ, :] = jnp.where(low, q, zero)
    qs_ref[DIFF_TQ:, :] = jnp.where(low, zero, q)
    m_ref[...] = jnp.full(m_ref.shape, NEG, F32)
    acc_ref[...] = jnp.zeros(acc_ref.shape, F32)
    ones = jnp.ones((DIFF_ONES, t), BF16)

    def produce(kt, s_ref, blocks):
        off = pl.multiple_of(kt * t, t)
        k = k_ref[0, pl.ds(off, t), :]
        for blk in blocks:
            lanes = slice(blk * t, (blk + 1) * t)
            s = lax.dot_general(k, qs_ref[lanes, :], (((1,), (1,)), ((), ())),
                                preferred_element_type=F32)
            delta = 2 * qi + blk % 2 - kt
            s_ref[:, lanes] = s + bias_ref[0, jnp.minimum(delta, n_bias - 1)]

    def consume(kt, s_ref, modes):
        off = pl.multiple_of(kt * t, t)
        vt = jnp.concatenate([vt_ref[0, :, pl.ds(off, t)], ones], axis=0)
        for blk, mode in enumerate(modes):
            if mode == "skip":
                continue
            lanes = slice(blk * t, (blk + 1) * t)
            s = s_ref[:, lanes]
            if mode == "diag":
                key = lax.broadcasted_iota(jnp.int32, (t, t), 0)
                qry = lax.broadcasted_iota(jnp.int32, (t, t), 1)
                s = jnp.where(qry >= key, s, NEG)
            m_old = m_ref[:, lanes]
            m_new = jnp.maximum(m_old, jnp.max(s, axis=0, keepdims=True))
            alpha = jnp.exp2(m_old - m_new)
            p = jnp.exp2(s - m_new).astype(BF16)
            pv = jnp.dot(vt, p, preferred_element_type=F32)
            acc_ref[:, lanes] = alpha * acc_ref[:, lanes] + pv
            m_ref[:, lanes] = m_new

    every = tuple(range(n_blk))
    full = ("full",) * n_blk
    produce(0, s0_ref, every)

    def body(j, carry):
        kt = 2 * j
        produce(kt + 1, s1_ref, every)
        consume(kt, s0_ref, full)
        produce(kt + 2, s0_ref, every)
        consume(kt + 1, s1_ref, full)
        return carry

    lax.fori_loop(0, qi, body, 0)
    produce(2 * qi + 1, s1_ref, (1, 3))
    consume(2 * qi, s0_ref, ("diag", "full", "diag", "full"))
    consume(2 * qi + 1, s1_ref, ("skip", "diag", "skip", "diag"))

    lv = lam_ref[...]
    s01 = jnp.sum(lv[0:1] * lv[1:2], axis=-1, keepdims=True)
    s23 = jnp.sum(lv[2:3] * lv[3:4], axis=-1, keepdims=True)
    lam = jnp.exp(s01) - jnp.exp(s23) + lam_init
    o_t = acc_ref[0:dv, :] / acc_ref[dv:dv + 1, :]
    ob = o_t[:, :DIFF_TQ] - lam * o_t[:, DIFF_TQ:]
    ms = jnp.mean(ob * ob, axis=0, keepdims=True)
    y = ob * lax.rsqrt(ms + EPS) * sg_ref[...]
    o_ref[0] = (y * (1.0 - lam_init)).T.astype(o_ref.dtype)


def _diff_attention(qb, kb, vbt, bias_b, lambda_vec, subln_g, layer):
    b, s, w = qb.shape
    t = DIFF_T
    dv = 2 * HEAD_DIM
    n_bias = bias_b.shape[1]
    lam_init = 0.8 - 0.6 * math.exp(-0.3 * layer)
    tile = pl.BlockSpec((1, DIFF_TQ, LANES), lambda i, h, n: (i, n, h))
    return pl.pallas_call(
        functools.partial(_diff_kernel, lam_init=lam_init, n_bias=n_bias),
        grid=(b, HB, s // DIFF_TQ),
        in_specs=[
            tile,
            pl.BlockSpec((1, s, LANES), lambda i, h, n: (i, 0, h)),
            pl.BlockSpec((1, dv, s), lambda i, h, n: (i, h, 0)),
            pl.BlockSpec((1, n_bias, t, t), lambda i, h, n: (h, 0, 0, 0)),
            pl.BlockSpec(lambda_vec.shape, lambda i, h, n: (0, 0)),
            pl.BlockSpec((dv, 1), lambda i, h, n: (0, 0)),
        ],
        out_specs=tile,
        out_shape=jax.ShapeDtypeStruct((b, s, w), BF16),
        scratch_shapes=[pltpu.VMEM((2 * DIFF_TQ, LANES), BF16),
                        pltpu.VMEM((1, 2 * DIFF_TQ), F32),
                        pltpu.VMEM((dv + DIFF_ONES, 2 * DIFF_TQ), F32),
                        pltpu.VMEM((t, 2 * DIFF_TQ), F32),
                        pltpu.VMEM((t, 2 * DIFF_TQ), F32)],
        compiler_params=_cparams(("arbitrary", "arbitrary", "arbitrary")),
        name="diff_attn",
    )(qb, kb, vbt, bias_b, lambda_vec, subln_g.reshape(dv, 1))


CONV_TS = 512
CONV_HALO = 32
CONV_ROWS = 64


def _conv_kernel(prev_ref, cur_ref, w_ref, cb_ref, lg_ref, lb_ref, o_ref, sh_ref, y_ref):
    j = pl.program_id(1)
    halo = prev_ref[0]
    total = CONV_HALO + CONV_TS
    sh_ref[0, 0:CONV_HALO, :] = jnp.where(j == 0, jnp.zeros_like(halo), halo)
    sh_ref[0, CONV_HALO:, :] = cur_ref[0]
    for b in range(1, SUBLANES):
        sh_ref[b, SUBLANES:, :] = sh_ref[0, SUBLANES - b:total - b, :]

    def chunk(r, carry):
        r0 = pl.multiple_of(r * CONV_ROWS, CONV_ROWS)
        for c in range(C_CONV // LANES):
            sl = slice(c * LANES, (c + 1) * LANES)
            acc = jnp.zeros((CONV_ROWS, LANES), F32)
            for tap in range(CONV_K):
                back = CONV_K - 1 - tap
                a, b = divmod(back, SUBLANES)
                rows = pl.ds(r0 + CONV_HALO - a * SUBLANES, CONV_ROWS)
                acc = acc + sh_ref[b, rows, sl] * w_ref[tap:tap + 1, sl]
            y_ref[pl.ds(r0, CONV_ROWS), sl] = acc
        return carry

    lax.fori_loop(0, CONV_TS // CONV_ROWS, chunk, 0)
    y = y_ref[...] + cb_ref[...]
    mu = jnp.mean(y, axis=-1, keepdims=True)
    var = jnp.mean(jnp.square(y - mu), axis=-1, keepdims=True)
    z = (y - mu) * lax.rsqrt(var + EPS) * lg_ref[...] + lb_ref[...]
    o_ref[0] = (z * _sigmoid(z)).astype(o_ref.dtype)


def _conv_module(u, conv_w, conv_b, ln_g, ln_b):
    b, s, c = u.shape
    per_tile = CONV_TS // CONV_HALO
    vec = pl.BlockSpec((1, c), lambda i, j: (0, 0))
    return pl.pallas_call(
        _conv_kernel,
        grid=(b, s // CONV_TS),
        in_specs=[
            pl.BlockSpec((1, CONV_HALO, c), lambda i, j: (i, jnp.maximum(j * per_tile - 1, 0), 0)),
            pl.BlockSpec((1, CONV_TS, c), lambda i, j: (i, j, 0)),
            pl.BlockSpec((CONV_K, c), lambda i, j: (0, 0)),
            vec, vec, vec,
        ],
        out_specs=pl.BlockSpec((1, CONV_TS, c), lambda i, j: (i, j, 0)),
        out_shape=jax.ShapeDtypeStruct((b, s, c), BF16),
        scratch_shapes=[pltpu.VMEM((SUBLANES, CONV_HALO + CONV_TS, c), F32),
                        pltpu.VMEM((CONV_TS, c), F32)],
        compiler_params=_cparams(("arbitrary", "arbitrary")),
        name="conv_module",
    )(u, u, conv_w, conv_b.reshape(1, c), ln_g.reshape(1, c), ln_b.reshape(1, c))


MERGE_TM = 512


def _merge_kernel(x_ref, mod_ref, g_ref, ya_ref, yb_ref, yc_ref,
                  wg_ref, bg_ref, wb_ref, wo_ref, o_ref):
    x = x_ref[0]
    d = x.shape[-1]
    h = _norm_mod(x, g_ref[...], mod_ref[0, 3:4, :], mod_ref[0, 4:5, :])
    hb = h.astype(BF16)
    mixed = jnp.zeros(x.shape, F32)
    for i, y_ref in enumerate((ya_ref, yb_ref, yc_ref)):
        z = jnp.dot(hb, wg_ref[:, i * d:(i + 1) * d], preferred_element_type=F32)
        gate = _sigmoid(z + bg_ref[:, i * d:(i + 1) * d])
        mixed = mixed + gate * jnp.dot(y_ref[0], wb_ref[i], preferred_element_type=F32)
    y = jnp.dot(mixed.astype(BF16), wo_ref[...], preferred_element_type=F32)
    o_ref[0] = x + mod_ref[0, 5:6, :] * y


def _merge(x, mod_l, g, y_a, y_b, y_c, w_gate, b_gate, w_branch, w_out):
    b, s, d = x.shape
    tok = pl.BlockSpec((1, MERGE_TM, d), lambda i, j: (i, j, 0))
    br = pl.BlockSpec((1, MERGE_TM, MIX_W), lambda i, j: (i, j, 0))
    return pl.pallas_call(
        _merge_kernel,
        grid=(b, s // MERGE_TM),
        in_specs=[
            tok,
            pl.BlockSpec((1, 9, d), lambda i, j: (i, 0, 0)),
            _resident((1, d)),
            br, br, br,
            _resident(w_gate.shape),
            _resident((1, N_BRANCH * d)),
            _resident(w_branch.shape),
            _resident(w_out.shape),
        ],
        out_specs=tok,
        out_shape=jax.ShapeDtypeStruct(x.shape, F32),
        compiler_params=_cparams(("arbitrary", "arbitrary")),
        name="merge",
    )(x, mod_l, g.reshape(1, d), y_a, y_b, y_c, w_gate, b_gate.reshape(1, N_BRANCH * d),
      w_branch, w_out)


def _qk_gain_rows(qk_gain):
    ga_q = jnp.tile(qk_gain[0], HA)
    ga_k = jnp.tile(qk_gain[1], HA)
    gb_q = jnp.tile(jnp.concatenate([qk_gain[2], qk_gain[3]]), HB)
    gb_k = jnp.tile(jnp.concatenate([qk_gain[4], qk_gain[5]]), HB)
    return jnp.stack([ga_q, ga_k, gb_q, gb_k])


def kernel(x, c, rel_bias, w_ada, b_ada, norm_g, w_ffn_in, w_ffn_out, w_in, qk_gain, lambda_vec,
           subln_g, conv_w, conv_b, conv_ln_g, conv_ln_b, w_branch, w_gate, b_gate, w_out):
    b, s, d = x.shape
    mod = _ada_mod(c, w_ada, b_ada)
    bias_a, bias_b = _bias_tiles(rel_bias, s)
    for l in range(DEPTH):
        mod_l = mod[l]
        x = _ffn(x, mod_l, norm_g[l, 0], w_ffn_in[l, 0].astype(BF16),
                 w_ffn_out[l, 0].astype(BF16), 0)
        qa, ka, va, qb, kb, vbt, u = _proj(x, mod_l, norm_g[l, 1], w_in[l].astype(BF16),
                                          _qk_gain_rows(qk_gain[l]))
        y_a = _dilated_mixer(qa, ka, va, bias_a)
        y_b = _diff_attention(qb, kb, vbt, bias_b, lambda_vec[l], subln_g[l], l)
        y_c = _conv_module(u, conv_w[l], conv_b[l], conv_ln_g[l], conv_ln_b[l])
        x = _merge(x, mod_l, norm_g[l, 1], y_a, y_b, y_c, w_gate[l].astype(BF16), b_gate[l],
                   w_branch[l].astype(BF16), w_out[l].astype(BF16))
        x = _ffn(x, mod_l, norm_g[l, 2], w_ffn_in[l, 1].astype(BF16),
                 w_ffn_out[l, 1].astype(BF16), 6)
    return x
```

```python
import functools
import math

import numpy as np
import jax
import jax.numpy as jnp
from jax import lax
from jax.experimental import pallas as pl
from jax.experimental.pallas import tpu as pltpu

D_MODEL = 1024
DEPTH = 2
HEAD_DIM = 64
HA = 8
DIL_PATTERNS = ((128, 1), (512, 4), (2048, 16))
HB = 4
C_CONV = 512
CONV_K = 31
D_FF = 2816
N_BUCKETS = 32
REL_MAX_DIST = 2048
BLK = 128
MIX_W = 512
N_BRANCH = 3
EPS = 1e-6
NEG = -1e30
LOG2E = math.log2(math.e)

LANES = 128
SUBLANES = 8
VMEM_LIMIT = 56 * 1024 * 1024

F32 = jnp.float32
BF16 = jnp.bfloat16


def _cparams(sem):
    return pltpu.CompilerParams(dimension_semantics=sem, vmem_limit_bytes=VMEM_LIMIT)


def _sigmoid(x):
    return 1.0 / (1.0 + jnp.exp(-x))


def _resident(shape):
    nd = len(shape)
    return pl.BlockSpec(shape, lambda *_: (0,) * nd, pipeline_mode=pl.Buffered(1))


def _norm_mod(x, g, shift, scale):
    ms = jnp.mean(x * x, axis=-1, keepdims=True)
    y = x * lax.rsqrt(ms + EPS) * g
    return y * (1.0 + scale) + shift


ADA_TN = 1152


def _ada_kernel(c_ref, w_ref, b_ref, o_ref):
    c = c_ref[...]
    a = c * _sigmoid(c)
    o_ref[0] = jnp.dot(a, w_ref[0], preferred_element_type=F32,
                       precision=lax.Precision.HIGHEST) + b_ref[0]


def _ada_mod(c, w_ada, b_ada):
    b, d = c.shape
    rows = 8
    c_pad = jnp.pad(c, ((0, rows - b), (0, 0)))
    n = w_ada.shape[-1]
    out = pl.pallas_call(
        _ada_kernel,
        grid=(DEPTH, n // ADA_TN),
        in_specs=[
            pl.BlockSpec((rows, d), lambda l, j: (0, 0)),
            pl.BlockSpec((1, d, ADA_TN), lambda l, j: (l, 0, j)),
            pl.BlockSpec((1, 1, ADA_TN), lambda l, j: (l, 0, j)),
        ],
        out_specs=pl.BlockSpec((1, rows, ADA_TN), lambda l, j: (l, 0, j)),
        out_shape=jax.ShapeDtypeStruct((DEPTH, rows, n), F32),
        compiler_params=_cparams(("arbitrary", "arbitrary")),
        name="ada_mod",
    )(c_pad, w_ada, b_ada.reshape(DEPTH, 1, n))
    return out[:, :b].reshape(DEPTH, b, 9, d)


FFN_TM = 512
FFN_TF = 256


def _ffn_kernel(x_ref, mod_ref, g_ref, wup_ref, wdn_ref, o_ref, act_ref, *, k0):
    x = x_ref[0]
    h = _norm_mod(x, g_ref[...], mod_ref[0, k0:k0 + 1, :], mod_ref[0, k0 + 1:k0 + 2, :])
    hb = h.astype(BF16)
    for j in range(D_FF // FFN_TF):
        lo = j * FFN_TF
        gate = jnp.dot(hb, wup_ref[:, lo:lo + FFN_TF], preferred_element_type=F32)
        up = jnp.dot(hb, wup_ref[:, D_FF + lo:D_FF + lo + FFN_TF], preferred_element_type=F32)
        act_ref[:, lo:lo + FFN_TF] = (gate * _sigmoid(gate) * up).astype(BF16)
    y = jnp.dot(act_ref[...], wdn_ref[...], preferred_element_type=F32)
    o_ref[0] = x + (0.5 * mod_ref[0, k0 + 2:k0 + 3, :]) * y


def _ffn(x, mod_l, g, w_up, w_dn, k0):
    b, s, d = x.shape
    tok = pl.BlockSpec((1, FFN_TM, d), lambda i, j: (i, j, 0))
    return pl.pallas_call(
        functools.partial(_ffn_kernel, k0=k0),
        grid=(b, s // FFN_TM),
        in_specs=[
            tok,
            pl.BlockSpec((1, 9, d), lambda i, j: (i, 0, 0)),
            _resident((1, d)),
            _resident((d, 2 * D_FF)),
            _resident((D_FF, d)),
        ],
        out_specs=tok,
        out_shape=jax.ShapeDtypeStruct(x.shape, F32),
        scratch_shapes=[pltpu.VMEM((FFN_TM, D_FF), BF16)],
        compiler_params=_cparams(("arbitrary", "arbitrary")),
        name="ffn",
    )(x, mod_l, g.reshape(1, d), w_up, w_dn)


PROJ_TM = 512


def _head_rmsnorm(acc, gain):
    rows = acc.shape[0]
    low = lax.broadcasted_iota(jnp.int32, (rows, LANES), 1) < HEAD_DIM
    outs = []
    for c in range(acc.shape[1] // LANES):
        xc = acc[:, c * LANES:(c + 1) * LANES]
        sq = xc * xc
        s_lo = jnp.sum(jnp.where(low, sq, 0.0), axis=-1, keepdims=True)
        s_hi = jnp.sum(jnp.where(low, 0.0, sq), axis=-1, keepdims=True)
        r_lo = lax.rsqrt(s_lo * (1.0 / HEAD_DIM) + EPS)
        r_hi = lax.rsqrt(s_hi * (1.0 / HEAD_DIM) + EPS)
        outs.append(xc * jnp.where(low, r_lo, r_hi) * gain[:, c * LANES:(c + 1) * LANES])
    return jnp.concatenate(outs, axis=-1)


def _proj_kernel(x_ref, mod_ref, g_ref, w_ref, gain_ref,
                 qa_ref, ka_ref, va_ref, qb_ref, kb_ref, vbt_ref, u_ref):
    x = x_ref[0]
    h = _norm_mod(x, g_ref[...], mod_ref[0, 3:4, :], mod_ref[0, 4:5, :])
    hb = h.astype(BF16)
    w = MIX_W

    def col(j):
        return jnp.dot(hb, w_ref[:, j * w:(j + 1) * w], preferred_element_type=F32)

    q_scale = 1.0 / math.sqrt(HEAD_DIM)
    qa_ref[0] = (_head_rmsnorm(col(0), gain_ref[0:1, :]) * (q_scale * LOG2E)).astype(BF16)
    ka_ref[0] = _head_rmsnorm(col(1), gain_ref[1:2, :]).astype(BF16)
    va_ref[0] = col(2).astype(BF16)
    qb_ref[0] = (_head_rmsnorm(col(3), gain_ref[2:3, :]) * (q_scale * LOG2E)).astype(BF16)
    kb_ref[0] = _head_rmsnorm(col(4), gain_ref[3:4, :]).astype(BF16)
    vbt_ref[0] = col(5).T.astype(BF16)
    u_ref[0] = col(6) * _sigmoid(col(7))


def _proj(x, mod_l, g, w_in, gains):
    b, s, d = x.shape
    tok_in = pl.BlockSpec((1, PROJ_TM, d), lambda i, j: (i, j, 0))
    tok_out = pl.BlockSpec((1, PROJ_TM, MIX_W), lambda i, j: (i, j, 0))
    bf = jax.ShapeDtypeStruct((b, s, MIX_W), BF16)
    tr_out = pl.BlockSpec((1, MIX_W, PROJ_TM), lambda i, j: (i, 0, j))
    return pl.pallas_call(
        _proj_kernel,
        grid=(b, s // PROJ_TM),
        in_specs=[
            tok_in,
            pl.BlockSpec((1, 9, d), lambda i, j: (i, 0, 0)),
            _resident((1, d)),
            _resident(w_in.shape),
            _resident(gains.shape),
        ],
        out_specs=[tok_out] * 5 + [tr_out, tok_out],
        out_shape=[bf] * 5 + [jax.ShapeDtypeStruct((b, MIX_W, s), BF16),
                              jax.ShapeDtypeStruct((b, s, MIX_W), F32)],
        compiler_params=_cparams(("arbitrary", "arbitrary")),
        name="proj",
    )(x, mod_l, g.reshape(1, d), w_in, gains)


def _bucket_thresholds():
    max_exact = N_BUCKETS // 2
    d = np.arange(0, 2 * REL_MAX_DIST + 2)
    df = np.maximum(d.astype(np.float32), np.float32(1.0))
    large = max_exact + (np.log(df / np.float32(max_exact))
                         / np.float32(math.log(REL_MAX_DIST / max_exact))
                         * np.float32(N_BUCKETS - max_exact)).astype(np.int32)
    bucket = np.where(d < max_exact, d, np.minimum(large, N_BUCKETS - 1))
    return [int(np.argmax(bucket >= b)) for b in range(1, N_BUCKETS)]


_THRESHOLDS = _bucket_thresholds()


def _bias_of_dist(dist, tab_ref, head):
    val = jnp.full(dist.shape, tab_ref[0, head], F32)
    for b in range(1, N_BUCKETS):
        val = jnp.where(dist >= _THRESHOLDS[b - 1], tab_ref[b, head], val)
    return val


DIFF_T = 256


def _diff_bias_tiles(seq):
    first_const = -(-(_THRESHOLDS[-1] + DIFF_T - 1) // DIFF_T)
    return min(seq // DIFF_T, first_const + 1)


def _bias_diff_kernel(tab_ref, o_ref):
    h = pl.program_id(0)
    delta = pl.program_id(1)
    j = lax.broadcasted_iota(jnp.int32, (DIFF_T, DIFF_T), 0)
    i = lax.broadcasted_iota(jnp.int32, (DIFF_T, DIFF_T), 1)
    dist = jnp.maximum(delta * DIFF_T + i - j, 0)
    o_ref[0, 0] = _bias_of_dist(dist, tab_ref, HA + h) * LOG2E


def _bias_dil_kernel(tab_ref, o_ref):
    p = pl.program_id(0)
    hp = pl.program_id(1)
    dil = jnp.where(p == 0, DIL_PATTERNS[0][1],
                    jnp.where(p == 1, DIL_PATTERNS[1][1], DIL_PATTERNS[2][1]))
    j = lax.broadcasted_iota(jnp.int32, (2 * BLK, BLK), 0)
    i = lax.broadcasted_iota(jnp.int32, (2 * BLK, BLK), 1)
    rel = i + BLK - j
    band = (rel >= 0) & (rel <= BLK)
    dist = jnp.maximum(rel, 0) * dil
    for sub in range(2):
        bias = _bias_of_dist(dist, tab_ref, 2 * hp + sub) * LOG2E
        lanes = slice(sub * BLK, (sub + 1) * BLK)
        o_ref[0, 0, 0, :, lanes] = jnp.where(band & (j >= BLK), bias, NEG)
        o_ref[0, 1, 0, :, lanes] = jnp.where(band, bias, NEG)


def _bias_tiles(rel_bias, seq):
    smem = pl.BlockSpec(memory_space=pltpu.SMEM)
    n_delta = _diff_bias_tiles(seq)
    bias_b = pl.pallas_call(
        _bias_diff_kernel,
        grid=(HB, n_delta),
        in_specs=[smem],
        out_specs=pl.BlockSpec((1, 1, DIFF_T, DIFF_T), lambda h, t: (h, t, 0, 0)),
        out_shape=jax.ShapeDtypeStruct((HB, n_delta, DIFF_T, DIFF_T), F32),
        compiler_params=_cparams(("arbitrary", "arbitrary")),
        name="bias_diff",
    )(rel_bias)
    n_pat = len(DIL_PATTERNS)
    bias_a = pl.pallas_call(
        _bias_dil_kernel,
        grid=(n_pat, HA // 2),
        in_specs=[smem],
        out_specs=pl.BlockSpec((1, 2, 1, 2 * BLK, 2 * BLK), lambda p, h: (p, 0, h, 0, 0)),
        out_shape=jax.ShapeDtypeStruct((n_pat, 2, HA // 2, 2 * BLK, 2 * BLK), F32),
        compiler_params=_cparams(("arbitrary", "arbitrary")),
        name="bias_dil",
    )(rel_bias)
    return bias_a, bias_b


DIL_CHAINS = 4
DIL_ONES = 16
DIL_COMBINE_ROWS = 512


def _dil_kernel(q_ref, k_ref, v_ref, bias_ref, y_ref,
                q32_ref, k32_ref, v32_ref, qd_ref, kd_ref, vtd_ref, ot_ref, lt_ref, on_ref, ln_ref):
    seq = q_ref.shape[1]
    pad = BLK
    q32_ref[...] = q_ref[0].astype(F32)
    k32_ref[...] = k_ref[0].astype(F32)
    v32_ref[...] = v_ref[0].astype(F32)
    kd_ref[0:pad, :] = jnp.zeros((pad, LANES), BF16)
    vtd_ref[:, 0:pad] = jnp.zeros((LANES, pad), BF16)
    low = lax.broadcasted_iota(jnp.int32, (BLK, LANES), 1) < HEAD_DIM
    top = lax.broadcasted_iota(jnp.int32, (LANES, BLK), 0) < HEAD_DIM
    ones = jnp.ones((DIL_ONES, 2 * BLK), BF16)

    for p, (_, dil) in enumerate(DIL_PATTERNS):
        sub_len = seq // dil
        nb = sub_len // BLK
        for r in range(dil):
            rows = pl.ds(r, sub_len, stride=dil)
            qd_ref[r * sub_len:(r + 1) * sub_len, :] = q32_ref[rows, :].astype(BF16)
            kd_ref[pad + r * sub_len:pad + (r + 1) * sub_len, :] = k32_ref[rows, :].astype(BF16)
            vtd_ref[:, pad + r * sub_len:pad + (r + 1) * sub_len] = v32_ref[rows, :].T.astype(BF16)

        def trip(j, carry, p=p, nb=nb):
            logits = []
            for c in range(DIL_CHAINS):
                chain = j * DIL_CHAINS + c
                base = pl.multiple_of(chain * BLK, BLK)
                var = jnp.where((chain & (nb - 1)) == 0, 0, 1)
                q = qd_ref[pl.ds(base, BLK), :]
                zero = jnp.zeros_like(q)
                qs = jnp.concatenate([jnp.where(low, q, zero), jnp.where(low, zero, q)], axis=0)
                kc = kd_ref[pl.ds(base, 2 * BLK), :]
                sc = lax.dot_general(kc, qs, (((1,), (1,)), ((), ())), preferred_element_type=F32)
                logits.append(sc + bias_ref[p, var])
            for c, sc in enumerate(logits):
                base = pl.multiple_of((j * DIL_CHAINS + c) * BLK, BLK)
                m = jnp.max(sc, axis=0, keepdims=True)
                pexp = jnp.exp2(sc - m).astype(BF16)
                vt = jnp.concatenate([vtd_ref[:, pl.ds(base, 2 * BLK)], ones], axis=0)
                pv = jnp.dot(vt, pexp, preferred_element_type=F32)
                den = pv[LANES:LANES + 1, :]
                o = pv[0:LANES, :] / den
                lse = m + jnp.log2(den)
                ot_ref[:, pl.ds(base, BLK)] = jnp.where(top, o[:, :BLK], o[:, BLK:])
                lt_ref[:, pl.ds(base, BLK)] = jnp.where(top, lse[:, :BLK], lse[:, BLK:])
            return carry

        lax.fori_loop(0, seq // BLK // DIL_CHAINS, trip, 0)
        for r in range(dil):
            rows = pl.ds(r, sub_len, stride=dil)
            on_ref[p, rows, :] = ot_ref[:, r * sub_len:(r + 1) * sub_len].T
            ln_ref[p, rows, :] = lt_ref[:, r * sub_len:(r + 1) * sub_len].T

    def combine(i, carry):
        rows = pl.ds(pl.multiple_of(i * DIL_COMBINE_ROWS, DIL_COMBINE_ROWS), DIL_COMBINE_ROWS)
        l1, l2, l3 = ln_ref[0, rows, :], ln_ref[1, rows, :], ln_ref[2, rows, :]
        mx = jnp.maximum(jnp.maximum(l1, l2), l3)
        e1 = jnp.exp2(l1 - mx)
        e2 = jnp.exp2(l2 - mx)
        e3 = jnp.exp2(l3 - mx)
        num = e1 * on_ref[0, rows, :] + e2 * on_ref[1, rows, :] + e3 * on_ref[2, rows, :]
        y_ref[0, rows, :] = (num / (e1 + e2 + e3)).astype(y_ref.dtype)
        return carry

    lax.fori_loop(0, seq // DIL_COMBINE_ROWS, combine, 0)


def _dilated_mixer(qa, ka, va, bias_a):
    b, s, w = qa.shape
    n_pat = len(DIL_PATTERNS)
    pair = pl.BlockSpec((1, s, LANES), lambda i, h: (i, 0, h))
    return pl.pallas_call(
        _dil_kernel,
        grid=(b, HA // 2),
        in_specs=[pair, pair, pair,
                  pl.BlockSpec((n_pat, 2, None, 2 * BLK, 2 * BLK), lambda i, h: (0, 0, h, 0, 0))],
        out_specs=pair,
        out_shape=jax.ShapeDtypeStruct((b, s, w), BF16),
        scratch_shapes=[
            pltpu.VMEM((s, LANES), F32), pltpu.VMEM((s, LANES), F32), pltpu.VMEM((s, LANES), F32),
            pltpu.VMEM((s, LANES), BF16), pltpu.VMEM((BLK + s, LANES), BF16),
            pltpu.VMEM((LANES, BLK + s), BF16),
            pltpu.VMEM((LANES, s), F32), pltpu.VMEM((LANES, s), F32),
            pltpu.VMEM((n_pat, s, LANES), F32), pltpu.VMEM((n_pat, s, LANES), F32),
        ],
        compiler_params=_cparams(("arbitrary", "arbitrary")),
        name="dilated",
    )(qa, ka, va, bias_a)


DIFF_TQ = 2 * DIFF_T
DIFF_ONES = 16


def _diff_kernel(q_ref, k_ref, vt_ref, bias_ref, lam_ref, sg_ref, o_ref,
                 qs_ref, m_ref, acc_ref, s0_ref, s1_ref, *, lam_init, n_bias):
    qi = pl.program_id(2)
    t = DIFF_T
    dv = 2 * HEAD_DIM
    n_blk = 2 * DIFF_TQ // t
    q = q_ref[0]
    low = lax.broadcasted_iota(jnp.int32, q.shape, 1) < HEAD_DIM
    zero = jnp.zeros_like(q)
    qs_ref[0:DIFF_TQ, :] = jnp.where(low, q, zero)
    qs_ref[DIFF_TQ:, :] = jnp.where(low, zero, q)
    m_ref[...] = jnp.full(m_ref.shape, NEG, F32)
    acc_ref[...] = jnp.zeros(acc_ref.shape, F32)
    ones = jnp.ones((DIFF_ONES, t), BF16)

    def produce(kt, s_ref, blocks):
        off = pl.multiple_of(kt * t, t)
        k = k_ref[0, pl.ds(off, t), :]
        for blk in blocks:
            lanes = slice(blk * t, (blk + 1) * t)
            s = lax.dot_general(k, qs_ref[lanes, :], (((1,), (1,)), ((), ())),
                                preferred_element_type=F32)
            delta = 2 * qi + blk % 2 - kt
            s_ref[:, lanes] = s + bias_ref[0, jnp.minimum(delta, n_bias - 1)]

    def consume(kt, s_ref, modes):
        off = pl.multiple_of(kt * t, t)
        vt = jnp.concatenate([vt_ref[0, :, pl.ds(off, t)], ones], axis=0)
        for blk, mode in enumerate(modes):
            if mode == "skip":
                continue
            lanes = slice(blk * t, (blk + 1) * t)
            s = s_ref[:, lanes]
            if mode == "diag":
                key = lax.broadcasted_iota(jnp.int32, (t, t), 0)
                qry = lax.broadcasted_iota(jnp.int32, (t, t), 1)
                s = jnp.where(qry >= key, s, NEG)
            m_old = m_ref[:, lanes]
            m_new = jnp.maximum(m_old, jnp.max(s, axis=0, keepdims=True))
            alpha = jnp.exp2(m_old - m_new)
            p = jnp.exp2(s - m_new).astype(BF16)
            pv = jnp.dot(vt, p, preferred_element_type=F32)
            acc_ref[:, lanes] = alpha * acc_ref[:, lanes] + pv
            m_ref[:, lanes] = m_new

    every = tuple(range(n_blk))
    full = ("full",) * n_blk
    produce(0, s0_ref, every)

    def body(j, carry):
        kt = 2 * j
        produce(kt + 1, s1_ref, every)
        consume(kt, s0_ref, full)
        produce(kt + 2, s0_ref, every)
        consume(kt + 1, s1_ref, full)
        return carry

    lax.fori_loop(0, qi, body, 0)
    produce(2 * qi + 1, s1_ref, (1, 3))
    consume(2 * qi, s0_ref, ("diag", "full", "diag", "full"))
    consume(2 * qi + 1, s1_ref, ("skip", "diag", "skip", "diag"))

    lv = lam_ref[...]
    s01 = jnp.sum(lv[0:1] * lv[1:2], axis=-1, keepdims=True)
    s23 = jnp.sum(lv[2:3] * lv[3:4], axis=-1, keepdims=True)
    lam = jnp.exp(s01) - jnp.exp(s23) + lam_init
    o_t = acc_ref[0:dv, :] / acc_ref[dv:dv + 1, :]
    ob = o_t[:, :DIFF_TQ] - lam * o_t[:, DIFF_TQ:]
    ms = jnp.mean(ob * ob, axis=0, keepdims=True)
    y = ob * lax.rsqrt(ms + EPS) * sg_ref[...]
    o_ref[0] = (y * (1.0 - lam_init)).T.astype(o_ref.dtype)


def _diff_attention(qb, kb, vbt, bias_b, lambda_vec, subln_g, layer):
    b, s, w = qb.shape
    t = DIFF_T
    dv = 2 * HEAD_DIM
    n_bias = bias_b.shape[1]
    lam_init = 0.8 - 0.6 * math.exp(-0.3 * layer)
    tile = pl.BlockSpec((1, DIFF_TQ, LANES), lambda i, h, n: (i, n, h))
    return pl.pallas_call(
        functools.partial(_diff_kernel, lam_init=lam_init, n_bias=n_bias),
        grid=(b, HB, s // DIFF_TQ),
        in_specs=[
            tile,
            pl.BlockSpec((1, s, LANES), lambda i, h, n: (i, 0, h)),
            pl.BlockSpec((1, dv, s), lambda i, h, n: (i, h, 0)),
            pl.BlockSpec((1, n_bias, t, t), lambda i, h, n: (h, 0, 0, 0)),
            pl.BlockSpec(lambda_vec.shape, lambda i, h, n: (0, 0)),
            pl.BlockSpec((dv, 1), lambda i, h, n: (0, 0)),
        ],
        out_specs=tile,
        out_shape=jax.ShapeDtypeStruct((b, s, w), BF16),
        scratch_shapes=[pltpu.VMEM((2 * DIFF_TQ, LANES), BF16),
                        pltpu.VMEM((1, 2 * DIFF_TQ), F32),
                        pltpu.VMEM((dv + DIFF_ONES, 2 * DIFF_TQ), F32),
                        pltpu.VMEM((t, 2 * DIFF_TQ), F32),
                        pltpu.VMEM((t, 2 * DIFF_TQ), F32)],
        compiler_params=_cparams(("arbitrary", "arbitrary", "arbitrary")),
        name="diff_attn",
    )(qb, kb, vbt, bias_b, lambda_vec, subln_g.reshape(dv, 1))


CONV_TS = 512
CONV_HALO = 32
CONV_ROWS = 64


def _conv_kernel(prev_ref, cur_ref, w_ref, cb_ref, lg_ref, lb_ref, o_ref, sh_ref, y_ref):
    j = pl.program_id(1)
    halo = prev_ref[0]
    total = CONV_HALO + CONV_TS
    sh_ref[0, 0:CONV_HALO, :] = jnp.where(j == 0, jnp.zeros_like(halo), halo)
    sh_ref[0, CONV_HALO:, :] = cur_ref[0]
    for b in range(1, SUBLANES):
        sh_ref[b, SUBLANES:, :] = sh_ref[0, SUBLANES - b:total - b, :]

    def chunk(r, carry):
        r0 = pl.multiple_of(r * CONV_ROWS, CONV_ROWS)
        for c in range(C_CONV // LANES):
            sl = slice(c * LANES, (c + 1) * LANES)
            acc = jnp.zeros((CONV_ROWS, LANES), F32)
            for tap in range(CONV_K):
                back = CONV_K - 1 - tap
                a, b = divmod(back, SUBLANES)
                rows = pl.ds(r0 + CONV_HALO - a * SUBLANES, CONV_ROWS)
                acc = acc + sh_ref[b, rows, sl] * w_ref[tap:tap + 1, sl]
            y_ref[pl.ds(r0, CONV_ROWS), sl] = acc
        return carry

    lax.fori_loop(0, CONV_TS // CONV_ROWS, chunk, 0)
    y = y_ref[...] + cb_ref[...]
    mu = jnp.mean(y, axis=-1, keepdims=True)
    var = jnp.mean(jnp.square(y - mu), axis=-1, keepdims=True)
    z = (y - mu) * lax.rsqrt(var + EPS) * lg_ref[...] + lb_ref[...]
    o_ref[0] = (z * _sigmoid(z)).astype(o_ref.dtype)


def _conv_module(u, conv_w, conv_b, ln_g, ln_b):
    b, s, c = u.shape
    per_tile = CONV_TS // CONV_HALO
    vec = pl.BlockSpec((1, c), lambda i, j: (0, 0))
    return pl.pallas_call(
        _conv_kernel,
        grid=(b, s // CONV_TS),
        in_specs=[
            pl.BlockSpec((1, CONV_HALO, c), lambda i, j: (i, jnp.maximum(j * per_tile - 1, 0), 0)),
            pl.BlockSpec((1, CONV_TS, c), lambda i, j: (i, j, 0)),
            pl.BlockSpec((CONV_K, c), lambda i, j: (0, 0)),
            vec, vec, vec,
        ],
        out_specs=pl.BlockSpec((1, CONV_TS, c), lambda i, j: (i, j, 0)),
        out_shape=jax.ShapeDtypeStruct((b, s, c), BF16),
        scratch_shapes=[pltpu.VMEM((SUBLANES, CONV_HALO + CONV_TS, c), F32),
                        pltpu.VMEM((CONV_TS, c), F32)],
        compiler_params=_cparams(("arbitrary", "arbitrary")),
        name="conv_module",
    )(u, u, conv_w, conv_b.reshape(1, c), ln_g.reshape(1, c), ln_b.reshape(1, c))


MERGE_TM = 512


def _merge_kernel(x_ref, mod_ref, g_ref, ya_ref, yb_ref, yc_ref,
                  wg_ref, bg_ref, wb_ref, wo_ref, o_ref):
    x = x_ref[0]
    d = x.shape[-1]
    h = _norm_mod(x, g_ref[...], mod_ref[0, 3:4, :], mod_ref[0, 4:5, :])
    hb = h.astype(BF16)
    mixed = jnp.zeros(x.shape, F32)
    for i, y_ref in enumerate((ya_ref, yb_ref, yc_ref)):
        z = jnp.dot(hb, wg_ref[:, i * d:(i + 1) * d], preferred_element_type=F32)
        gate = _sigmoid(z + bg_ref[:, i * d:(i + 1) * d])
        mixed = mixed + gate * jnp.dot(y_ref[0], wb_ref[i], preferred_element_type=F32)
    y = jnp.dot(mixed.astype(BF16), wo_ref[...], preferred_element_type=F32)
    o_ref[0] = x + mod_ref[0, 5:6, :] * y


def _merge(x, mod_l, g, y_a, y_b, y_c, w_gate, b_gate, w_branch, w_out):
    b, s, d = x.shape
    tok = pl.BlockSpec((1, MERGE_TM, d), lambda i, j: (i, j, 0))
    br = pl.BlockSpec((1, MERGE_TM, MIX_W), lambda i, j: (i, j, 0))
    return pl.pallas_call(
        _merge_kernel,
        grid=(b, s // MERGE_TM),
        in_specs=[
            tok,
            pl.BlockSpec((1, 9, d), lambda i, j: (i, 0, 0)),
            _resident((1, d)),
            br, br, br,
            _resident(w_gate.shape),
            _resident((1, N_BRANCH * d)),
            _resident(w_branch.shape),
            _resident(w_out.shape),
        ],
        out_specs=tok,
        out_shape=jax.ShapeDtypeStruct(x.shape, F32),
        compiler_params=_cparams(("arbitrary", "arbitrary")),
        name="merge",
    )(x, mod_l, g.reshape(1, d), y_a, y_b, y_c, w_gate, b_gate.reshape(1, N_BRANCH * d),
      w_branch, w_out)


def _qk_gain_rows(qk_gain):
    ga_q = jnp.tile(qk_gain[0], HA)
    ga_k = jnp.tile(qk_gain[1], HA)
    gb_q = jnp.tile(jnp.concatenate([qk_gain[2], qk_gain[3]]), HB)
    gb_k = jnp.tile(jnp.concatenate([qk_gain[4], qk_gain[5]]), HB)
    return jnp.stack([ga_q, ga_k, gb_q, gb_k])


def kernel(x, c, rel_bias, w_ada, b_ada, norm_g, w_ffn_in, w_ffn_out, w_in, qk_gain, lambda_vec,
           subln_g, conv_w, conv_b, conv_ln_g, conv_ln_b, w_branch, w_gate, b_gate, w_out):
    b, s, d = x.shape
    mod = _ada_mod(c, w_ada, b_ada)
    bias_a, bias_b = _bias_tiles(rel_bias, s)
    for l in range(DEPTH):
        mod_l = mod[l]
        x = _ffn(x, mod_l, norm_g[l, 0], w_ffn_in[l, 0].astype(BF16),
                 w_ffn_out[l, 0].astype(BF16), 0)
        qa, ka, va, qb, kb, vbt, u = _proj(x, mod_l, norm_g[l, 1], w_in[l].astype(BF16),
                                          _qk_gain_rows(qk_gain[l]))
        y_a = _dilated_mixer(qa, ka, va, bias_a)
        y_b = _diff_attention(qb, kb, vbt, bias_b, lambda_vec[l], subln_g[l], l)
        y_c = _conv_module(u, conv_w[l], conv_b[l], conv_ln_g[l], conv_ln_b[l])
        x = _merge(x, mod_l, norm_g[l, 1], y_a, y_b, y_c, w_gate[l].astype(BF16), b_gate[l],
                   w_branch[l].astype(BF16), w_out[l].astype(BF16))
        x = _ffn(x, mod_l, norm_g[l, 2], w_ffn_in[l, 1].astype(BF16),
                 w_ffn_out[l, 1].astype(BF16), 6)
    return x
```

```python
import functools
import math

import numpy as np
import jax
import jax.numpy as jnp
from jax import lax
from jax.experimental import pallas as pl
from jax.experimental.pallas import tpu as pltpu

D_MODEL = 1024
DEPTH = 2
HEAD_DIM = 64
HA = 8
DIL_PATTERNS = ((128, 1), (512, 4), (2048, 16))
HB = 4
C_CONV = 512
CONV_K = 31
D_FF = 2816
N_BUCKETS = 32
REL_MAX_DIST = 2048
BLK = 128
MIX_W = 512
N_BRANCH = 3
EPS = 1e-6
NEG = -1e30
LOG2E = math.log2(math.e)

LANES = 128
SUBLANES = 8
VMEM_LIMIT = 56 * 1024 * 1024

F32 = jnp.float32
BF16 = jnp.bfloat16


def _cparams(sem):
    return pltpu.CompilerParams(dimension_semantics=sem, vmem_limit_bytes=VMEM_LIMIT)


def _sigmoid(x):
    return 1.0 / (1.0 + jnp.exp(-x))


def _resident(shape):
    nd = len(shape)
    return pl.BlockSpec(shape, lambda *_: (0,) * nd, pipeline_mode=pl.Buffered(1))


def _norm_mod(x, g, shift, scale):
    ms = jnp.mean(x * x, axis=-1, keepdims=True)
    y = x * lax.rsqrt(ms + EPS) * g
    return y * (1.0 + scale) + shift


ADA_TN = 1152


def _ada_kernel(c_ref, w_ref, b_ref, o_ref):
    c = c_ref[...]
    a = c * _sigmoid(c)
    o_ref[0] = jnp.dot(a, w_ref[0], preferred_element_type=F32,
                       precision=lax.Precision.HIGHEST) + b_ref[0]


def _ada_mod(c, w_ada, b_ada):
    b, d = c.shape
    rows = 8
    c_pad = jnp.pad(c, ((0, rows - b), (0, 0)))
    n = w_ada.shape[-1]
    out = pl.pallas_call(
        _ada_kernel,
        grid=(DEPTH, n // ADA_TN),
        in_specs=[
            pl.BlockSpec((rows, d), lambda l, j: (0, 0)),
            pl.BlockSpec((1, d, ADA_TN), lambda l, j: (l, 0, j)),
            pl.BlockSpec((1, 1, ADA_TN), lambda l, j: (l, 0, j)),
        ],
        out_specs=pl.BlockSpec((1, rows, ADA_TN), lambda l, j: (l, 0, j)),
        out_shape=jax.ShapeDtypeStruct((DEPTH, rows, n), F32),
        compiler_params=_cparams(("arbitrary", "arbitrary")),
        name="ada_mod",
    )(c_pad, w_ada, b_ada.reshape(DEPTH, 1, n))
    return out[:, :b].reshape(DEPTH, b, 9, d)


FFN_TM = 512
FFN_TF = 256


def _ffn_kernel(x_ref, mod_ref, g_ref, wup_ref, wdn_ref, o_ref, act_ref, *, k0):
    x = x_ref[0]
    h = _norm_mod(x, g_ref[...], mod_ref[0, k0:k0 + 1, :], mod_ref[0, k0 + 1:k0 + 2, :])
    hb = h.astype(BF16)
    for j in range(D_FF // FFN_TF):
        lo = j * FFN_TF
        gate = jnp.dot(hb, wup_ref[:, lo:lo + FFN_TF], preferred_element_type=F32)
        up = jnp.dot(hb, wup_ref[:, D_FF + lo:D_FF + lo + FFN_TF], preferred_element_type=F32)
        act_ref[:, lo:lo + FFN_TF] = (gate * _sigmoid(gate) * up).astype(BF16)
    y = jnp.dot(act_ref[...], wdn_ref[...], preferred_element_type=F32)
    o_ref[0] = x + (0.5 * mod_ref[0, k0 + 2:k0 + 3, :]) * y


def _ffn(x, mod_l, g, w_up, w_dn, k0):
    b, s, d = x.shape
    tok = pl.BlockSpec((1, FFN_TM, d), lambda i, j: (i, j, 0))
    return pl.pallas_call(
        functools.partial(_ffn_kernel, k0=k0),
        grid=(b, s // FFN_TM),
        in_specs=[
            tok,
            pl.BlockSpec((1, 9, d), lambda i, j: (i, 0, 0)),
            _resident((1, d)),
            _resident((d, 2 * D_FF)),
            _resident((D_FF, d)),
        ],
        out_specs=tok,
        out_shape=jax.ShapeDtypeStruct(x.shape, F32),
        scratch_shapes=[pltpu.VMEM((FFN_TM, D_FF), BF16)],
        compiler_params=_cparams(("arbitrary", "arbitrary")),
        name="ffn",
    )(x, mod_l, g.reshape(1, d), w_up, w_dn)


PROJ_TM = 512


def _head_rmsnorm(acc, gain):
    rows = acc.shape[0]
    low = lax.broadcasted_iota(jnp.int32, (rows, LANES), 1) < HEAD_DIM
    outs = []
    for c in range(acc.shape[1] // LANES):
        xc = acc[:, c * LANES:(c + 1) * LANES]
        sq = xc * xc
        s_lo = jnp.sum(jnp.where(low, sq, 0.0), axis=-1, keepdims=True)
        s_hi = jnp.sum(jnp.where(low, 0.0, sq), axis=-1, keepdims=True)
        r_lo = lax.rsqrt(s_lo * (1.0 / HEAD_DIM) + EPS)
        r_hi = lax.rsqrt(s_hi * (1.0 / HEAD_DIM) + EPS)
        outs.append(xc * jnp.where(low, r_lo, r_hi) * gain[:, c * LANES:(c + 1) * LANES])
    return jnp.concatenate(outs, axis=-1)


def _proj_kernel(x_ref, mod_ref, g_ref, w_ref, gain_ref,
                 qa_ref, ka_ref, va_ref, qb_ref, kb_ref, vbt_ref, u_ref):
    x = x_ref[0]
    h = _norm_mod(x, g_ref[...], mod_ref[0, 3:4, :], mod_ref[0, 4:5, :])
    hb = h.astype(BF16)
    w = MIX_W

    def col(j):
        return jnp.dot(hb, w_ref[:, j * w:(j + 1) * w], preferred_element_type=F32)

    q_scale = 1.0 / math.sqrt(HEAD_DIM)
    qa_ref[0] = (_head_rmsnorm(col(0), gain_ref[0:1, :]) * (q_scale * LOG2E)).astype(BF16)
    ka_ref[0] = _head_rmsnorm(col(1), gain_ref[1:2, :]).astype(BF16)
    va_ref[0] = col(2).astype(BF16)
    qb_ref[0] = (_head_rmsnorm(col(3), gain_ref[2:3, :]) * (q_scale * LOG2E)).astype(BF16)
    kb_ref[0] = _head_rmsnorm(col(4), gain_ref[3:4, :]).astype(BF16)
    vbt_ref[0] = col(5).T.astype(BF16)
    u_ref[0] = col(6) * _sigmoid(col(7))


def _proj(x, mod_l, g, w_in, gains):
    b, s, d = x.shape
    tok_in = pl.BlockSpec((1, PROJ_TM, d), lambda i, j: (i, j, 0))
    tok_out = pl.BlockSpec((1, PROJ_TM, MIX_W), lambda i, j: (i, j, 0))
    bf = jax.ShapeDtypeStruct((b, s, MIX_W), BF16)
    tr_out = pl.BlockSpec((1, MIX_W, PROJ_TM), lambda i, j: (i, 0, j))
    return pl.pallas_call(
        _proj_kernel,
        grid=(b, s // PROJ_TM),
        in_specs=[
            tok_in,
            pl.BlockSpec((1, 9, d), lambda i, j: (i, 0, 0)),
            _resident((1, d)),
            _resident(w_in.shape),
            _resident(gains.shape),
        ],
        out_specs=[tok_out] * 5 + [tr_out, tok_out],
        out_shape=[bf] * 5 + [jax.ShapeDtypeStruct((b, MIX_W, s), BF16),
                              jax.ShapeDtypeStruct((b, s, MIX_W), F32)],
        compiler_params=_cparams(("arbitrary", "arbitrary")),
        name="proj",
    )(x, mod_l, g.reshape(1, d), w_in, gains)


def _bucket_thresholds():
    max_exact = N_BUCKETS // 2
    d = np.arange(0, 2 * REL_MAX_DIST + 2)
    df = np.maximum(d.astype(np.float32), np.float32(1.0))
    large = max_exact + (np.log(df / np.float32(max_exact))
                         / np.float32(math.log(REL_MAX_DIST / max_exact))
                         * np.float32(N_BUCKETS - max_exact)).astype(np.int32)
    bucket = np.where(d < max_exact, d, np.minimum(large, N_BUCKETS - 1))
    return [int(np.argmax(bucket >= b)) for b in range(1, N_BUCKETS)]


_THRESHOLDS = _bucket_thresholds()


def _bias_of_dist(dist, tab_ref, head):
    val = jnp.full(dist.shape, tab_ref[0, head], F32)
    for b in range(1, N_BUCKETS):
        val = jnp.where(dist >= _THRESHOLDS[b - 1], tab_ref[b, head], val)
    return val


DIFF_T = 256


def _diff_bias_tiles(seq):
    first_const = -(-(_THRESHOLDS[-1] + DIFF_T - 1) // DIFF_T)
    return min(seq // DIFF_T, first_const + 1)


def _bias_diff_kernel(tab_ref, o_ref):
    h = pl.program_id(0)
    delta = pl.program_id(1)
    j = lax.broadcasted_iota(jnp.int32, (DIFF_T, DIFF_T), 0)
    i = lax.broadcasted_iota(jnp.int32, (DIFF_T, DIFF_T), 1)
    dist = jnp.maximum(delta * DIFF_T + i - j, 0)
    o_ref[0, 0] = _bias_of_dist(dist, tab_ref, HA + h) * LOG2E


def _bias_dil_kernel(tab_ref, o_ref):
    p = pl.program_id(0)
    hp = pl.program_id(1)
    dil = jnp.where(p == 0, DIL_PATTERNS[0][1],
                    jnp.where(p == 1, DIL_PATTERNS[1][1], DIL_PATTERNS[2][1]))
    j = lax.broadcasted_iota(jnp.int32, (2 * BLK, BLK), 0)
    i = lax.broadcasted_iota(jnp.int32, (2 * BLK, BLK), 1)
    rel = i + BLK - j
    band = (rel >= 0) & (rel <= BLK)
    dist = jnp.maximum(rel, 0) * dil
    for sub in range(2):
        bias = _bias_of_dist(dist, tab_ref, 2 * hp + sub) * LOG2E
        lanes = slice(sub * BLK, (sub + 1) * BLK)
        o_ref[0, 0, 0, :, lanes] = jnp.where(band & (j >= BLK), bias, NEG)
        o_ref[0, 1, 0, :, lanes] = jnp.where(band, bias, NEG)


def _bias_tiles(rel_bias, seq):
    smem = pl.BlockSpec(memory_space=pltpu.SMEM)
    n_delta = _diff_bias_tiles(seq)
    bias_b = pl.pallas_call(
        _bias_diff_kernel,
        grid=(HB, n_delta),
        in_specs=[smem],
        out_specs=pl.BlockSpec((1, 1, DIFF_T, DIFF_T), lambda h, t: (h, t, 0, 0)),
        out_shape=jax.ShapeDtypeStruct((HB, n_delta, DIFF_T, DIFF_T), F32),
        compiler_params=_cparams(("arbitrary", "arbitrary")),
        name="bias_diff",
    )(rel_bias)
    n_pat = len(DIL_PATTERNS)
    bias_a = pl.pallas_call(
        _bias_dil_kernel,
        grid=(n_pat, HA // 2),
        in_specs=[smem],
        out_specs=pl.BlockSpec((1, 2, 1, 2 * BLK, 2 * BLK), lambda p, h: (p, 0, h, 0, 0)),
        out_shape=jax.ShapeDtypeStruct((n_pat, 2, HA // 2, 2 * BLK, 2 * BLK), F32),
        compiler_params=_cparams(("arbitrary", "arbitrary")),
        name="bias_dil",
    )(rel_bias)
    return bias_a, bias_b


DIL_CHAINS = 4
DIL_ONES = 16
DIL_COMBINE_ROWS = 512


def _dil_kernel(q_ref, k_ref, v_ref, bias_ref, y_ref,
                q32_ref, k32_ref, v32_ref, qd_ref, kd_ref, vtd_ref, ot_ref, lt_ref, on_ref, ln_ref,
                sa_ref, sb_ref):
    seq = q_ref.shape[1]
    pad = BLK
    q32_ref[...] = q_ref[0].astype(F32)
    k32_ref[...] = k_ref[0].astype(F32)
    v32_ref[...] = v_ref[0].astype(F32)
    kd_ref[0:pad, :] = jnp.zeros((pad, LANES), BF16)
    vtd_ref[:, 0:pad] = jnp.zeros((LANES, pad), BF16)
    low = lax.broadcasted_iota(jnp.int32, (BLK, LANES), 1) < HEAD_DIM
    top = lax.broadcasted_iota(jnp.int32, (LANES, BLK), 0) < HEAD_DIM
    ones = jnp.ones((DIL_ONES, 2 * BLK), BF16)

    for p, (_, dil) in enumerate(DIL_PATTERNS):
        sub_len = seq // dil
        nb = sub_len // BLK
        for r in range(dil):
            rows = pl.ds(r, sub_len, stride=dil)
            qd_ref[r * sub_len:(r + 1) * sub_len, :] = q32_ref[rows, :].astype(BF16)
            kd_ref[pad + r * sub_len:pad + (r + 1) * sub_len, :] = k32_ref[rows, :].astype(BF16)
            vtd_ref[:, pad + r * sub_len:pad + (r + 1) * sub_len] = v32_ref[rows, :].T.astype(BF16)

        def produce(j, s_ref, p=p, nb=nb):
            for c in range(DIL_CHAINS):
                chain = j * DIL_CHAINS + c
                base = pl.multiple_of(chain * BLK, BLK)
                var = jnp.where((chain & (nb - 1)) == 0, 0, 1)
                q = qd_ref[pl.ds(base, BLK), :]
                zero = jnp.zeros_like(q)
                qs = jnp.concatenate([jnp.where(low, q, zero), jnp.where(low, zero, q)], axis=0)
                kc = kd_ref[pl.ds(base, 2 * BLK), :]
                sc = lax.dot_general(kc, qs, (((1,), (1,)), ((), ())), preferred_element_type=F32)
                s_ref[:, c * 2 * BLK:(c + 1) * 2 * BLK] = sc + bias_ref[p, var]

        def consume(j, s_ref):
            for c in range(DIL_CHAINS):
                base = pl.multiple_of((j * DIL_CHAINS + c) * BLK, BLK)
                sc = s_ref[:, c * 2 * BLK:(c + 1) * 2 * BLK]
                m = jnp.max(sc, axis=0, keepdims=True)
                pexp = jnp.exp2(sc - m).astype(BF16)
                vt = jnp.concatenate([vtd_ref[:, pl.ds(base, 2 * BLK)], ones], axis=0)
                pv = jnp.dot(vt, pexp, preferred_element_type=F32)
                den = pv[LANES:LANES + 1, :]
                o = pv[0:LANES, :] / den
                lse = m + jnp.log2(den)
                ot_ref[:, pl.ds(base, BLK)] = jnp.where(top, o[:, :BLK], o[:, BLK:])
                lt_ref[:, pl.ds(base, BLK)] = jnp.where(top, lse[:, :BLK], lse[:, BLK:])

        n_trips = seq // BLK // DIL_CHAINS
        produce(0, sa_ref)

        def two_trips(i, carry, produce=produce, consume=consume):
            j = 2 * i
            produce(j + 1, sb_ref)
            consume(j, sa_ref)
            produce(j + 2, sa_ref)
            consume(j + 1, sb_ref)
            return carry

        lax.fori_loop(0, n_trips // 2 - 1, two_trips, 0)
        produce(n_trips - 1, sb_ref)
        consume(n_trips - 2, sa_ref)
        consume(n_trips - 1, sb_ref)
        for r in range(dil):
            rows = pl.ds(r, sub_len, stride=dil)
            on_ref[p, rows, :] = ot_ref[:, r * sub_len:(r + 1) * sub_len].T
            ln_ref[p, rows, :] = lt_ref[:, r * sub_len:(r + 1) * sub_len].T

    def combine(i, carry):
        rows = pl.ds(pl.multiple_of(i * DIL_COMBINE_ROWS, DIL_COMBINE_ROWS), DIL_COMBINE_ROWS)
        l1, l2, l3 = ln_ref[0, rows, :], ln_ref[1, rows, :], ln_ref[2, rows, :]
        mx = jnp.maximum(jnp.maximum(l1, l2), l3)
        e1 = jnp.exp2(l1 - mx)
        e2 = jnp.exp2(l2 - mx)
        e3 = jnp.exp2(l3 - mx)
        num = e1 * on_ref[0, rows, :] + e2 * on_ref[1, rows, :] + e3 * on_ref[2, rows, :]
        y_ref[0, rows, :] = (num / (e1 + e2 + e3)).astype(y_ref.dtype)
        return carry

    lax.fori_loop(0, seq // DIL_COMBINE_ROWS, combine, 0)


def _dilated_mixer(qa, ka, va, bias_a):
    b, s, w = qa.shape
    n_pat = len(DIL_PATTERNS)
    pair = pl.BlockSpec((1, s, LANES), lambda i, h: (i, 0, h))
    return pl.pallas_call(
        _dil_kernel,
        grid=(b, HA // 2),
        in_specs=[pair, pair, pair,
                  pl.BlockSpec((n_pat, 2, None, 2 * BLK, 2 * BLK), lambda i, h: (0, 0, h, 0, 0))],
        out_specs=pair,
        out_shape=jax.ShapeDtypeStruct((b, s, w), BF16),
        scratch_shapes=[
            pltpu.VMEM((s, LANES), F32), pltpu.VMEM((s, LANES), F32), pltpu.VMEM((s, LANES), F32),
            pltpu.VMEM((s, LANES), BF16), pltpu.VMEM((BLK + s, LANES), BF16),
            pltpu.VMEM((LANES, BLK + s), BF16),
            pltpu.VMEM((LANES, s), F32), pltpu.VMEM((LANES, s), F32),
            pltpu.VMEM((n_pat, s, LANES), F32), pltpu.VMEM((n_pat, s, LANES), F32),
            pltpu.VMEM((2 * BLK, DIL_CHAINS * 2 * BLK), F32),
            pltpu.VMEM((2 * BLK, DIL_CHAINS * 2 * BLK), F32),
        ],
        compiler_params=_cparams(("arbitrary", "arbitrary")),
        name="dilated",
    )(qa, ka, va, bias_a)


DIFF_TQ = 2 * DIFF_T
DIFF_ONES = 16


def _diff_kernel(q_ref, k_ref, vt_ref, bias_ref, lam_ref, sg_ref, o_ref,
                 qs_ref, m_ref, acc_ref, s0_ref, s1_ref, *, lam_init, n_bias):
    qi = pl.program_id(2)
    t = DIFF_T
    dv = 2 * HEAD_DIM
    n_blk = 2 * DIFF_TQ // t
    q = q_ref[0]
    low = lax.broadcasted_iota(jnp.int32, q.shape, 1) < HEAD_DIM
    zero = jnp.zeros_like(q)
    qs_ref[0:DIFF_TQ, :] = jnp.where(low, q, zero)
    qs_ref[DIFF_TQ:, :] = jnp.where(low, zero, q)
    m_ref[...] = jnp.full(m_ref.shape, NEG, F32)
    acc_ref[...] = jnp.zeros(acc_ref.shape, F32)
    ones = jnp.ones((DIFF_ONES, t), BF16)

    def produce(kt, s_ref, blocks):
        off = pl.multiple_of(kt * t, t)
        k = k_ref[0, pl.ds(off, t), :]
        for blk in blocks:
            lanes = slice(blk * t, (blk + 1) * t)
            s = lax.dot_general(k, qs_ref[lanes, :], (((1,), (1,)), ((), ())),
                                preferred_element_type=F32)
            delta = 2 * qi + blk % 2 - kt
            s_ref[:, lanes] = s + bias_ref[0, jnp.minimum(delta, n_bias - 1)]

    def consume(kt, s_ref, modes):
        off = pl.multiple_of(kt * t, t)
        vt = jnp.concatenate([vt_ref[0, :, pl.ds(off, t)], ones], axis=0)
        for blk, mode in enumerate(modes):
            if mode == "skip":
                continue
            lanes = slice(blk * t, (blk + 1) * t)
            s = s_ref[:, lanes]
            if mode == "diag":
                key = lax.broadcasted_iota(jnp.int32, (t, t), 0)
                qry = lax.broadcasted_iota(jnp.int32, (t, t), 1)
                s = jnp.where(qry >= key, s, NEG)
            m_old = m_ref[:, lanes]
            m_new = jnp.maximum(m_old, jnp.max(s, axis=0, keepdims=True))
            alpha = jnp.exp2(m_old - m_new)
            p = jnp.exp2(s - m_new).astype(BF16)
            pv = jnp.dot(vt, p, preferred_element_type=F32)
            acc_ref[:, lanes] = alpha * acc_ref[:, lanes] + pv
            m_ref[:, lanes] = m_new

    every = tuple(range(n_blk))
    full = ("full",) * n_blk
    produce(0, s0_ref, every)

    def body(j, carry):
        kt = 2 * j
        produce(kt + 1, s1_ref, every)
        consume(kt, s0_ref, full)
        produce(kt + 2, s0_ref, every)
        consume(kt + 1, s1_ref, full)
        return carry

    lax.fori_loop(0, qi, body, 0)
    produce(2 * qi + 1, s1_ref, (1, 3))
    consume(2 * qi, s0_ref, ("diag", "full", "diag", "full"))
    consume(2 * qi + 1, s1_ref, ("skip", "diag", "skip", "diag"))

    lv = lam_ref[...]
    s01 = jnp.sum(lv[0:1] * lv[1:2], axis=-1, keepdims=True)
    s23 = jnp.sum(lv[2:3] * lv[3:4], axis=-1, keepdims=True)
    lam = jnp.exp(s01) - jnp.exp(s23) + lam_init
    o_t = acc_ref[0:dv, :] / acc_ref[dv:dv + 1, :]
    ob = o_t[:, :DIFF_TQ] - lam * o_t[:, DIFF_TQ:]
    ms = jnp.mean(ob * ob, axis=0, keepdims=True)
    y = ob * lax.rsqrt(ms + EPS) * sg_ref[...]
    o_ref[0] = (y * (1.0 - lam_init)).T.astype(o_ref.dtype)


def _diff_attention(qb, kb, vbt, bias_b, lambda_vec, subln_g, layer):
    b, s, w = qb.shape
    t = DIFF_T
    dv = 2 * HEAD_DIM
    n_bias = bias_b.shape[1]
    lam_init = 0.8 - 0.6 * math.exp(-0.3 * layer)
    tile = pl.BlockSpec((1, DIFF_TQ, LANES), lambda i, h, n: (i, n, h))
    return pl.pallas_call(
        functools.partial(_diff_kernel, lam_init=lam_init, n_bias=n_bias),
        grid=(b, HB, s // DIFF_TQ),
        in_specs=[
            tile,
            pl.BlockSpec((1, s, LANES), lambda i, h, n: (i, 0, h)),
            pl.BlockSpec((1, dv, s), lambda i, h, n: (i, h, 0)),
            pl.BlockSpec((1, n_bias, t, t), lambda i, h, n: (h, 0, 0, 0)),
            pl.BlockSpec(lambda_vec.shape, lambda i, h, n: (0, 0)),
            pl.BlockSpec((dv, 1), lambda i, h, n: (0, 0)),
        ],
        out_specs=tile,
        out_shape=jax.ShapeDtypeStruct((b, s, w), BF16),
        scratch_shapes=[pltpu.VMEM((2 * DIFF_TQ, LANES), BF16),
                        pltpu.VMEM((1, 2 * DIFF_TQ), F32),
                        pltpu.VMEM((dv + DIFF_ONES, 2 * DIFF_TQ), F32),
                        pltpu.VMEM((t, 2 * DIFF_TQ), F32),
                        pltpu.VMEM((t, 2 * DIFF_TQ), F32)],
        compiler_params=_cparams(("arbitrary", "arbitrary", "arbitrary")),
        name="diff_attn",
    )(qb, kb, vbt, bias_b, lambda_vec, subln_g.reshape(dv, 1))


CONV_TS = 512
CONV_HALO = 32
CONV_ROWS = 32


def _conv_kernel(prev_ref, cur_ref, w_ref, cb_ref, lg_ref, lb_ref, o_ref, sh_ref, y_ref):
    j = pl.program_id(1)
    halo = prev_ref[0]
    total = CONV_HALO + CONV_TS
    sh_ref[0, 0:CONV_HALO, :] = jnp.where(j == 0, jnp.zeros_like(halo), halo)
    sh_ref[0, CONV_HALO:, :] = cur_ref[0]
    for b in range(1, SUBLANES):
        sh_ref[b] = pltpu.roll(sh_ref[0], b, axis=0)

    def chunk(r, carry):
        r0 = pl.multiple_of(r * CONV_ROWS, CONV_ROWS)
        acc = jnp.zeros((CONV_ROWS, C_CONV), F32)
        for tap in range(CONV_K):
            back = CONV_K - 1 - tap
            a, b = divmod(back, SUBLANES)
            rows = pl.ds(r0 + CONV_HALO - a * SUBLANES, CONV_ROWS)
            acc = acc + sh_ref[b, rows, :] * w_ref[tap:tap + 1, :]
        y_ref[pl.ds(r0, CONV_ROWS), :] = acc
        return carry

    lax.fori_loop(0, CONV_TS // CONV_ROWS, chunk, 0)
    y = y_ref[...] + cb_ref[...]
    mu = jnp.mean(y, axis=-1, keepdims=True)
    var = jnp.mean(jnp.square(y - mu), axis=-1, keepdims=True)
    z = (y - mu) * lax.rsqrt(var + EPS) * lg_ref[...] + lb_ref[...]
    o_ref[0] = (z * _sigmoid(z)).astype(o_ref.dtype)


def _conv_module(u, conv_w, conv_b, ln_g, ln_b):
    b, s, c = u.shape
    per_tile = CONV_TS // CONV_HALO
    vec = pl.BlockSpec((1, c), lambda i, j: (0, 0))
    return pl.pallas_call(
        _conv_kernel,
        grid=(b, s // CONV_TS),
        in_specs=[
            pl.BlockSpec((1, CONV_HALO, c), lambda i, j: (i, jnp.maximum(j * per_tile - 1, 0), 0)),
            pl.BlockSpec((1, CONV_TS, c), lambda i, j: (i, j, 0)),
            pl.BlockSpec((CONV_K, c), lambda i, j: (0, 0)),
            vec, vec, vec,
        ],
        out_specs=pl.BlockSpec((1, CONV_TS, c), lambda i, j: (i, j, 0)),
        out_shape=jax.ShapeDtypeStruct((b, s, c), BF16),
        scratch_shapes=[pltpu.VMEM((SUBLANES, CONV_HALO + CONV_TS, c), F32),
                        pltpu.VMEM((CONV_TS, c), F32)],
        compiler_params=_cparams(("arbitrary", "arbitrary")),
        name="conv_module",
    )(u, u, conv_w, conv_b.reshape(1, c), ln_g.reshape(1, c), ln_b.reshape(1, c))


MERGE_TM = 512


def _merge_kernel(x_ref, mod_ref, g_ref, ya_ref, yb_ref, yc_ref,
                  wg_ref, bg_ref, wb_ref, wo_ref, o_ref):
    x = x_ref[0]
    d = x.shape[-1]
    h = _norm_mod(x, g_ref[...], mod_ref[0, 3:4, :], mod_ref[0, 4:5, :])
    hb = h.astype(BF16)
    mixed = jnp.zeros(x.shape, F32)
    for i, y_ref in enumerate((ya_ref, yb_ref, yc_ref)):
        z = jnp.dot(hb, wg_ref[:, i * d:(i + 1) * d], preferred_element_type=F32)
        gate = _sigmoid(z + bg_ref[:, i * d:(i + 1) * d])
        mixed = mixed + gate * jnp.dot(y_ref[0], wb_ref[i], preferred_element_type=F32)
    y = jnp.dot(mixed.astype(BF16), wo_ref[...], preferred_element_type=F32)
    o_ref[0] = x + mod_ref[0, 5:6, :] * y


def _merge(x, mod_l, g, y_a, y_b, y_c, w_gate, b_gate, w_branch, w_out):
    b, s, d = x.shape
    tok = pl.BlockSpec((1, MERGE_TM, d), lambda i, j: (i, j, 0))
    br = pl.BlockSpec((1, MERGE_TM, MIX_W), lambda i, j: (i, j, 0))
    return pl.pallas_call(
        _merge_kernel,
        grid=(b, s // MERGE_TM),
        in_specs=[
            tok,
            pl.BlockSpec((1, 9, d), lambda i, j: (i, 0, 0)),
            _resident((1, d)),
            br, br, br,
            _resident(w_gate.shape),
            _resident((1, N_BRANCH * d)),
            _resident(w_branch.shape),
            _resident(w_out.shape),
        ],
        out_specs=tok,
        out_shape=jax.ShapeDtypeStruct(x.shape, F32),
        compiler_params=_cparams(("arbitrary", "arbitrary")),
        name="merge",
    )(x, mod_l, g.reshape(1, d), y_a, y_b, y_c, w_gate, b_gate.reshape(1, N_BRANCH * d),
      w_branch, w_out)


def _qk_gain_rows(qk_gain):
    ga_q = jnp.tile(qk_gain[0], HA)
    ga_k = jnp.tile(qk_gain[1], HA)
    gb_q = jnp.tile(jnp.concatenate([qk_gain[2], qk_gain[3]]), HB)
    gb_k = jnp.tile(jnp.concatenate([qk_gain[4], qk_gain[5]]), HB)
    return jnp.stack([ga_q, ga_k, gb_q, gb_k])


def kernel(x, c, rel_bias, w_ada, b_ada, norm_g, w_ffn_in, w_ffn_out, w_in, qk_gain, lambda_vec,
           subln_g, conv_w, conv_b, conv_ln_g, conv_ln_b, w_branch, w_gate, b_gate, w_out):
    b, s, d = x.shape
    mod = _ada_mod(c, w_ada, b_ada)
    bias_a, bias_b = _bias_tiles(rel_bias, s)
    for l in range(DEPTH):
        mod_l = mod[l]
        x = _ffn(x, mod_l, norm_g[l, 0], w_ffn_in[l, 0].astype(BF16),
                 w_ffn_out[l, 0].astype(BF16), 0)
        qa, ka, va, qb, kb, vbt, u = _proj(x, mod_l, norm_g[l, 1], w_in[l].astype(BF16),
                                          _qk_gain_rows(qk_gain[l]))
        y_a = _dilated_mixer(qa, ka, va, bias_a)
        y_b = _diff_attention(qb, kb, vbt, bias_b, lambda_vec[l], subln_g[l], l)
        y_c = _conv_module(u, conv_w[l], conv_b[l], conv_ln_g[l], conv_ln_b[l])
        x = _merge(x, mod_l, norm_g[l, 1], y_a, y_b, y_c, w_gate[l].astype(BF16), b_gate[l],
                   w_branch[l].astype(BF16), w_out[l].astype(BF16))
        x = _ffn(x, mod_l, norm_g[l, 2], w_ffn_in[l, 1].astype(BF16),
                 w_ffn_out[l, 1].astype(BF16), 6)
    return x
```

```python
import functools
import math

import numpy as np
import jax
import jax.numpy as jnp
from jax import lax
from jax.experimental import pallas as pl
from jax.experimental.pallas import tpu as pltpu

D_MODEL = 1024
DEPTH = 2
HEAD_DIM = 64
HA = 8
DIL_PATTERNS = ((128, 1), (512, 4), (2048, 16))
HB = 4
C_CONV = 512
CONV_K = 31
D_FF = 2816
N_BUCKETS = 32
REL_MAX_DIST = 2048
BLK = 128
MIX_W = 512
N_BRANCH = 3
EPS = 1e-6
NEG = -1e30
LOG2E = math.log2(math.e)

LANES = 128
SUBLANES = 8
VMEM_LIMIT = 56 * 1024 * 1024

F32 = jnp.float32
BF16 = jnp.bfloat16


def _cparams(sem):
    return pltpu.CompilerParams(dimension_semantics=sem, vmem_limit_bytes=VMEM_LIMIT)


def _sigmoid(x):
    return 1.0 / (1.0 + jnp.exp(-x))


def _resident(shape, lead=()):
    lead = tuple(lead)
    zeros = (0,) * len(shape)
    return pl.BlockSpec((None,) * len(lead) + tuple(shape), lambda *_: lead + zeros,
                        pipeline_mode=pl.Buffered(1))


def _norm_mod(x, g, shift, scale):
    ms = jnp.mean(x * x, axis=-1, keepdims=True)
    y = x * lax.rsqrt(ms + EPS) * g
    return y * (1.0 + scale) + shift


ADA_TN = 2304


def _ada_kernel(c_ref, w_ref, b_ref, o_ref):
    c = c_ref[...]
    a = c * _sigmoid(c)
    o_ref[0] = jnp.dot(a, w_ref[0], preferred_element_type=F32,
                       precision=lax.Precision.HIGHEST) + b_ref[0]


def _ada_mod(c, w_ada, b_ada):
    b, d = c.shape
    rows = 8
    c_pad = jnp.pad(c, ((0, rows - b), (0, 0)))
    n = w_ada.shape[-1]
    out = pl.pallas_call(
        _ada_kernel,
        grid=(DEPTH, n // ADA_TN),
        in_specs=[
            pl.BlockSpec((rows, d), lambda l, j: (0, 0)),
            pl.BlockSpec((1, d, ADA_TN), lambda l, j: (l, 0, j)),
            pl.BlockSpec((1, 1, ADA_TN), lambda l, j: (l, 0, j)),
        ],
        out_specs=pl.BlockSpec((1, rows, ADA_TN), lambda l, j: (l, 0, j)),
        out_shape=jax.ShapeDtypeStruct((DEPTH, rows, n), F32),
        compiler_params=_cparams(("arbitrary", "arbitrary")),
        name="ada_mod",
    )(c_pad, w_ada, b_ada.reshape(DEPTH, 1, n))
    return out[:, :b].reshape(DEPTH, b, 9, d)


FFN_TM = 512
FFN_TF = 256


def _ffn_kernel(x_ref, mod_ref, g_ref, wup_ref, wdn_ref, o_ref, act_ref, *, k0):
    x = x_ref[0]
    h = _norm_mod(x, g_ref[...], mod_ref[0, k0:k0 + 1, :], mod_ref[0, k0 + 1:k0 + 2, :])
    hb = h.astype(BF16)
    for j in range(D_FF // FFN_TF):
        lo = j * FFN_TF
        gate = jnp.dot(hb, wup_ref[:, lo:lo + FFN_TF], preferred_element_type=F32)
        up = jnp.dot(hb, wup_ref[:, D_FF + lo:D_FF + lo + FFN_TF], preferred_element_type=F32)
        act_ref[:, lo:lo + FFN_TF] = (gate * _sigmoid(gate) * up).astype(BF16)
    y = jnp.dot(act_ref[...], wdn_ref[...], preferred_element_type=F32)
    o_ref[0] = x + (0.5 * mod_ref[0, k0 + 2:k0 + 3, :]) * y


def _ffn(x, mod_l, g, w_up, w_dn, lead, k0):
    b, s, d = x.shape
    tok = pl.BlockSpec((1, FFN_TM, d), lambda i, j: (i, j, 0))
    return pl.pallas_call(
        functools.partial(_ffn_kernel, k0=k0),
        grid=(b, s // FFN_TM),
        in_specs=[
            tok,
            pl.BlockSpec((1, 9, d), lambda i, j: (i, 0, 0)),
            _resident((1, d)),
            _resident((d, 2 * D_FF), lead),
            _resident((D_FF, d), lead),
        ],
        out_specs=tok,
        out_shape=jax.ShapeDtypeStruct(x.shape, F32),
        scratch_shapes=[pltpu.VMEM((FFN_TM, D_FF), BF16)],
        compiler_params=_cparams(("arbitrary", "arbitrary")),
        name="ffn",
    )(x, mod_l, g.reshape(1, d), w_up, w_dn)


PROJ_TM = 512


def _head_rmsnorm(acc, gain):
    rows = acc.shape[0]
    low = lax.broadcasted_iota(jnp.int32, (rows, LANES), 1) < HEAD_DIM
    outs = []
    for c in range(acc.shape[1] // LANES):
        xc = acc[:, c * LANES:(c + 1) * LANES]
        sq = xc * xc
        s_lo = jnp.sum(jnp.where(low, sq, 0.0), axis=-1, keepdims=True)
        s_hi = jnp.sum(jnp.where(low, 0.0, sq), axis=-1, keepdims=True)
        r_lo = lax.rsqrt(s_lo * (1.0 / HEAD_DIM) + EPS)
        r_hi = lax.rsqrt(s_hi * (1.0 / HEAD_DIM) + EPS)
        outs.append(xc * jnp.where(low, r_lo, r_hi) * gain[:, c * LANES:(c + 1) * LANES])
    return jnp.concatenate(outs, axis=-1)


def _proj_kernel(x_ref, mod_ref, g_ref, w_ref, gain_ref,
                 qa_ref, ka_ref, va_ref, qb_ref, kb_ref, vbt_ref, u_ref):
    x = x_ref[0]
    h = _norm_mod(x, g_ref[...], mod_ref[0, 3:4, :], mod_ref[0, 4:5, :])
    hb = h.astype(BF16)
    w = MIX_W

    def col(j):
        return jnp.dot(hb, w_ref[:, j * w:(j + 1) * w], preferred_element_type=F32)

    q_scale = 1.0 / math.sqrt(HEAD_DIM)
    qa_ref[0] = (_head_rmsnorm(col(0), gain_ref[0:1, :]) * (q_scale * LOG2E)).astype(BF16)
    ka_ref[0] = _head_rmsnorm(col(1), gain_ref[1:2, :]).astype(BF16)
    va_ref[0] = col(2).astype(BF16)
    qb_ref[0] = (_head_rmsnorm(col(3), gain_ref[2:3, :]) * (q_scale * LOG2E)).astype(BF16)
    kb_ref[0] = _head_rmsnorm(col(4), gain_ref[3:4, :]).astype(BF16)
    vbt_ref[0] = col(5).T.astype(BF16)
    u_ref[0] = col(6) * _sigmoid(col(7))


def _proj(x, mod_l, g, w_in, layer, gains):
    b, s, d = x.shape
    tok_in = pl.BlockSpec((1, PROJ_TM, d), lambda i, j: (i, j, 0))
    tok_out = pl.BlockSpec((1, PROJ_TM, MIX_W), lambda i, j: (i, j, 0))
    bf = jax.ShapeDtypeStruct((b, s, MIX_W), BF16)
    tr_out = pl.BlockSpec((1, MIX_W, PROJ_TM), lambda i, j: (i, 0, j))
    return pl.pallas_call(
        _proj_kernel,
        grid=(b, s // PROJ_TM),
        in_specs=[
            tok_in,
            pl.BlockSpec((1, 9, d), lambda i, j: (i, 0, 0)),
            _resident((1, d)),
            _resident(w_in.shape[1:], (layer,)),
            _resident(gains.shape),
        ],
        out_specs=[tok_out] * 5 + [tr_out, tok_out],
        out_shape=[bf] * 5 + [jax.ShapeDtypeStruct((b, MIX_W, s), BF16),
                              jax.ShapeDtypeStruct((b, s, MIX_W), F32)],
        compiler_params=_cparams(("arbitrary", "arbitrary")),
        name="proj",
    )(x, mod_l, g.reshape(1, d), w_in, gains)


def _bucket_thresholds():
    max_exact = N_BUCKETS // 2
    d = np.arange(0, 2 * REL_MAX_DIST + 2)
    df = np.maximum(d.astype(np.float32), np.float32(1.0))
    large = max_exact + (np.log(df / np.float32(max_exact))
                         / np.float32(math.log(REL_MAX_DIST / max_exact))
                         * np.float32(N_BUCKETS - max_exact)).astype(np.int32)
    bucket = np.where(d < max_exact, d, np.minimum(large, N_BUCKETS - 1))
    return [int(np.argmax(bucket >= b)) for b in range(1, N_BUCKETS)]


_THRESHOLDS = _bucket_thresholds()


def _bias_of_dist(dist, tab_ref, head):
    val = jnp.full(dist.shape, tab_ref[0, head], F32)
    for b in range(1, N_BUCKETS):
        val = jnp.where(dist >= _THRESHOLDS[b - 1], tab_ref[b, head], val)
    return val


DIFF_T = 256


def _diff_bias_tiles(seq):
    first_const = -(-(_THRESHOLDS[-1] + DIFF_T - 1) // DIFF_T)
    return min(seq // DIFF_T, first_const + 1)


def _bias_diff_kernel(tab_ref, o_ref):
    h = pl.program_id(0)
    delta = pl.program_id(1)
    j = lax.broadcasted_iota(jnp.int32, (DIFF_T, DIFF_T), 0)
    i = lax.broadcasted_iota(jnp.int32, (DIFF_T, DIFF_T), 1)
    dist = jnp.maximum(delta * DIFF_T + i - j, 0)
    o_ref[0, 0] = _bias_of_dist(dist, tab_ref, HA + h) * LOG2E


def _bias_dil_kernel(tab_ref, o_ref):
    p = pl.program_id(0)
    hp = pl.program_id(1)
    dil = jnp.where(p == 0, DIL_PATTERNS[0][1],
                    jnp.where(p == 1, DIL_PATTERNS[1][1], DIL_PATTERNS[2][1]))
    j = lax.broadcasted_iota(jnp.int32, (2 * BLK, BLK), 0)
    i = lax.broadcasted_iota(jnp.int32, (2 * BLK, BLK), 1)
    rel = i + BLK - j
    band = (rel >= 0) & (rel <= BLK)
    dist = jnp.maximum(rel, 0) * dil
    for sub in range(2):
        bias = _bias_of_dist(dist, tab_ref, 2 * hp + sub) * LOG2E
        lanes = slice(sub * BLK, (sub + 1) * BLK)
        o_ref[0, 0, 0, :, lanes] = jnp.where(band & (j >= BLK), bias, NEG)
        o_ref[0, 1, 0, :, lanes] = jnp.where(band, bias, NEG)


def _bias_tiles(rel_bias, seq):
    smem = pl.BlockSpec(memory_space=pltpu.SMEM)
    n_delta = _diff_bias_tiles(seq)
    bias_b = pl.pallas_call(
        _bias_diff_kernel,
        grid=(HB, n_delta),
        in_specs=[smem],
        out_specs=pl.BlockSpec((1, 1, DIFF_T, DIFF_T), lambda h, t: (h, t, 0, 0)),
        out_shape=jax.ShapeDtypeStruct((HB, n_delta, DIFF_T, DIFF_T), F32),
        compiler_params=_cparams(("arbitrary", "arbitrary")),
        name="bias_diff",
    )(rel_bias)
    n_pat = len(DIL_PATTERNS)
    bias_a = pl.pallas_call(
        _bias_dil_kernel,
        grid=(n_pat, HA // 2),
        in_specs=[smem],
        out_specs=pl.BlockSpec((1, 2, 1, 2 * BLK, 2 * BLK), lambda p, h: (p, 0, h, 0, 0)),
        out_shape=jax.ShapeDtypeStruct((n_pat, 2, HA // 2, 2 * BLK, 2 * BLK), F32),
        compiler_params=_cparams(("arbitrary", "arbitrary")),
        name="bias_dil",
    )(rel_bias)
    return bias_a, bias_b


DIL_CHAINS = 4
DIL_ONES = 16
DIL_COMBINE_ROWS = 512


def _dil_kernel(q_ref, k_ref, v_ref, bias_ref, y_ref,
                q32_ref, k32_ref, v32_ref, qd_ref, kd_ref, vtd_ref, ot_ref, lt_ref, on_ref, ln_ref,
                sa_ref, sb_ref):
    seq = q_ref.shape[1]
    pad = BLK
    q32_ref[...] = q_ref[0].astype(F32)
    k32_ref[...] = k_ref[0].astype(F32)
    v32_ref[...] = v_ref[0].astype(F32)
    kd_ref[0:pad, :] = jnp.zeros((pad, LANES), BF16)
    vtd_ref[:, 0:pad] = jnp.zeros((LANES, pad), BF16)
    low = lax.broadcasted_iota(jnp.int32, (BLK, LANES), 1) < HEAD_DIM
    top = lax.broadcasted_iota(jnp.int32, (LANES, BLK), 0) < HEAD_DIM
    ones = jnp.ones((DIL_ONES, 2 * BLK), BF16)

    for p, (_, dil) in enumerate(DIL_PATTERNS):
        sub_len = seq // dil
        nb = sub_len // BLK
        for r in range(dil):
            rows = pl.ds(r, sub_len, stride=dil)
            qd_ref[r * sub_len:(r + 1) * sub_len, :] = q32_ref[rows, :].astype(BF16)
            kd_ref[pad + r * sub_len:pad + (r + 1) * sub_len, :] = k32_ref[rows, :].astype(BF16)
            vtd_ref[:, pad + r * sub_len:pad + (r + 1) * sub_len] = v32_ref[rows, :].T.astype(BF16)

        def produce(j, s_ref, p=p, nb=nb):
            for c in range(DIL_CHAINS):
                chain = j * DIL_CHAINS + c
                base = pl.multiple_of(chain * BLK, BLK)
                var = jnp.where((chain & (nb - 1)) == 0, 0, 1)
                q = qd_ref[pl.ds(base, BLK), :]
                zero = jnp.zeros_like(q)
                qs = jnp.concatenate([jnp.where(low, q, zero), jnp.where(low, zero, q)], axis=0)
                kc = kd_ref[pl.ds(base, 2 * BLK), :]
                sc = lax.dot_general(kc, qs, (((1,), (1,)), ((), ())), preferred_element_type=F32)
                s_ref[:, c * 2 * BLK:(c + 1) * 2 * BLK] = sc + bias_ref[p, var]

        def consume(j, s_ref):
            for c in range(DIL_CHAINS):
                base = pl.multiple_of((j * DIL_CHAINS + c) * BLK, BLK)
                sc = s_ref[:, c * 2 * BLK:(c + 1) * 2 * BLK]
                m = jnp.max(sc, axis=0, keepdims=True)
                pexp = jnp.exp2(sc - m).astype(BF16)
                vt = jnp.concatenate([vtd_ref[:, pl.ds(base, 2 * BLK)], ones], axis=0)
                pv = jnp.dot(vt, pexp, preferred_element_type=F32)
                den = pv[LANES:LANES + 1, :]
                o = pv[0:LANES, :] / den
                lse = m + jnp.log2(den)
                ot_ref[:, pl.ds(base, BLK)] = jnp.where(top, o[:, :BLK], o[:, BLK:])
                lt_ref[:, pl.ds(base, BLK)] = jnp.where(top, lse[:, :BLK], lse[:, BLK:])

        n_trips = seq // BLK // DIL_CHAINS
        produce(0, sa_ref)

        def two_trips(i, carry, produce=produce, consume=consume):
            j = 2 * i
            produce(j + 1, sb_ref)
            consume(j, sa_ref)
            produce(j + 2, sa_ref)
            consume(j + 1, sb_ref)
            return carry

        lax.fori_loop(0, n_trips // 2 - 1, two_trips, 0)
        produce(n_trips - 1, sb_ref)
        consume(n_trips - 2, sa_ref)
        consume(n_trips - 1, sb_ref)
        for r in range(dil):
            rows = pl.ds(r, sub_len, stride=dil)
            on_ref[p, rows, :] = ot_ref[:, r * sub_len:(r + 1) * sub_len].T
            ln_ref[p, rows, :] = lt_ref[:, r * sub_len:(r + 1) * sub_len].T

    def combine(i, carry):
        rows = pl.ds(pl.multiple_of(i * DIL_COMBINE_ROWS, DIL_COMBINE_ROWS), DIL_COMBINE_ROWS)
        l1, l2, l3 = ln_ref[0, rows, :], ln_ref[1, rows, :], ln_ref[2, rows, :]
        mx = jnp.maximum(jnp.maximum(l1, l2), l3)
        e1 = jnp.exp2(l1 - mx)
        e2 = jnp.exp2(l2 - mx)
        e3 = jnp.exp2(l3 - mx)
        num = e1 * on_ref[0, rows, :] + e2 * on_ref[1, rows, :] + e3 * on_ref[2, rows, :]
        y_ref[0, rows, :] = (num / (e1 + e2 + e3)).astype(y_ref.dtype)
        return carry

    lax.fori_loop(0, seq // DIL_COMBINE_ROWS, combine, 0)


def _dilated_mixer(qa, ka, va, bias_a):
    b, s, w = qa.shape
    n_pat = len(DIL_PATTERNS)
    pair = pl.BlockSpec((1, s, LANES), lambda i, h: (i, 0, h))
    return pl.pallas_call(
        _dil_kernel,
        grid=(b, HA // 2),
        in_specs=[pair, pair, pair,
                  pl.BlockSpec((n_pat, 2, None, 2 * BLK, 2 * BLK), lambda i, h: (0, 0, h, 0, 0))],
        out_specs=pair,
        out_shape=jax.ShapeDtypeStruct((b, s, w), BF16),
        scratch_shapes=[
            pltpu.VMEM((s, LANES), F32), pltpu.VMEM((s, LANES), F32), pltpu.VMEM((s, LANES), F32),
            pltpu.VMEM((s, LANES), BF16), pltpu.VMEM((BLK + s, LANES), BF16),
            pltpu.VMEM((LANES, BLK + s), BF16),
            pltpu.VMEM((LANES, s), F32), pltpu.VMEM((LANES, s), F32),
            pltpu.VMEM((n_pat, s, LANES), F32), pltpu.VMEM((n_pat, s, LANES), F32),
            pltpu.VMEM((2 * BLK, DIL_CHAINS * 2 * BLK), F32),
            pltpu.VMEM((2 * BLK, DIL_CHAINS * 2 * BLK), F32),
        ],
        compiler_params=_cparams(("arbitrary", "arbitrary")),
        name="dilated",
    )(qa, ka, va, bias_a)


DIFF_TQ = 2 * DIFF_T
DIFF_ONES = 16


def _diff_kernel(q_ref, k_ref, vt_ref, bias_ref, lam_ref, sg_ref, o_ref,
                 qs_ref, m_ref, acc_ref, s0_ref, s1_ref, *, lam_init, n_bias):
    qi = pl.program_id(2)
    t = DIFF_T
    dv = 2 * HEAD_DIM
    n_blk = 2 * DIFF_TQ // t
    q = q_ref[0]
    low = lax.broadcasted_iota(jnp.int32, q.shape, 1) < HEAD_DIM
    zero = jnp.zeros_like(q)
    qs_ref[0:DIFF_TQ, :] = jnp.where(low, q, zero)
    qs_ref[DIFF_TQ:, :] = jnp.where(low, zero, q)
    m_ref[...] = jnp.full(m_ref.shape, NEG, F32)
    acc_ref[...] = jnp.zeros(acc_ref.shape, F32)
    ones = jnp.ones((DIFF_ONES, t), BF16)

    def produce(kt, s_ref, blocks):
        off = pl.multiple_of(kt * t, t)
        k = k_ref[0, pl.ds(off, t), :]
        for blk in blocks:
            lanes = slice(blk * t, (blk + 1) * t)
            s = lax.dot_general(k, qs_ref[lanes, :], (((1,), (1,)), ((), ())),
                                preferred_element_type=F32)
            delta = 2 * qi + blk % 2 - kt
            s_ref[:, lanes] = s + bias_ref[0, jnp.minimum(delta, n_bias - 1)]

    def consume(kt, s_ref, modes):
        off = pl.multiple_of(kt * t, t)
        vt = jnp.concatenate([vt_ref[0, :, pl.ds(off, t)], ones], axis=0)
        for blk, mode in enumerate(modes):
            if mode == "skip":
                continue
            lanes = slice(blk * t, (blk + 1) * t)
            s = s_ref[:, lanes]
            if mode == "diag":
                key = lax.broadcasted_iota(jnp.int32, (t, t), 0)
                qry = lax.broadcasted_iota(jnp.int32, (t, t), 1)
                s = jnp.where(qry >= key, s, NEG)
            m_old = m_ref[:, lanes]
            m_new = jnp.maximum(m_old, jnp.max(s, axis=0, keepdims=True))
            alpha = jnp.exp2(m_old - m_new)
            p = jnp.exp2(s - m_new).astype(BF16)
            pv = jnp.dot(vt, p, preferred_element_type=F32)
            acc_ref[:, lanes] = alpha * acc_ref[:, lanes] + pv
            m_ref[:, lanes] = m_new

    every = tuple(range(n_blk))
    full = ("full",) * n_blk
    produce(0, s0_ref, every)

    def body(j, carry):
        kt = 2 * j
        produce(kt + 1, s1_ref, every)
        consume(kt, s0_ref, full)
        produce(kt + 2, s0_ref, every)
        consume(kt + 1, s1_ref, full)
        return carry

    lax.fori_loop(0, qi, body, 0)
    produce(2 * qi + 1, s1_ref, (1, 3))
    consume(2 * qi, s0_ref, ("diag", "full", "diag", "full"))
    consume(2 * qi + 1, s1_ref, ("skip", "diag", "skip", "diag"))

    lv = lam_ref[...]
    s01 = jnp.sum(lv[0:1] * lv[1:2], axis=-1, keepdims=True)
    s23 = jnp.sum(lv[2:3] * lv[3:4], axis=-1, keepdims=True)
    lam = jnp.exp(s01) - jnp.exp(s23) + lam_init
    o_t = acc_ref[0:dv, :] / acc_ref[dv:dv + 1, :]
    ob = o_t[:, :DIFF_TQ] - lam * o_t[:, DIFF_TQ:]
    ms = jnp.mean(ob * ob, axis=0, keepdims=True)
    y = ob * lax.rsqrt(ms + EPS) * sg_ref[...]
    o_ref[0] = (y * (1.0 - lam_init)).T.astype(o_ref.dtype)


def _diff_attention(qb, kb, vbt, bias_b, lambda_vec, subln_g, layer):
    b, s, w = qb.shape
    t = DIFF_T
    dv = 2 * HEAD_DIM
    n_bias = bias_b.shape[1]
    lam_init = 0.8 - 0.6 * math.exp(-0.3 * layer)
    tile = pl.BlockSpec((1, DIFF_TQ, LANES), lambda i, h, n: (i, n, h))
    return pl.pallas_call(
        functools.partial(_diff_kernel, lam_init=lam_init, n_bias=n_bias),
        grid=(b, HB, s // DIFF_TQ),
        in_specs=[
            tile,
            pl.BlockSpec((1, s, LANES), lambda i, h, n: (i, 0, h)),
            pl.BlockSpec((1, dv, s), lambda i, h, n: (i, h, 0)),
            pl.BlockSpec((1, n_bias, t, t), lambda i, h, n: (h, 0, 0, 0)),
            pl.BlockSpec(lambda_vec.shape, lambda i, h, n: (0, 0)),
            pl.BlockSpec((dv, 1), lambda i, h, n: (0, 0)),
        ],
        out_specs=tile,
        out_shape=jax.ShapeDtypeStruct((b, s, w), BF16),
        scratch_shapes=[pltpu.VMEM((2 * DIFF_TQ, LANES), BF16),
                        pltpu.VMEM((1, 2 * DIFF_TQ), F32),
                        pltpu.VMEM((dv + DIFF_ONES, 2 * DIFF_TQ), F32),
                        pltpu.VMEM((t, 2 * DIFF_TQ), F32),
                        pltpu.VMEM((t, 2 * DIFF_TQ), F32)],
        compiler_params=_cparams(("arbitrary", "arbitrary", "arbitrary")),
        name="diff_attn",
    )(qb, kb, vbt, bias_b, lambda_vec, subln_g.reshape(dv, 1))


MERGE_TM = 512
CONV_HALO = 32
CONV_ROWS = 32


def _conv_tile(prev_ref, cur_ref, w_ref, cb_ref, lg_ref, lb_ref, sh_ref, y_ref, first_tile):
    halo = prev_ref[0]
    rows_in = cur_ref.shape[1]
    sh_ref[0, 0:CONV_HALO, :] = jnp.where(first_tile, jnp.zeros_like(halo), halo)
    sh_ref[0, CONV_HALO:, :] = cur_ref[0]
    for b in range(1, SUBLANES):
        sh_ref[b] = pltpu.roll(sh_ref[0], b, axis=0)
    for r0 in range(0, rows_in, CONV_ROWS):
        acc = jnp.zeros((CONV_ROWS, C_CONV), F32)
        for tap in range(CONV_K):
            back = CONV_K - 1 - tap
            a, b = divmod(back, SUBLANES)
            lo = r0 + CONV_HALO - a * SUBLANES
            acc = acc + sh_ref[b, lo:lo + CONV_ROWS, :] * w_ref[tap:tap + 1, :]
        y_ref[r0:r0 + CONV_ROWS, :] = acc
    y = y_ref[...] + cb_ref[...]
    mu = jnp.mean(y, axis=-1, keepdims=True)
    var = jnp.mean(jnp.square(y - mu), axis=-1, keepdims=True)
    z = (y - mu) * lax.rsqrt(var + EPS) * lg_ref[...] + lb_ref[...]
    return z * _sigmoid(z)


def _merge_kernel(x_ref, mod_ref, g_ref, ya_ref, yb_ref, uprev_ref, u_ref,
                  cw_ref, cb_ref, lg_ref, lb_ref, wg_ref, bg_ref, wb_ref, wo_ref, o_ref,
                  sh_ref, yc_ref):
    x = x_ref[0]
    d = x.shape[-1]
    h = _norm_mod(x, g_ref[...], mod_ref[0, 3:4, :], mod_ref[0, 4:5, :])
    hb = h.astype(BF16)
    y_c = _conv_tile(uprev_ref, u_ref, cw_ref, cb_ref, lg_ref, lb_ref, sh_ref, yc_ref,
                     pl.program_id(1) == 0).astype(BF16)
    mixed = jnp.zeros(x.shape, F32)
    for i, y in enumerate((ya_ref[0], yb_ref[0], y_c)):
        z = jnp.dot(hb, wg_ref[:, i * d:(i + 1) * d], preferred_element_type=F32)
        gate = _sigmoid(z + bg_ref[:, i * d:(i + 1) * d])
        mixed = mixed + gate * jnp.dot(y, wb_ref[i], preferred_element_type=F32)
    y = jnp.dot(mixed.astype(BF16), wo_ref[...], preferred_element_type=F32)
    o_ref[0] = x + mod_ref[0, 5:6, :] * y


def _merge(x, mod_l, g, y_a, y_b, u, conv_w, conv_b, ln_g, ln_b, layer,
           w_gate, b_gate, w_branch, w_out):
    b, s, d = x.shape
    c = u.shape[-1]
    per_tile = MERGE_TM // CONV_HALO
    tok = pl.BlockSpec((1, MERGE_TM, d), lambda i, j: (i, j, 0))
    br = pl.BlockSpec((1, MERGE_TM, MIX_W), lambda i, j: (i, j, 0))
    return pl.pallas_call(
        _merge_kernel,
        grid=(b, s // MERGE_TM),
        in_specs=[
            tok,
            pl.BlockSpec((1, 9, d), lambda i, j: (i, 0, 0)),
            _resident((1, d)),
            br, br,
            pl.BlockSpec((1, CONV_HALO, c), lambda i, j: (i, jnp.maximum(j * per_tile - 1, 0), 0)),
            br,
            _resident((CONV_K, c)), _resident((1, c)), _resident((1, c)), _resident((1, c)),
            _resident(w_gate.shape[1:], (layer,)),
            _resident((1, N_BRANCH * d)),
            _resident(w_branch.shape[1:], (layer,)),
            _resident(w_out.shape[1:], (layer,)),
        ],
        out_specs=tok,
        out_shape=jax.ShapeDtypeStruct(x.shape, F32),
        scratch_shapes=[pltpu.VMEM((SUBLANES, CONV_HALO + MERGE_TM, c), F32),
                        pltpu.VMEM((MERGE_TM, c), F32)],
        compiler_params=_cparams(("arbitrary", "arbitrary")),
        name="merge",
    )(x, mod_l, g.reshape(1, d), y_a, y_b, u, u, conv_w, conv_b.reshape(1, c),
      ln_g.reshape(1, c), ln_b.reshape(1, c), w_gate, b_gate.reshape(1, N_BRANCH * d),
      w_branch, w_out)


def _qk_gain_rows(qk_gain):
    ga_q = jnp.tile(qk_gain[0], HA)
    ga_k = jnp.tile(qk_gain[1], HA)
    gb_q = jnp.tile(jnp.concatenate([qk_gain[2], qk_gain[3]]), HB)
    gb_k = jnp.tile(jnp.concatenate([qk_gain[4], qk_gain[5]]), HB)
    return jnp.stack([ga_q, ga_k, gb_q, gb_k])


def kernel(x, c, rel_bias, w_ada, b_ada, norm_g, w_ffn_in, w_ffn_out, w_in, qk_gain, lambda_vec,
           subln_g, conv_w, conv_b, conv_ln_g, conv_ln_b, w_branch, w_gate, b_gate, w_out):
    b, s, d = x.shape
    mod = _ada_mod(c, w_ada, b_ada)
    bias_a, bias_b = _bias_tiles(rel_bias, s)
    w_ffn_in, w_ffn_out, w_in = w_ffn_in.astype(BF16), w_ffn_out.astype(BF16), w_in.astype(BF16)
    w_gate, w_branch, w_out = w_gate.astype(BF16), w_branch.astype(BF16), w_out.astype(BF16)
    for l in range(DEPTH):
        mod_l = mod[l]
        x = _ffn(x, mod_l, norm_g[l, 0], w_ffn_in, w_ffn_out, (l, 0), 0)
        qa, ka, va, qb, kb, vbt, u = _proj(x, mod_l, norm_g[l, 1], w_in, l,
                                          _qk_gain_rows(qk_gain[l]))
        y_a = _dilated_mixer(qa, ka, va, bias_a)
        y_b = _diff_attention(qb, kb, vbt, bias_b, lambda_vec[l], subln_g[l], l)
        x = _merge(x, mod_l, norm_g[l, 1], y_a, y_b, u, conv_w[l], conv_b[l], conv_ln_g[l],
                   conv_ln_b[l], l, w_gate, b_gate[l], w_branch, w_out)
        x = _ffn(x, mod_l, norm_g[l, 2], w_ffn_in, w_ffn_out, (l, 1), 6)
    return x
```

```python
import functools
import math

import numpy as np
import jax
import jax.numpy as jnp
from jax import lax
from jax.experimental import pallas as pl
from jax.experimental.pallas import tpu as pltpu

D_MODEL = 1024
DEPTH = 2
HEAD_DIM = 64
HA = 8
DIL_PATTERNS = ((128, 1), (512, 4), (2048, 16))
HB = 4
C_CONV = 512
CONV_K = 31
D_FF = 2816
N_BUCKETS = 32
REL_MAX_DIST = 2048
BLK = 128
MIX_W = 512
N_BRANCH = 3
EPS = 1e-6
NEG = -1e30
LOG2E = math.log2(math.e)

LANES = 128
SUBLANES = 8
VMEM_LIMIT = 56 * 1024 * 1024

F32 = jnp.float32
BF16 = jnp.bfloat16


def _cparams(sem):
    return pltpu.CompilerParams(dimension_semantics=sem, vmem_limit_bytes=VMEM_LIMIT)


def _sigmoid(x):
    return 1.0 / (1.0 + jnp.exp(-x))


def _resident(shape, lead=()):
    lead = tuple(lead)
    zeros = (0,) * len(shape)
    return pl.BlockSpec((None,) * len(lead) + tuple(shape), lambda *_: lead + zeros,
                        pipeline_mode=pl.Buffered(1))


def _norm_mod(x, g, shift, scale):
    ms = jnp.mean(x * x, axis=-1, keepdims=True)
    y = x * lax.rsqrt(ms + EPS) * g
    return y * (1.0 + scale) + shift


ADA_TN = 2304


def _ada_kernel(c_ref, w_ref, b_ref, o_ref):
    c = c_ref[...]
    a = c * _sigmoid(c)
    o_ref[0] = jnp.dot(a, w_ref[0], preferred_element_type=F32,
                       precision=lax.Precision.HIGHEST) + b_ref[0]


def _ada_mod(c, w_ada, b_ada):
    b, d = c.shape
    rows = 8
    c_pad = jnp.pad(c, ((0, rows - b), (0, 0)))
    n = w_ada.shape[-1]
    out = pl.pallas_call(
        _ada_kernel,
        grid=(DEPTH, n // ADA_TN),
        in_specs=[
            pl.BlockSpec((rows, d), lambda l, j: (0, 0)),
            pl.BlockSpec((1, d, ADA_TN), lambda l, j: (l, 0, j)),
            pl.BlockSpec((1, 1, ADA_TN), lambda l, j: (l, 0, j)),
        ],
        out_specs=pl.BlockSpec((1, rows, ADA_TN), lambda l, j: (l, 0, j)),
        out_shape=jax.ShapeDtypeStruct((DEPTH, rows, n), F32),
        compiler_params=_cparams(("arbitrary", "arbitrary")),
        name="ada_mod",
    )(c_pad, w_ada, b_ada.reshape(DEPTH, 1, n))
    return out[:, :b].reshape(DEPTH, b, 9, d)


FFN_TM = 512
FFN_TF = 256


def _ffn_kernel(x_ref, mod_ref, g_ref, wup_ref, wdn_ref, o_ref, act_ref, *, k0):
    x = x_ref[0]
    h = _norm_mod(x, g_ref[...], mod_ref[0, k0:k0 + 1, :], mod_ref[0, k0 + 1:k0 + 2, :])
    hb = h.astype(BF16)
    for j in range(D_FF // FFN_TF):
        lo = j * FFN_TF
        gate = jnp.dot(hb, wup_ref[:, lo:lo + FFN_TF], preferred_element_type=F32)
        up = jnp.dot(hb, wup_ref[:, D_FF + lo:D_FF + lo + FFN_TF], preferred_element_type=F32)
        act_ref[:, lo:lo + FFN_TF] = (gate * _sigmoid(gate) * up).astype(BF16)
    y = jnp.dot(act_ref[...], wdn_ref[...], preferred_element_type=F32)
    o_ref[0] = x + (0.5 * mod_ref[0, k0 + 2:k0 + 3, :]) * y


def _ffn(x, mod_l, g, w_up, w_dn, lead, k0):
    b, s, d = x.shape
    tok = pl.BlockSpec((1, FFN_TM, d), lambda i, j: (i, j, 0))
    return pl.pallas_call(
        functools.partial(_ffn_kernel, k0=k0),
        grid=(b, s // FFN_TM),
        in_specs=[
            tok,
            pl.BlockSpec((1, 9, d), lambda i, j: (i, 0, 0)),
            _resident((1, d)),
            _resident((d, 2 * D_FF), lead),
            _resident((D_FF, d), lead),
        ],
        out_specs=tok,
        out_shape=jax.ShapeDtypeStruct(x.shape, F32),
        scratch_shapes=[pltpu.VMEM((FFN_TM, D_FF), BF16)],
        compiler_params=_cparams(("arbitrary", "arbitrary")),
        name="ffn",
    )(x, mod_l, g.reshape(1, d), w_up, w_dn)


PROJ_TM = 512


def _head_rmsnorm(acc, gain):
    rows = acc.shape[0]
    low = lax.broadcasted_iota(jnp.int32, (rows, LANES), 1) < HEAD_DIM
    outs = []
    for c in range(acc.shape[1] // LANES):
        xc = acc[:, c * LANES:(c + 1) * LANES]
        sq = xc * xc
        s_lo = jnp.sum(jnp.where(low, sq, 0.0), axis=-1, keepdims=True)
        s_hi = jnp.sum(jnp.where(low, 0.0, sq), axis=-1, keepdims=True)
        r_lo = lax.rsqrt(s_lo * (1.0 / HEAD_DIM) + EPS)
        r_hi = lax.rsqrt(s_hi * (1.0 / HEAD_DIM) + EPS)
        outs.append(xc * jnp.where(low, r_lo, r_hi) * gain[:, c * LANES:(c + 1) * LANES])
    return jnp.concatenate(outs, axis=-1)


def _proj_kernel(x_ref, mod_ref, g_ref, w_ref, gain_ref,
                 qa_ref, ka_ref, va_ref, qb_ref, kb_ref, vbt_ref, u_ref):
    x = x_ref[0]
    h = _norm_mod(x, g_ref[...], mod_ref[0, 3:4, :], mod_ref[0, 4:5, :])
    hb = h.astype(BF16)
    w = MIX_W

    def col(j):
        return jnp.dot(hb, w_ref[:, j * w:(j + 1) * w], preferred_element_type=F32)

    q_scale = 1.0 / math.sqrt(HEAD_DIM)
    qa_ref[0] = (_head_rmsnorm(col(0), gain_ref[0:1, :]) * (q_scale * LOG2E)).astype(BF16)
    ka_ref[0] = _head_rmsnorm(col(1), gain_ref[1:2, :]).astype(BF16)
    va_ref[0] = col(2).astype(BF16)
    qb_ref[0] = (_head_rmsnorm(col(3), gain_ref[2:3, :]) * (q_scale * LOG2E)).astype(BF16)
    kb_ref[0] = _head_rmsnorm(col(4), gain_ref[3:4, :]).astype(BF16)
    vbt_ref[0] = col(5).T.astype(BF16)
    u_ref[0] = col(6) * _sigmoid(col(7))


def _proj(x, mod_l, g, w_in, layer, gains):
    b, s, d = x.shape
    tok_in = pl.BlockSpec((1, PROJ_TM, d), lambda i, j: (i, j, 0))
    tok_out = pl.BlockSpec((1, PROJ_TM, MIX_W), lambda i, j: (i, j, 0))
    bf = jax.ShapeDtypeStruct((b, s, MIX_W), BF16)
    tr_out = pl.BlockSpec((1, MIX_W, PROJ_TM), lambda i, j: (i, 0, j))
    return pl.pallas_call(
        _proj_kernel,
        grid=(b, s // PROJ_TM),
        in_specs=[
            tok_in,
            pl.BlockSpec((1, 9, d), lambda i, j: (i, 0, 0)),
            _resident((1, d)),
            _resident(w_in.shape[1:], (layer,)),
            _resident(gains.shape),
        ],
        out_specs=[tok_out] * 5 + [tr_out, tok_out],
        out_shape=[bf] * 5 + [jax.ShapeDtypeStruct((b, MIX_W, s), BF16),
                              jax.ShapeDtypeStruct((b, s, MIX_W), F32)],
        compiler_params=_cparams(("arbitrary", "arbitrary")),
        name="proj",
    )(x, mod_l, g.reshape(1, d), w_in, gains)


def _bucket_thresholds():
    max_exact = N_BUCKETS // 2
    d = np.arange(0, 2 * REL_MAX_DIST + 2)
    df = np.maximum(d.astype(np.float32), np.float32(1.0))
    large = max_exact + (np.log(df / np.float32(max_exact))
                         / np.float32(math.log(REL_MAX_DIST / max_exact))
                         * np.float32(N_BUCKETS - max_exact)).astype(np.int32)
    bucket = np.where(d < max_exact, d, np.minimum(large, N_BUCKETS - 1))
    return [int(np.argmax(bucket >= b)) for b in range(1, N_BUCKETS)]


_THRESHOLDS = _bucket_thresholds()


def _bias_of_dist(dist, tab_ref, head):
    val = jnp.full(dist.shape, tab_ref[0, head], F32)
    for b in range(1, N_BUCKETS):
        val = jnp.where(dist >= _THRESHOLDS[b - 1], tab_ref[b, head], val)
    return val


DIFF_T = 256


def _diff_bias_tiles(seq):
    first_const = -(-(_THRESHOLDS[-1] + DIFF_T - 1) // DIFF_T)
    return min(seq // DIFF_T, first_const + 1)


def _bias_diff_kernel(tab_ref, o_ref):
    h = pl.program_id(0)
    delta = pl.program_id(1)
    j = lax.broadcasted_iota(jnp.int32, (DIFF_T, DIFF_T), 0)
    i = lax.broadcasted_iota(jnp.int32, (DIFF_T, DIFF_T), 1)
    dist = jnp.maximum(delta * DIFF_T + i - j, 0)
    o_ref[0, 0] = _bias_of_dist(dist, tab_ref, HA + h) * LOG2E


def _bias_dil_kernel(tab_ref, o_ref):
    p = pl.program_id(0)
    hp = pl.program_id(1)
    dil = jnp.where(p == 0, DIL_PATTERNS[0][1],
                    jnp.where(p == 1, DIL_PATTERNS[1][1], DIL_PATTERNS[2][1]))
    j = lax.broadcasted_iota(jnp.int32, (2 * BLK, BLK), 0)
    i = lax.broadcasted_iota(jnp.int32, (2 * BLK, BLK), 1)
    rel = i + BLK - j
    band = (rel >= 0) & (rel <= BLK)
    dist = jnp.maximum(rel, 0) * dil
    for sub in range(2):
        bias = _bias_of_dist(dist, tab_ref, 2 * hp + sub) * LOG2E
        lanes = slice(sub * BLK, (sub + 1) * BLK)
        o_ref[0, 0, 0, :, lanes] = jnp.where(band & (j >= BLK), bias, NEG)
        o_ref[0, 1, 0, :, lanes] = jnp.where(band, bias, NEG)


def _bias_tiles(rel_bias, seq):
    smem = pl.BlockSpec(memory_space=pltpu.SMEM)
    n_delta = _diff_bias_tiles(seq)
    bias_b = pl.pallas_call(
        _bias_diff_kernel,
        grid=(HB, n_delta),
        in_specs=[smem],
        out_specs=pl.BlockSpec((1, 1, DIFF_T, DIFF_T), lambda h, t: (h, t, 0, 0)),
        out_shape=jax.ShapeDtypeStruct((HB, n_delta, DIFF_T, DIFF_T), F32),
        compiler_params=_cparams(("arbitrary", "arbitrary")),
        name="bias_diff",
    )(rel_bias)
    n_pat = len(DIL_PATTERNS)
    bias_a = pl.pallas_call(
        _bias_dil_kernel,
        grid=(n_pat, HA // 2),
        in_specs=[smem],
        out_specs=pl.BlockSpec((1, 2, 1, 2 * BLK, 2 * BLK), lambda p, h: (p, 0, h, 0, 0)),
        out_shape=jax.ShapeDtypeStruct((n_pat, 2, HA // 2, 2 * BLK, 2 * BLK), F32),
        compiler_params=_cparams(("arbitrary", "arbitrary")),
        name="bias_dil",
    )(rel_bias)
    return bias_a, bias_b


DIL_CHAINS = 4
DIL_ONES = 16
DIL_COMBINE_ROWS = 512


def _dil_kernel(q_ref, k_ref, v_ref, bias_ref, y_ref,
                q32_ref, k32_ref, v32_ref, qd_ref, kd_ref, vtd_ref, ot_ref, lt_ref, on_ref, ln_ref,
                sa_ref, sb_ref):
    seq = q_ref.shape[1]
    pad = BLK
    q32_ref[...] = q_ref[0].astype(F32)
    k32_ref[...] = k_ref[0].astype(F32)
    v32_ref[...] = v_ref[0].astype(F32)
    kd_ref[0:pad, :] = jnp.zeros((pad, LANES), BF16)
    vtd_ref[:, 0:pad] = jnp.zeros((LANES, pad), BF16)
    low = lax.broadcasted_iota(jnp.int32, (BLK, LANES), 1) < HEAD_DIM
    top = lax.broadcasted_iota(jnp.int32, (LANES, BLK), 0) < HEAD_DIM
    ones = jnp.ones((DIL_ONES, 2 * BLK), BF16)

    for p, (_, dil) in enumerate(DIL_PATTERNS):
        sub_len = seq // dil
        nb = sub_len // BLK
        for r in range(dil):
            rows = pl.ds(r, sub_len, stride=dil)
            qd_ref[r * sub_len:(r + 1) * sub_len, :] = q32_ref[rows, :].astype(BF16)
            kd_ref[pad + r * sub_len:pad + (r + 1) * sub_len, :] = k32_ref[rows, :].astype(BF16)
            vtd_ref[:, pad + r * sub_len:pad + (r + 1) * sub_len] = v32_ref[rows, :].T.astype(BF16)

        def produce(j, s_ref, p=p, nb=nb):
            for c in range(DIL_CHAINS):
                chain = j * DIL_CHAINS + c
                base = pl.multiple_of(chain * BLK, BLK)
                var = jnp.where((chain & (nb - 1)) == 0, 0, 1)
                q = qd_ref[pl.ds(base, BLK), :]
                zero = jnp.zeros_like(q)
                qs = jnp.concatenate([jnp.where(low, q, zero), jnp.where(low, zero, q)], axis=0)
                kc = kd_ref[pl.ds(base, 2 * BLK), :]
                sc = lax.dot_general(kc, qs, (((1,), (1,)), ((), ())), preferred_element_type=F32)
                s_ref[:, c * 2 * BLK:(c + 1) * 2 * BLK] = sc + bias_ref[p, var]

        def consume(j, s_ref):
            for c in range(DIL_CHAINS):
                base = pl.multiple_of((j * DIL_CHAINS + c) * BLK, BLK)
                sc = s_ref[:, c * 2 * BLK:(c + 1) * 2 * BLK]
                m = jnp.max(sc, axis=0, keepdims=True)
                pexp = jnp.exp2(sc - m).astype(BF16)
                vt = jnp.concatenate([vtd_ref[:, pl.ds(base, 2 * BLK)], ones], axis=0)
                pv = jnp.dot(vt, pexp, preferred_element_type=F32)
                den = pv[LANES:LANES + 1, :]
                o = pv[0:LANES, :] / den
                lse = m + jnp.log2(den)
                ot_ref[:, pl.ds(base, BLK)] = jnp.where(top, o[:, :BLK], o[:, BLK:])
                lt_ref[:, pl.ds(base, BLK)] = jnp.where(top, lse[:, :BLK], lse[:, BLK:])

        n_trips = seq // BLK // DIL_CHAINS
        produce(0, sa_ref)

        def two_trips(i, carry, produce=produce, consume=consume):
            j = 2 * i
            produce(j + 1, sb_ref)
            consume(j, sa_ref)
            produce(j + 2, sa_ref)
            consume(j + 1, sb_ref)
            return carry

        lax.fori_loop(0, n_trips // 2 - 1, two_trips, 0)
        produce(n_trips - 1, sb_ref)
        consume(n_trips - 2, sa_ref)
        consume(n_trips - 1, sb_ref)
        for r in range(dil):
            rows = pl.ds(r, sub_len, stride=dil)
            on_ref[p, rows, :] = ot_ref[:, r * sub_len:(r + 1) * sub_len].T
            ln_ref[p, rows, :] = lt_ref[:, r * sub_len:(r + 1) * sub_len].T

    def combine(i, carry):
        rows = pl.ds(pl.multiple_of(i * DIL_COMBINE_ROWS, DIL_COMBINE_ROWS), DIL_COMBINE_ROWS)
        l1, l2, l3 = ln_ref[0, rows, :], ln_ref[1, rows, :], ln_ref[2, rows, :]
        mx = jnp.maximum(jnp.maximum(l1, l2), l3)
        e1 = jnp.exp2(l1 - mx)
        e2 = jnp.exp2(l2 - mx)
        e3 = jnp.exp2(l3 - mx)
        num = e1 * on_ref[0, rows, :] + e2 * on_ref[1, rows, :] + e3 * on_ref[2, rows, :]
        y_ref[0, rows, :] = (num / (e1 + e2 + e3)).astype(y_ref.dtype)
        return carry

    lax.fori_loop(0, seq // DIL_COMBINE_ROWS, combine, 0)


def _dilated_mixer(qa, ka, va, bias_a):
    b, s, w = qa.shape
    n_pat = len(DIL_PATTERNS)
    pair = pl.BlockSpec((1, s, LANES), lambda i, h: (i, 0, h))
    return pl.pallas_call(
        _dil_kernel,
        grid=(b, HA // 2),
        in_specs=[pair, pair, pair,
                  pl.BlockSpec((n_pat, 2, None, 2 * BLK, 2 * BLK), lambda i, h: (0, 0, h, 0, 0))],
        out_specs=pair,
        out_shape=jax.ShapeDtypeStruct((b, s, w), BF16),
        scratch_shapes=[
            pltpu.VMEM((s, LANES), F32), pltpu.VMEM((s, LANES), F32), pltpu.VMEM((s, LANES), F32),
            pltpu.VMEM((s, LANES), BF16), pltpu.VMEM((BLK + s, LANES), BF16),
            pltpu.VMEM((LANES, BLK + s), BF16),
            pltpu.VMEM((LANES, s), F32), pltpu.VMEM((LANES, s), F32),
            pltpu.VMEM((n_pat, s, LANES), F32), pltpu.VMEM((n_pat, s, LANES), F32),
            pltpu.VMEM((2 * BLK, DIL_CHAINS * 2 * BLK), F32),
            pltpu.VMEM((2 * BLK, DIL_CHAINS * 2 * BLK), F32),
        ],
        compiler_params=_cparams(("arbitrary", "arbitrary")),
        name="dilated",
    )(qa, ka, va, bias_a)


DIFF_TQ = 2 * DIFF_T
DIFF_HEADS = 2
DIFF_ONES = 16


def _diff_kernel(q_ref, k_ref, vt_ref, bias_ref, lam_ref, sg_ref, o_ref,
                 qs_ref, m_ref, acc_ref, s0_ref, s1_ref, *, lam_init, n_bias):
    qi = pl.program_id(2)
    t = DIFF_T
    dv = 2 * HEAD_DIM
    n_blk = 2 * DIFF_TQ // t
    width = 2 * DIFF_TQ
    low = lax.broadcasted_iota(jnp.int32, (DIFF_TQ, LANES), 1) < HEAD_DIM
    for hd in range(DIFF_HEADS):
        q = q_ref[0, :, hd * LANES:(hd + 1) * LANES]
        zero = jnp.zeros_like(q)
        qs_ref[hd * width:hd * width + DIFF_TQ, :] = jnp.where(low, q, zero)
        qs_ref[hd * width + DIFF_TQ:(hd + 1) * width, :] = jnp.where(low, zero, q)
    m_ref[...] = jnp.full(m_ref.shape, NEG, F32)
    acc_ref[...] = jnp.zeros(acc_ref.shape, F32)
    ones = jnp.ones((DIFF_ONES, t), BF16)

    def produce(kt, s_ref, blocks):
        off = pl.multiple_of(kt * t, t)
        for hd in range(DIFF_HEADS):
            k = k_ref[0, pl.ds(off, t), hd * LANES:(hd + 1) * LANES]
            for blk in blocks:
                lanes = slice(hd * width + blk * t, hd * width + (blk + 1) * t)
                s = lax.dot_general(k, qs_ref[lanes, :], (((1,), (1,)), ((), ())),
                                    preferred_element_type=F32)
                delta = 2 * qi + blk % 2 - kt
                s_ref[:, lanes] = s + bias_ref[hd, jnp.minimum(delta, n_bias - 1)]

    def consume(kt, s_ref, modes):
        off = pl.multiple_of(kt * t, t)
        for hd in range(DIFF_HEADS):
            vt = jnp.concatenate([vt_ref[0, hd * dv:(hd + 1) * dv, pl.ds(off, t)], ones], axis=0)
            for blk, mode in enumerate(modes):
                if mode == "skip":
                    continue
                lanes = slice(hd * width + blk * t, hd * width + (blk + 1) * t)
                s = s_ref[:, lanes]
                if mode == "diag":
                    key = lax.broadcasted_iota(jnp.int32, (t, t), 0)
                    qry = lax.broadcasted_iota(jnp.int32, (t, t), 1)
                    s = jnp.where(qry >= key, s, NEG)
                m_old = m_ref[:, lanes]
                m_new = jnp.maximum(m_old, jnp.max(s, axis=0, keepdims=True))
                alpha = jnp.exp2(m_old - m_new)
                p = jnp.exp2(s - m_new).astype(BF16)
                pv = jnp.dot(vt, p, preferred_element_type=F32)
                acc_ref[:, lanes] = alpha * acc_ref[:, lanes] + pv
                m_ref[:, lanes] = m_new

    every = tuple(range(n_blk))
    full = ("full",) * n_blk
    produce(0, s0_ref, every)

    def body(j, carry):
        kt = 2 * j
        produce(kt + 1, s1_ref, every)
        consume(kt, s0_ref, full)
        produce(kt + 2, s0_ref, every)
        consume(kt + 1, s1_ref, full)
        return carry

    lax.fori_loop(0, qi, body, 0)
    produce(2 * qi + 1, s1_ref, (1, 3))
    consume(2 * qi, s0_ref, ("diag", "full", "diag", "full"))
    consume(2 * qi + 1, s1_ref, ("skip", "diag", "skip", "diag"))

    lv = lam_ref[...]
    s01 = jnp.sum(lv[0:1] * lv[1:2], axis=-1, keepdims=True)
    s23 = jnp.sum(lv[2:3] * lv[3:4], axis=-1, keepdims=True)
    lam = jnp.exp(s01) - jnp.exp(s23) + lam_init
    for hd in range(DIFF_HEADS):
        cols = slice(hd * width, (hd + 1) * width)
        o_t = acc_ref[0:dv, cols] / acc_ref[dv:dv + 1, cols]
        ob = o_t[:, :DIFF_TQ] - lam * o_t[:, DIFF_TQ:]
        ms = jnp.mean(ob * ob, axis=0, keepdims=True)
        y = ob * lax.rsqrt(ms + EPS) * sg_ref[...]
        o_ref[0, :, hd * LANES:(hd + 1) * LANES] = (y * (1.0 - lam_init)).T.astype(o_ref.dtype)


def _diff_attention(qb, kb, vbt, bias_b, lambda_vec, subln_g, layer):
    b, s, w = qb.shape
    t = DIFF_T
    dv = 2 * HEAD_DIM
    n_bias = bias_b.shape[1]
    lam_init = 0.8 - 0.6 * math.exp(-0.3 * layer)
    heads = DIFF_HEADS
    tile = pl.BlockSpec((1, DIFF_TQ, heads * LANES), lambda i, h, n: (i, n, h))
    return pl.pallas_call(
        functools.partial(_diff_kernel, lam_init=lam_init, n_bias=n_bias),
        grid=(b, HB // heads, s // DIFF_TQ),
        in_specs=[
            tile,
            pl.BlockSpec((1, s, heads * LANES), lambda i, h, n: (i, 0, h)),
            pl.BlockSpec((1, heads * dv, s), lambda i, h, n: (i, h, 0)),
            pl.BlockSpec((heads, n_bias, t, t), lambda i, h, n: (h, 0, 0, 0)),
            pl.BlockSpec(lambda_vec.shape, lambda i, h, n: (0, 0)),
            pl.BlockSpec((dv, 1), lambda i, h, n: (0, 0)),
        ],
        out_specs=tile,
        out_shape=jax.ShapeDtypeStruct((b, s, w), BF16),
        scratch_shapes=[pltpu.VMEM((heads * 2 * DIFF_TQ, LANES), BF16),
                        pltpu.VMEM((1, heads * 2 * DIFF_TQ), F32),
                        pltpu.VMEM((dv + DIFF_ONES, heads * 2 * DIFF_TQ), F32),
                        pltpu.VMEM((t, heads * 2 * DIFF_TQ), F32),
                        pltpu.VMEM((t, heads * 2 * DIFF_TQ), F32)],
        compiler_params=_cparams(("arbitrary", "arbitrary", "arbitrary")),
        name="diff_attn",
    )(qb, kb, vbt, bias_b, lambda_vec, subln_g.reshape(dv, 1))


MERGE_TM = 512
CONV_HALO = 32
CONV_ROWS = 128


def _conv_tile(prev_ref, cur_ref, w_ref, cb_ref, lg_ref, lb_ref, sh_ref, y_ref, first_tile):
    halo = prev_ref[0]
    rows_in = cur_ref.shape[1]
    sh_ref[0, 0:CONV_HALO, :] = jnp.where(first_tile, jnp.zeros_like(halo), halo)
    sh_ref[0, CONV_HALO:, :] = cur_ref[0]
    for b in range(1, SUBLANES):
        sh_ref[b] = pltpu.roll(sh_ref[0], b, axis=0)
    for r0 in range(0, rows_in, CONV_ROWS):
        acc = jnp.zeros((CONV_ROWS, C_CONV), F32)
        for tap in range(CONV_K):
            back = CONV_K - 1 - tap
            a, b = divmod(back, SUBLANES)
            lo = r0 + CONV_HALO - a * SUBLANES
            acc = acc + sh_ref[b, lo:lo + CONV_ROWS, :] * w_ref[tap:tap + 1, :]
        y_ref[r0:r0 + CONV_ROWS, :] = acc
    y = y_ref[...] + cb_ref[...]
    mu = jnp.mean(y, axis=-1, keepdims=True)
    var = jnp.mean(jnp.square(y - mu), axis=-1, keepdims=True)
    z = (y - mu) * lax.rsqrt(var + EPS) * lg_ref[...] + lb_ref[...]
    return z * _sigmoid(z)


def _merge_kernel(x_ref, mod_ref, g_ref, ya_ref, yb_ref, uprev_ref, u_ref,
                  cw_ref, cb_ref, lg_ref, lb_ref, wg_ref, bg_ref, wb_ref, wo_ref, o_ref,
                  sh_ref, yc_ref):
    x = x_ref[0]
    d = x.shape[-1]
    h = _norm_mod(x, g_ref[...], mod_ref[0, 3:4, :], mod_ref[0, 4:5, :])
    hb = h.astype(BF16)
    y_c = _conv_tile(uprev_ref, u_ref, cw_ref, cb_ref, lg_ref, lb_ref, sh_ref, yc_ref,
                     pl.program_id(1) == 0).astype(BF16)
    mixed = jnp.zeros(x.shape, F32)
    for i, y in enumerate((ya_ref[0], yb_ref[0], y_c)):
        z = jnp.dot(hb, wg_ref[:, i * d:(i + 1) * d], preferred_element_type=F32)
        gate = _sigmoid(z + bg_ref[:, i * d:(i + 1) * d])
        mixed = mixed + gate * jnp.dot(y, wb_ref[i], preferred_element_type=F32)
    y = jnp.dot(mixed.astype(BF16), wo_ref[...], preferred_element_type=F32)
    o_ref[0] = x + mod_ref[0, 5:6, :] * y


def _merge(x, mod_l, g, y_a, y_b, u, conv_w, conv_b, ln_g, ln_b, layer,
           w_gate, b_gate, w_branch, w_out):
    b, s, d = x.shape
    c = u.shape[-1]
    per_tile = MERGE_TM // CONV_HALO
    tok = pl.BlockSpec((1, MERGE_TM, d), lambda i, j: (i, j, 0))
    br = pl.BlockSpec((1, MERGE_TM, MIX_W), lambda i, j: (i, j, 0))
    return pl.pallas_call(
        _merge_kernel,
        grid=(b, s // MERGE_TM),
        in_specs=[
            tok,
            pl.BlockSpec((1, 9, d), lambda i, j: (i, 0, 0)),
            _resident((1, d)),
            br, br,
            pl.BlockSpec((1, CONV_HALO, c), lambda i, j: (i, jnp.maximum(j * per_tile - 1, 0), 0)),
            br,
            _resident((CONV_K, c)), _resident((1, c)), _resident((1, c)), _resident((1, c)),
            _resident(w_gate.shape[1:], (layer,)),
            _resident((1, N_BRANCH * d)),
            _resident(w_branch.shape[1:], (layer,)),
            _resident(w_out.shape[1:], (layer,)),
        ],
        out_specs=tok,
        out_shape=jax.ShapeDtypeStruct(x.shape, F32),
        scratch_shapes=[pltpu.VMEM((SUBLANES, CONV_HALO + MERGE_TM, c), F32),
                        pltpu.VMEM((MERGE_TM, c), F32)],
        compiler_params=_cparams(("arbitrary", "arbitrary")),
        name="merge",
    )(x, mod_l, g.reshape(1, d), y_a, y_b, u, u, conv_w, conv_b.reshape(1, c),
      ln_g.reshape(1, c), ln_b.reshape(1, c), w_gate, b_gate.reshape(1, N_BRANCH * d),
      w_branch, w_out)


def _qk_gain_rows(qk_gain):
    ga_q = jnp.tile(qk_gain[0], HA)
    ga_k = jnp.tile(qk_gain[1], HA)
    gb_q = jnp.tile(jnp.concatenate([qk_gain[2], qk_gain[3]]), HB)
    gb_k = jnp.tile(jnp.concatenate([qk_gain[4], qk_gain[5]]), HB)
    return jnp.stack([ga_q, ga_k, gb_q, gb_k])


def kernel(x, c, rel_bias, w_ada, b_ada, norm_g, w_ffn_in, w_ffn_out, w_in, qk_gain, lambda_vec,
           subln_g, conv_w, conv_b, conv_ln_g, conv_ln_b, w_branch, w_gate, b_gate, w_out):
    b, s, d = x.shape
    mod = _ada_mod(c, w_ada, b_ada)
    bias_a, bias_b = _bias_tiles(rel_bias, s)
    w_ffn_in, w_ffn_out, w_in = w_ffn_in.astype(BF16), w_ffn_out.astype(BF16), w_in.astype(BF16)
    w_gate, w_branch, w_out = w_gate.astype(BF16), w_branch.astype(BF16), w_out.astype(BF16)
    for l in range(DEPTH):
        mod_l = mod[l]
        x = _ffn(x, mod_l, norm_g[l, 0], w_ffn_in, w_ffn_out, (l, 0), 0)
        qa, ka, va, qb, kb, vbt, u = _proj(x, mod_l, norm_g[l, 1], w_in, l,
                                          _qk_gain_rows(qk_gain[l]))
        y_a = _dilated_mixer(qa, ka, va, bias_a)
        y_b = _diff_attention(qb, kb, vbt, bias_b, lambda_vec[l], subln_g[l], l)
        x = _merge(x, mod_l, norm_g[l, 1], y_a, y_b, u, conv_w[l], conv_b[l], conv_ln_g[l],
                   conv_ln_b[l], l, w_gate, b_gate[l], w_branch, w_out)
        x = _ffn(x, mod_l, norm_g[l, 2], w_ffn_in, w_ffn_out, (l, 1), 6)
    return x
```

```python
import functools
import math

import numpy as np
import jax
import jax.numpy as jnp
from jax import lax
from jax.experimental import pallas as pl
from jax.experimental.pallas import tpu as pltpu

D_MODEL = 1024
DEPTH = 2
HEAD_DIM = 64
HA = 8
DIL_PATTERNS = ((128, 1), (512, 4), (2048, 16))
HB = 4
C_CONV = 512
CONV_K = 31
D_FF = 2816
N_BUCKETS = 32
REL_MAX_DIST = 2048
BLK = 128
MIX_W = 512
N_BRANCH = 3
EPS = 1e-6
NEG = -1e30
LOG2E = math.log2(math.e)

LANES = 128
SUBLANES = 8
VMEM_LIMIT = 56 * 1024 * 1024

F32 = jnp.float32
BF16 = jnp.bfloat16


def _cparams(sem):
    return pltpu.CompilerParams(dimension_semantics=sem, vmem_limit_bytes=VMEM_LIMIT)


def _sigmoid(x):
    return 1.0 / (1.0 + jnp.exp(-x))


def _resident(shape, lead=()):
    lead = tuple(lead)
    zeros = (0,) * len(shape)
    return pl.BlockSpec((None,) * len(lead) + tuple(shape), lambda *_: lead + zeros,
                        pipeline_mode=pl.Buffered(1))


def _norm_mod(x, g, shift, scale):
    ms = jnp.mean(x * x, axis=-1, keepdims=True)
    y = x * lax.rsqrt(ms + EPS) * g
    return y * (1.0 + scale) + shift


ADA_TN = 2304


def _ada_kernel(c_ref, w_ref, b_ref, o_ref):
    c = c_ref[...]
    a = c * _sigmoid(c)
    o_ref[0] = jnp.dot(a, w_ref[0], preferred_element_type=F32,
                       precision=lax.Precision.HIGHEST) + b_ref[0]


def _ada_mod(c, w_ada, b_ada):
    b, d = c.shape
    rows = 8
    c_pad = jnp.pad(c, ((0, rows - b), (0, 0)))
    n = w_ada.shape[-1]
    out = pl.pallas_call(
        _ada_kernel,
        grid=(DEPTH, n // ADA_TN),
        in_specs=[
            pl.BlockSpec((rows, d), lambda l, j: (0, 0)),
            pl.BlockSpec((1, d, ADA_TN), lambda l, j: (l, 0, j)),
            pl.BlockSpec((1, 1, ADA_TN), lambda l, j: (l, 0, j)),
        ],
        out_specs=pl.BlockSpec((1, rows, ADA_TN), lambda l, j: (l, 0, j)),
        out_shape=jax.ShapeDtypeStruct((DEPTH, rows, n), F32),
        compiler_params=_cparams(("arbitrary", "arbitrary")),
        name="ada_mod",
    )(c_pad, w_ada, b_ada.reshape(DEPTH, 1, n))
    return out[:, :b].reshape(DEPTH, b, 9, d)


FFN_TM = 512
FFN_TF = 256


def _ffn_kernel(x_ref, mod_ref, g_ref, wup_ref, wdn_ref, o_ref, act_ref, *, k0):
    x = x_ref[0]
    h = _norm_mod(x, g_ref[...], mod_ref[0, k0:k0 + 1, :], mod_ref[0, k0 + 1:k0 + 2, :])
    hb = h.astype(BF16)
    for j in range(D_FF // FFN_TF):
        lo = j * FFN_TF
        gate = jnp.dot(hb, wup_ref[:, lo:lo + FFN_TF], preferred_element_type=F32)
        up = jnp.dot(hb, wup_ref[:, D_FF + lo:D_FF + lo + FFN_TF], preferred_element_type=F32)
        act_ref[:, lo:lo + FFN_TF] = (gate * _sigmoid(gate) * up).astype(BF16)
    y = jnp.dot(act_ref[...], wdn_ref[...], preferred_element_type=F32)
    o_ref[0] = x + (0.5 * mod_ref[0, k0 + 2:k0 + 3, :]) * y


def _ffn(x, mod_l, g, w_up, w_dn, lead, k0):
    b, s, d = x.shape
    tok = pl.BlockSpec((1, FFN_TM, d), lambda i, j: (i, j, 0))
    return pl.pallas_call(
        functools.partial(_ffn_kernel, k0=k0),
        grid=(b, s // FFN_TM),
        in_specs=[
            tok,
            pl.BlockSpec((1, 9, d), lambda i, j: (i, 0, 0)),
            _resident((1, d)),
            _resident((d, 2 * D_FF), lead),
            _resident((D_FF, d), lead),
        ],
        out_specs=tok,
        out_shape=jax.ShapeDtypeStruct(x.shape, F32),
        scratch_shapes=[pltpu.VMEM((FFN_TM, D_FF), BF16)],
        compiler_params=_cparams(("arbitrary", "arbitrary")),
        name="ffn",
    )(x, mod_l, g.reshape(1, d), w_up, w_dn)


PROJ_TM = 512


def _head_rmsnorm(acc, gain):
    rows = acc.shape[0]
    low = lax.broadcasted_iota(jnp.int32, (rows, LANES), 1) < HEAD_DIM
    outs = []
    for c in range(acc.shape[1] // LANES):
        xc = acc[:, c * LANES:(c + 1) * LANES]
        sq = xc * xc
        s_lo = jnp.sum(jnp.where(low, sq, 0.0), axis=-1, keepdims=True)
        s_hi = jnp.sum(jnp.where(low, 0.0, sq), axis=-1, keepdims=True)
        r_lo = lax.rsqrt(s_lo * (1.0 / HEAD_DIM) + EPS)
        r_hi = lax.rsqrt(s_hi * (1.0 / HEAD_DIM) + EPS)
        outs.append(xc * jnp.where(low, r_lo, r_hi) * gain[:, c * LANES:(c + 1) * LANES])
    return jnp.concatenate(outs, axis=-1)


def _proj_kernel(x_ref, mod_ref, g_ref, w_ref, gain_ref,
                 qa_ref, ka_ref, va_ref, qb_ref, kb_ref, vbt_ref, u_ref):
    x = x_ref[0]
    h = _norm_mod(x, g_ref[...], mod_ref[0, 3:4, :], mod_ref[0, 4:5, :])
    hb = h.astype(BF16)
    w = MIX_W

    def col(j):
        return jnp.dot(hb, w_ref[:, j * w:(j + 1) * w], preferred_element_type=F32)

    q_scale = 1.0 / math.sqrt(HEAD_DIM)
    qa_ref[0] = (_head_rmsnorm(col(0), gain_ref[0:1, :]) * (q_scale * LOG2E)).astype(BF16)
    ka_ref[0] = _head_rmsnorm(col(1), gain_ref[1:2, :]).astype(BF16)
    va_ref[0] = col(2).astype(BF16)
    qb_ref[0] = (_head_rmsnorm(col(3), gain_ref[2:3, :]) * (q_scale * LOG2E)).astype(BF16)
    kb_ref[0] = _head_rmsnorm(col(4), gain_ref[3:4, :]).astype(BF16)
    vbt_ref[0] = col(5).T.astype(BF16)
    u_ref[0] = col(6) * _sigmoid(col(7))


def _proj(x, mod_l, g, w_in, layer, gains):
    b, s, d = x.shape
    tok_in = pl.BlockSpec((1, PROJ_TM, d), lambda i, j: (i, j, 0))
    tok_out = pl.BlockSpec((1, PROJ_TM, MIX_W), lambda i, j: (i, j, 0))
    bf = jax.ShapeDtypeStruct((b, s, MIX_W), BF16)
    tr_out = pl.BlockSpec((1, MIX_W, PROJ_TM), lambda i, j: (i, 0, j))
    return pl.pallas_call(
        _proj_kernel,
        grid=(b, s // PROJ_TM),
        in_specs=[
            tok_in,
            pl.BlockSpec((1, 9, d), lambda i, j: (i, 0, 0)),
            _resident((1, d)),
            _resident(w_in.shape[1:], (layer,)),
            _resident(gains.shape),
        ],
        out_specs=[tok_out] * 5 + [tr_out, tok_out],
        out_shape=[bf] * 5 + [jax.ShapeDtypeStruct((b, MIX_W, s), BF16),
                              jax.ShapeDtypeStruct((b, s, MIX_W), F32)],
        compiler_params=_cparams(("arbitrary", "arbitrary")),
        name="proj",
    )(x, mod_l, g.reshape(1, d), w_in, gains)


def _bucket_thresholds():
    max_exact = N_BUCKETS // 2
    d = np.arange(0, 2 * REL_MAX_DIST + 2)
    df = np.maximum(d.astype(np.float32), np.float32(1.0))
    large = max_exact + (np.log(df / np.float32(max_exact))
                         / np.float32(math.log(REL_MAX_DIST / max_exact))
                         * np.float32(N_BUCKETS - max_exact)).astype(np.int32)
    bucket = np.where(d < max_exact, d, np.minimum(large, N_BUCKETS - 1))
    return [int(np.argmax(bucket >= b)) for b in range(1, N_BUCKETS)]


_THRESHOLDS = _bucket_thresholds()


def _bias_of_dist(dist, tab_ref, head):
    val = jnp.full(dist.shape, tab_ref[0, head], F32)
    for b in range(1, N_BUCKETS):
        val = jnp.where(dist >= _THRESHOLDS[b - 1], tab_ref[b, head], val)
    return val


DIFF_T = 256


def _diff_bias_tiles(seq):
    first_const = -(-(_THRESHOLDS[-1] + DIFF_T - 1) // DIFF_T)
    return min(seq // DIFF_T, first_const + 1)


def _bias_diff_kernel(tab_ref, o_ref):
    h = pl.program_id(0)
    delta = pl.program_id(1)
    j = lax.broadcasted_iota(jnp.int32, (DIFF_T, DIFF_T), 0)
    i = lax.broadcasted_iota(jnp.int32, (DIFF_T, DIFF_T), 1)
    dist = jnp.maximum(delta * DIFF_T + i - j, 0)
    o_ref[0, 0] = _bias_of_dist(dist, tab_ref, HA + h) * LOG2E


def _bias_dil_kernel(tab_ref, o_ref):
    p = pl.program_id(0)
    hp = pl.program_id(1)
    dil = jnp.where(p == 0, DIL_PATTERNS[0][1],
                    jnp.where(p == 1, DIL_PATTERNS[1][1], DIL_PATTERNS[2][1]))
    j = lax.broadcasted_iota(jnp.int32, (2 * BLK, BLK), 0)
    i = lax.broadcasted_iota(jnp.int32, (2 * BLK, BLK), 1)
    rel = i + BLK - j
    band = (rel >= 0) & (rel <= BLK)
    dist = jnp.maximum(rel, 0) * dil
    for sub in range(2):
        bias = _bias_of_dist(dist, tab_ref, 2 * hp + sub) * LOG2E
        lanes = slice(sub * BLK, (sub + 1) * BLK)
        o_ref[0, 0, 0, :, lanes] = jnp.where(band & (j >= BLK), bias, NEG)
        o_ref[0, 1, 0, :, lanes] = jnp.where(band, bias, NEG)


def _bias_tiles(rel_bias, seq):
    smem = pl.BlockSpec(memory_space=pltpu.SMEM)
    n_delta = _diff_bias_tiles(seq)
    bias_b = pl.pallas_call(
        _bias_diff_kernel,
        grid=(HB, n_delta),
        in_specs=[smem],
        out_specs=pl.BlockSpec((1, 1, DIFF_T, DIFF_T), lambda h, t: (h, t, 0, 0)),
        out_shape=jax.ShapeDtypeStruct((HB, n_delta, DIFF_T, DIFF_T), F32),
        compiler_params=_cparams(("arbitrary", "arbitrary")),
        name="bias_diff",
    )(rel_bias)
    n_pat = len(DIL_PATTERNS)
    bias_a = pl.pallas_call(
        _bias_dil_kernel,
        grid=(n_pat, HA // 2),
        in_specs=[smem],
        out_specs=pl.BlockSpec((1, 2, 1, 2 * BLK, 2 * BLK), lambda p, h: (p, 0, h, 0, 0)),
        out_shape=jax.ShapeDtypeStruct((n_pat, 2, HA // 2, 2 * BLK, 2 * BLK), F32),
        compiler_params=_cparams(("arbitrary", "arbitrary")),
        name="bias_dil",
    )(rel_bias)
    return bias_a, bias_b


DIL_CHAINS = 4
DIL_ONES = 16
DIL_COMBINE_ROWS = 512


def _dil_kernel(q_ref, k_ref, v_ref, bias_ref, y_ref,
                q32_ref, k32_ref, v32_ref, qd_ref, kd_ref, vtd_ref, ot_ref, lt_ref, on_ref, ln_ref,
                sa_ref, sb_ref):
    seq = q_ref.shape[1]
    pad = BLK
    q32_ref[...] = q_ref[0].astype(F32)
    k32_ref[...] = k_ref[0].astype(F32)
    v32_ref[...] = v_ref[0].astype(F32)
    kd_ref[0:pad, :] = jnp.zeros((pad, LANES), BF16)
    vtd_ref[:, 0:pad] = jnp.zeros((LANES, pad), BF16)
    low = lax.broadcasted_iota(jnp.int32, (BLK, LANES), 1) < HEAD_DIM
    top = lax.broadcasted_iota(jnp.int32, (LANES, BLK), 0) < HEAD_DIM
    ones = jnp.ones((DIL_ONES, 2 * BLK), BF16)

    for p, (_, dil) in enumerate(DIL_PATTERNS):
        sub_len = seq // dil
        nb = sub_len // BLK
        for r in range(dil):
            rows = pl.ds(r, sub_len, stride=dil)
            qd_ref[r * sub_len:(r + 1) * sub_len, :] = q32_ref[rows, :].astype(BF16)
            kd_ref[pad + r * sub_len:pad + (r + 1) * sub_len, :] = k32_ref[rows, :].astype(BF16)
            vtd_ref[:, pad + r * sub_len:pad + (r + 1) * sub_len] = v32_ref[rows, :].T.astype(BF16)

        def produce(j, s_ref, p=p, nb=nb):
            for c in range(DIL_CHAINS):
                chain = j * DIL_CHAINS + c
                base = pl.multiple_of(chain * BLK, BLK)
                var = jnp.where((chain & (nb - 1)) == 0, 0, 1)
                q = qd_ref[pl.ds(base, BLK), :]
                zero = jnp.zeros_like(q)
                qs = jnp.concatenate([jnp.where(low, q, zero), jnp.where(low, zero, q)], axis=0)
                kc = kd_ref[pl.ds(base, 2 * BLK), :]
                sc = lax.dot_general(kc, qs, (((1,), (1,)), ((), ())), preferred_element_type=F32)
                s_ref[c] = sc + bias_ref[p, var]

        def consume(j, s_ref):
            for c in range(DIL_CHAINS):
                base = pl.multiple_of((j * DIL_CHAINS + c) * BLK, BLK)
                sc = s_ref[c]
                m = jnp.max(sc, axis=0, keepdims=True)
                pexp = jnp.exp2(sc - m).astype(BF16)
                vt = jnp.concatenate([vtd_ref[:, pl.ds(base, 2 * BLK)], ones], axis=0)
                pv = jnp.dot(vt, pexp, preferred_element_type=F32)
                den = pv[LANES:LANES + 1, :]
                o = pv[0:LANES, :] / den
                lse = m + jnp.log2(den)
                chain = j * DIL_CHAINS + c
                ot_ref[chain] = jnp.where(top, o[:, :BLK], o[:, BLK:])
                lt_ref[chain] = jnp.where(top, lse[:, :BLK], lse[:, BLK:])

        n_trips = seq // BLK // DIL_CHAINS
        produce(0, sa_ref)

        def two_trips(i, carry, produce=produce, consume=consume):
            j = 2 * i
            produce(j + 1, sb_ref)
            consume(j, sa_ref)
            produce(j + 2, sa_ref)
            consume(j + 1, sb_ref)
            return carry

        lax.fori_loop(0, n_trips // 2 - 1, two_trips, 0)
        produce(n_trips - 1, sb_ref)
        consume(n_trips - 2, sa_ref)
        consume(n_trips - 1, sb_ref)
        for chain in range(seq // BLK):
            r, n = divmod(chain, nb)
            rows = pl.ds(n * BLK * dil + r, BLK, stride=dil)
            on_ref[p, rows, :] = ot_ref[chain].T
            ln_ref[p, rows, :] = lt_ref[chain].T

    def combine(i, carry):
        rows = pl.ds(pl.multiple_of(i * DIL_COMBINE_ROWS, DIL_COMBINE_ROWS), DIL_COMBINE_ROWS)
        l1, l2, l3 = ln_ref[0, rows, :], ln_ref[1, rows, :], ln_ref[2, rows, :]
        mx = jnp.maximum(jnp.maximum(l1, l2), l3)
        e1 = jnp.exp2(l1 - mx)
        e2 = jnp.exp2(l2 - mx)
        e3 = jnp.exp2(l3 - mx)
        num = e1 * on_ref[0, rows, :] + e2 * on_ref[1, rows, :] + e3 * on_ref[2, rows, :]
        y_ref[0, rows, :] = (num / (e1 + e2 + e3)).astype(y_ref.dtype)
        return carry

    lax.fori_loop(0, seq // DIL_COMBINE_ROWS, combine, 0)


def _dilated_mixer(qa, ka, va, bias_a):
    b, s, w = qa.shape
    n_pat = len(DIL_PATTERNS)
    pair = pl.BlockSpec((1, s, LANES), lambda i, h: (i, 0, h))
    return pl.pallas_call(
        _dil_kernel,
        grid=(b, HA // 2),
        in_specs=[pair, pair, pair,
                  pl.BlockSpec((n_pat, 2, None, 2 * BLK, 2 * BLK), lambda i, h: (0, 0, h, 0, 0))],
        out_specs=pair,
        out_shape=jax.ShapeDtypeStruct((b, s, w), BF16),
        scratch_shapes=[
            pltpu.VMEM((s, LANES), F32), pltpu.VMEM((s, LANES), F32), pltpu.VMEM((s, LANES), F32),
            pltpu.VMEM((s, LANES), BF16), pltpu.VMEM((BLK + s, LANES), BF16),
            pltpu.VMEM((LANES, BLK + s), BF16),
            pltpu.VMEM((s // BLK, LANES, BLK), F32), pltpu.VMEM((s // BLK, LANES, BLK), F32),
            pltpu.VMEM((n_pat, s, LANES), F32), pltpu.VMEM((n_pat, s, LANES), F32),
            pltpu.VMEM((DIL_CHAINS, 2 * BLK, 2 * BLK), F32),
            pltpu.VMEM((DIL_CHAINS, 2 * BLK, 2 * BLK), F32),
        ],
        compiler_params=_cparams(("arbitrary", "arbitrary")),
        name="dilated",
    )(qa, ka, va, bias_a)


DIFF_TQ = 2 * DIFF_T
DIFF_HEADS = 2
DIFF_ONES = 16


def _diff_kernel(q_ref, k_ref, vt_ref, bias_ref, lam_ref, sg_ref, o_ref,
                 qs_ref, m_ref, acc_ref, s0_ref, s1_ref, *, lam_init, n_bias):
    qi = pl.program_id(2)
    t = DIFF_T
    dv = 2 * HEAD_DIM
    n_blk = 2 * DIFF_TQ // t
    width = 2 * DIFF_TQ
    low = lax.broadcasted_iota(jnp.int32, (DIFF_TQ, LANES), 1) < HEAD_DIM
    for hd in range(DIFF_HEADS):
        q = q_ref[0, :, hd * LANES:(hd + 1) * LANES]
        zero = jnp.zeros_like(q)
        qs_ref[hd * width:hd * width + DIFF_TQ, :] = jnp.where(low, q, zero)
        qs_ref[hd * width + DIFF_TQ:(hd + 1) * width, :] = jnp.where(low, zero, q)
    m_ref[...] = jnp.full(m_ref.shape, NEG, F32)
    acc_ref[...] = jnp.zeros(acc_ref.shape, F32)
    ones = jnp.ones((DIFF_ONES, t), BF16)

    def produce(kt, s_ref, blocks):
        off = pl.multiple_of(kt * t, t)
        for hd in range(DIFF_HEADS):
            k = k_ref[0, pl.ds(off, t), hd * LANES:(hd + 1) * LANES]
            for blk in blocks:
                lanes = slice(hd * width + blk * t, hd * width + (blk + 1) * t)
                s = lax.dot_general(k, qs_ref[lanes, :], (((1,), (1,)), ((), ())),
                                    preferred_element_type=F32)
                delta = 2 * qi + blk % 2 - kt
                s_ref[hd * n_blk + blk] = s + bias_ref[hd, jnp.minimum(delta, n_bias - 1)]

    def consume(kt, s_ref, modes):
        off = pl.multiple_of(kt * t, t)
        for hd in range(DIFF_HEADS):
            vt = jnp.concatenate([vt_ref[0, hd * dv:(hd + 1) * dv, pl.ds(off, t)], ones], axis=0)
            for blk, mode in enumerate(modes):
                if mode == "skip":
                    continue
                g = hd * n_blk + blk
                lanes = slice(g * t, (g + 1) * t)
                s = s_ref[g]
                if mode == "diag":
                    key = lax.broadcasted_iota(jnp.int32, (t, t), 0)
                    qry = lax.broadcasted_iota(jnp.int32, (t, t), 1)
                    s = jnp.where(qry >= key, s, NEG)
                m_old = m_ref[:, lanes]
                m_new = jnp.maximum(m_old, jnp.max(s, axis=0, keepdims=True))
                alpha = jnp.exp2(m_old - m_new)
                p = jnp.exp2(s - m_new).astype(BF16)
                pv = jnp.dot(vt, p, preferred_element_type=F32)
                acc_ref[g] = alpha * acc_ref[g] + pv
                m_ref[:, lanes] = m_new

    every = tuple(range(n_blk))
    full = ("full",) * n_blk
    produce(0, s0_ref, every)

    def body(j, carry):
        kt = 2 * j
        produce(kt + 1, s1_ref, every)
        consume(kt, s0_ref, full)
        produce(kt + 2, s0_ref, every)
        consume(kt + 1, s1_ref, full)
        return carry

    lax.fori_loop(0, qi, body, 0)
    produce(2 * qi + 1, s1_ref, (1, 3))
    consume(2 * qi, s0_ref, ("diag", "full", "diag", "full"))
    consume(2 * qi + 1, s1_ref, ("skip", "diag", "skip", "diag"))

    lv = lam_ref[...]
    s01 = jnp.sum(lv[0:1] * lv[1:2], axis=-1, keepdims=True)
    s23 = jnp.sum(lv[2:3] * lv[3:4], axis=-1, keepdims=True)
    lam = jnp.exp(s01) - jnp.exp(s23) + lam_init
    for hd in range(DIFF_HEADS):
        o_t = [acc_ref[hd * n_blk + blk, 0:dv, :] / acc_ref[hd * n_blk + blk, dv:dv + 1, :]
               for blk in range(n_blk)]
        half = n_blk // 2
        ob = jnp.concatenate([o_t[blk] - lam * o_t[half + blk] for blk in range(half)],
                             axis=1)
        ms = jnp.mean(ob * ob, axis=0, keepdims=True)
        y = ob * lax.rsqrt(ms + EPS) * sg_ref[...]
        o_ref[0, :, hd * LANES:(hd + 1) * LANES] = (y * (1.0 - lam_init)).T.astype(o_ref.dtype)


def _diff_attention(qb, kb, vbt, bias_b, lambda_vec, subln_g, layer):
    b, s, w = qb.shape
    t = DIFF_T
    dv = 2 * HEAD_DIM
    n_bias = bias_b.shape[1]
    lam_init = 0.8 - 0.6 * math.exp(-0.3 * layer)
    heads = DIFF_HEADS
    n_lane_blocks = heads * 2 * DIFF_TQ // t
    tile = pl.BlockSpec((1, DIFF_TQ, heads * LANES), lambda i, h, n: (i, n, h))
    return pl.pallas_call(
        functools.partial(_diff_kernel, lam_init=lam_init, n_bias=n_bias),
        grid=(b, HB // heads, s // DIFF_TQ),
        in_specs=[
            tile,
            pl.BlockSpec((1, s, heads * LANES), lambda i, h, n: (i, 0, h)),
            pl.BlockSpec((1, heads * dv, s), lambda i, h, n: (i, h, 0)),
            pl.BlockSpec((heads, n_bias, t, t), lambda i, h, n: (h, 0, 0, 0)),
            pl.BlockSpec(lambda_vec.shape, lambda i, h, n: (0, 0)),
            pl.BlockSpec((dv, 1), lambda i, h, n: (0, 0)),
        ],
        out_specs=tile,
        out_shape=jax.ShapeDtypeStruct((b, s, w), BF16),
        scratch_shapes=[pltpu.VMEM((heads * 2 * DIFF_TQ, LANES), BF16),
                        pltpu.VMEM((1, heads * 2 * DIFF_TQ), F32),
                        pltpu.VMEM((n_lane_blocks, dv + DIFF_ONES, t), F32),
                        pltpu.VMEM((n_lane_blocks, t, t), F32),
                        pltpu.VMEM((n_lane_blocks, t, t), F32)],
        compiler_params=_cparams(("arbitrary", "arbitrary", "arbitrary")),
        name="diff_attn",
    )(qb, kb, vbt, bias_b, lambda_vec, subln_g.reshape(dv, 1))


MERGE_TM = 512
CONV_HALO = 32
CONV_ROWS = 128


def _conv_tile(prev_ref, cur_ref, w_ref, cb_ref, lg_ref, lb_ref, sh_ref, y_ref, first_tile):
    halo = prev_ref[0]
    rows_in = cur_ref.shape[1]
    sh_ref[0, 0:CONV_HALO, :] = jnp.where(first_tile, jnp.zeros_like(halo), halo)
    sh_ref[0, CONV_HALO:, :] = cur_ref[0]
    for b in range(1, SUBLANES):
        sh_ref[b] = pltpu.roll(sh_ref[0], b, axis=0)
    for r0 in range(0, rows_in, CONV_ROWS):
        acc = jnp.zeros((CONV_ROWS, C_CONV), F32)
        for tap in range(CONV_K):
            back = CONV_K - 1 - tap
            a, b = divmod(back, SUBLANES)
            lo = r0 + CONV_HALO - a * SUBLANES
            acc = acc + sh_ref[b, lo:lo + CONV_ROWS, :] * w_ref[tap:tap + 1, :]
        y_ref[r0:r0 + CONV_ROWS, :] = acc
    y = y_ref[...] + cb_ref[...]
    mu = jnp.mean(y, axis=-1, keepdims=True)
    var = jnp.mean(jnp.square(y - mu), axis=-1, keepdims=True)
    z = (y - mu) * lax.rsqrt(var + EPS) * lg_ref[...] + lb_ref[...]
    return z * _sigmoid(z)


def _merge_kernel(x_ref, mod_ref, g_ref, ya_ref, yb_ref, uprev_ref, u_ref,
                  cw_ref, cb_ref, lg_ref, lb_ref, wg_ref, bg_ref, wb_ref, wo_ref, o_ref,
                  sh_ref, yc_ref):
    x = x_ref[0]
    d = x.shape[-1]
    h = _norm_mod(x, g_ref[...], mod_ref[0, 3:4, :], mod_ref[0, 4:5, :])
    hb = h.astype(BF16)
    y_c = _conv_tile(uprev_ref, u_ref, cw_ref, cb_ref, lg_ref, lb_ref, sh_ref, yc_ref,
                     pl.program_id(1) == 0).astype(BF16)
    mixed = jnp.zeros(x.shape, F32)
    for i, y in enumerate((ya_ref[0], yb_ref[0], y_c)):
        z = jnp.dot(hb, wg_ref[:, i * d:(i + 1) * d], preferred_element_type=F32)
        gate = _sigmoid(z + bg_ref[:, i * d:(i + 1) * d])
        mixed = mixed + gate * jnp.dot(y, wb_ref[i], preferred_element_type=F32)
    y = jnp.dot(mixed.astype(BF16), wo_ref[...], preferred_element_type=F32)
    o_ref[0] = x + mod_ref[0, 5:6, :] * y


def _merge(x, mod_l, g, y_a, y_b, u, conv_w, conv_b, ln_g, ln_b, layer,
           w_gate, b_gate, w_branch, w_out):
    b, s, d = x.shape
    c = u.shape[-1]
    per_tile = MERGE_TM // CONV_HALO
    tok = pl.BlockSpec((1, MERGE_TM, d), lambda i, j: (i, j, 0))
    br = pl.BlockSpec((1, MERGE_TM, MIX_W), lambda i, j: (i, j, 0))
    return pl.pallas_call(
        _merge_kernel,
        grid=(b, s // MERGE_TM),
        in_specs=[
            tok,
            pl.BlockSpec((1, 9, d), lambda i, j: (i, 0, 0)),
            _resident((1, d)),
            br, br,
            pl.BlockSpec((1, CONV_HALO, c), lambda i, j: (i, jnp.maximum(j * per_tile - 1, 0), 0)),
            br,
            _resident((CONV_K, c)), _resident((1, c)), _resident((1, c)), _resident((1, c)),
            _resident(w_gate.shape[1:], (layer,)),
            _resident((1, N_BRANCH * d)),
            _resident(w_branch.shape[1:], (layer,)),
            _resident(w_out.shape[1:], (layer,)),
        ],
        out_specs=tok,
        out_shape=jax.ShapeDtypeStruct(x.shape, F32),
        scratch_shapes=[pltpu.VMEM((SUBLANES, CONV_HALO + MERGE_TM, c), F32),
                        pltpu.VMEM((MERGE_TM, c), F32)],
        compiler_params=_cparams(("arbitrary", "arbitrary")),
        name="merge",
    )(x, mod_l, g.reshape(1, d), y_a, y_b, u, u, conv_w, conv_b.reshape(1, c),
      ln_g.reshape(1, c), ln_b.reshape(1, c), w_gate, b_gate.reshape(1, N_BRANCH * d),
      w_branch, w_out)


def _qk_gain_rows(qk_gain):
    ga_q = jnp.tile(qk_gain[0], HA)
    ga_k = jnp.tile(qk_gain[1], HA)
    gb_q = jnp.tile(jnp.concatenate([qk_gain[2], qk_gain[3]]), HB)
    gb_k = jnp.tile(jnp.concatenate([qk_gain[4], qk_gain[5]]), HB)
    return jnp.stack([ga_q, ga_k, gb_q, gb_k])


def kernel(x, c, rel_bias, w_ada, b_ada, norm_g, w_ffn_in, w_ffn_out, w_in, qk_gain, lambda_vec,
           subln_g, conv_w, conv_b, conv_ln_g, conv_ln_b, w_branch, w_gate, b_gate, w_out):
    b, s, d = x.shape
    mod = _ada_mod(c, w_ada, b_ada)
    bias_a, bias_b = _bias_tiles(rel_bias, s)
    w_ffn_in, w_ffn_out, w_in = w_ffn_in.astype(BF16), w_ffn_out.astype(BF16), w_in.astype(BF16)
    w_gate, w_branch, w_out = w_gate.astype(BF16), w_branch.astype(BF16), w_out.astype(BF16)
    for l in range(DEPTH):
        mod_l = mod[l]
        x = _ffn(x, mod_l, norm_g[l, 0], w_ffn_in, w_ffn_out, (l, 0), 0)
        qa, ka, va, qb, kb, vbt, u = _proj(x, mod_l, norm_g[l, 1], w_in, l,
                                          _qk_gain_rows(qk_gain[l]))
        y_a = _dilated_mixer(qa, ka, va, bias_a)
        y_b = _diff_attention(qb, kb, vbt, bias_b, lambda_vec[l], subln_g[l], l)
        x = _merge(x, mod_l, norm_g[l, 1], y_a, y_b, u, conv_w[l], conv_b[l], conv_ln_g[l],
                   conv_ln_b[l], l, w_gate, b_gate[l], w_branch, w_out)
        x = _ffn(x, mod_l, norm_g[l, 2], w_ffn_in, w_ffn_out, (l, 1), 6)
    return x
```

```python
import functools
import math

import numpy as np
import jax
import jax.numpy as jnp
from jax import lax
from jax.experimental import pallas as pl
from jax.experimental.pallas import tpu as pltpu

D_MODEL = 1024
DEPTH = 2
HEAD_DIM = 64
HA = 8
DIL_PATTERNS = ((128, 1), (512, 4), (2048, 16))
HB = 4
C_CONV = 512
CONV_K = 31
D_FF = 2816
N_BUCKETS = 32
REL_MAX_DIST = 2048
BLK = 128
MIX_W = 512
N_BRANCH = 3
EPS = 1e-6
NEG = -1e30
LOG2E = math.log2(math.e)

LANES = 128
SUBLANES = 8
VMEM_LIMIT = 56 * 1024 * 1024

F32 = jnp.float32
BF16 = jnp.bfloat16


def _cparams(sem):
    return pltpu.CompilerParams(dimension_semantics=sem, vmem_limit_bytes=VMEM_LIMIT)


def _sigmoid(x):
    return 1.0 / (1.0 + jnp.exp(-x))


def _resident(shape, lead=()):
    lead = tuple(lead)
    zeros = (0,) * len(shape)
    return pl.BlockSpec((None,) * len(lead) + tuple(shape), lambda *_: lead + zeros,
                        pipeline_mode=pl.Buffered(1))


def _norm_mod(x, g, shift, scale):
    ms = jnp.mean(x * x, axis=-1, keepdims=True)
    y = x * lax.rsqrt(ms + EPS) * g
    return y * (1.0 + scale) + shift


ADA_TN = 2304


def _ada_kernel(c_ref, w_ref, b_ref, o_ref):
    c = c_ref[...]
    a = c * _sigmoid(c)
    o_ref[0] = jnp.dot(a, w_ref[0], preferred_element_type=F32,
                       precision=lax.Precision.HIGHEST) + b_ref[0]


def _ada_mod(c, w_ada, b_ada):
    b, d = c.shape
    rows = 8
    c_pad = jnp.pad(c, ((0, rows - b), (0, 0)))
    n = w_ada.shape[-1]
    out = pl.pallas_call(
        _ada_kernel,
        grid=(DEPTH, n // ADA_TN),
        in_specs=[
            pl.BlockSpec((rows, d), lambda l, j: (0, 0)),
            pl.BlockSpec((1, d, ADA_TN), lambda l, j: (l, 0, j)),
            pl.BlockSpec((1, 1, ADA_TN), lambda l, j: (l, 0, j)),
        ],
        out_specs=pl.BlockSpec((1, rows, ADA_TN), lambda l, j: (l, 0, j)),
        out_shape=jax.ShapeDtypeStruct((DEPTH, rows, n), F32),
        compiler_params=_cparams(("arbitrary", "arbitrary")),
        name="ada_mod",
    )(c_pad, w_ada, b_ada.reshape(DEPTH, 1, n))
    return out[:, :b].reshape(DEPTH, b, 9, d)


FFN_TM = 512
FFN_TF = 256


def _ffn_kernel(x_ref, mod_ref, g_ref, wup_ref, wdn_ref, o_ref, act_ref, *, k0):
    x = x_ref[0]
    h = _norm_mod(x, g_ref[...], mod_ref[0, k0:k0 + 1, :], mod_ref[0, k0 + 1:k0 + 2, :])
    hb = h.astype(BF16)
    for j in range(D_FF // FFN_TF):
        lo = j * FFN_TF
        gate = jnp.dot(hb, wup_ref[:, lo:lo + FFN_TF], preferred_element_type=F32)
        up = jnp.dot(hb, wup_ref[:, D_FF + lo:D_FF + lo + FFN_TF], preferred_element_type=F32)
        act_ref[:, lo:lo + FFN_TF] = (gate * _sigmoid(gate) * up).astype(BF16)
    y = jnp.dot(act_ref[...], wdn_ref[...], preferred_element_type=F32)
    o_ref[0] = x + (0.5 * mod_ref[0, k0 + 2:k0 + 3, :]) * y


def _ffn(x, mod_l, g, w_up, w_dn, lead, k0):
    b, s, d = x.shape
    tok = pl.BlockSpec((1, FFN_TM, d), lambda i, j: (i, j, 0))
    return pl.pallas_call(
        functools.partial(_ffn_kernel, k0=k0),
        grid=(b, s // FFN_TM),
        in_specs=[
            tok,
            pl.BlockSpec((1, 9, d), lambda i, j: (i, 0, 0)),
            _resident((1, d)),
            _resident((d, 2 * D_FF), lead),
            _resident((D_FF, d), lead),
        ],
        out_specs=tok,
        out_shape=jax.ShapeDtypeStruct(x.shape, F32),
        scratch_shapes=[pltpu.VMEM((FFN_TM, D_FF), BF16)],
        compiler_params=_cparams(("arbitrary", "arbitrary")),
        name="ffn",
    )(x, mod_l, g.reshape(1, d), w_up, w_dn)


PROJ_TM = 512


def _head_rmsnorm(acc, gain):
    rows = acc.shape[0]
    low = lax.broadcasted_iota(jnp.int32, (rows, LANES), 1) < HEAD_DIM
    outs = []
    for c in range(acc.shape[1] // LANES):
        xc = acc[:, c * LANES:(c + 1) * LANES]
        sq = xc * xc
        s_lo = jnp.sum(jnp.where(low, sq, 0.0), axis=-1, keepdims=True)
        s_hi = jnp.sum(jnp.where(low, 0.0, sq), axis=-1, keepdims=True)
        r_lo = lax.rsqrt(s_lo * (1.0 / HEAD_DIM) + EPS)
        r_hi = lax.rsqrt(s_hi * (1.0 / HEAD_DIM) + EPS)
        outs.append(xc * jnp.where(low, r_lo, r_hi) * gain[:, c * LANES:(c + 1) * LANES])
    return jnp.concatenate(outs, axis=-1)


def _proj_kernel(x_ref, mod_ref, g_ref, w_ref, gain_ref,
                 qa_ref, ka_ref, va_ref, qb_ref, kb_ref, vbt_ref, u_ref):
    x = x_ref[0]
    h = _norm_mod(x, g_ref[...], mod_ref[0, 3:4, :], mod_ref[0, 4:5, :])
    hb = h.astype(BF16)
    w = MIX_W

    def col(j):
        return jnp.dot(hb, w_ref[:, j * w:(j + 1) * w], preferred_element_type=F32)

    q_scale = 1.0 / math.sqrt(HEAD_DIM)
    qa_ref[0] = (_head_rmsnorm(col(0), gain_ref[0:1, :]) * (q_scale * LOG2E)).astype(BF16)
    ka_ref[0] = _head_rmsnorm(col(1), gain_ref[1:2, :]).astype(BF16)
    va_ref[0] = col(2).astype(BF16)
    qb_ref[0] = (_head_rmsnorm(col(3), gain_ref[2:3, :]) * (q_scale * LOG2E)).astype(BF16)
    kb_ref[0] = _head_rmsnorm(col(4), gain_ref[3:4, :]).astype(BF16)
    vbt_ref[0] = col(5).T.astype(BF16)
    u_ref[0] = col(6) * _sigmoid(col(7))


def _proj(x, mod_l, g, w_in, layer, gains):
    b, s, d = x.shape
    tok_in = pl.BlockSpec((1, PROJ_TM, d), lambda i, j: (i, j, 0))
    tok_out = pl.BlockSpec((1, PROJ_TM, MIX_W), lambda i, j: (i, j, 0))
    bf = jax.ShapeDtypeStruct((b, s, MIX_W), BF16)
    tr_out = pl.BlockSpec((1, MIX_W, PROJ_TM), lambda i, j: (i, 0, j))
    return pl.pallas_call(
        _proj_kernel,
        grid=(b, s // PROJ_TM),
        in_specs=[
            tok_in,
            pl.BlockSpec((1, 9, d), lambda i, j: (i, 0, 0)),
            _resident((1, d)),
            _resident(w_in.shape[1:], (layer,)),
            _resident(gains.shape),
        ],
        out_specs=[tok_out] * 5 + [tr_out, tok_out],
        out_shape=[bf] * 5 + [jax.ShapeDtypeStruct((b, MIX_W, s), BF16),
                              jax.ShapeDtypeStruct((b, s, MIX_W), F32)],
        compiler_params=_cparams(("arbitrary", "arbitrary")),
        name="proj",
    )(x, mod_l, g.reshape(1, d), w_in, gains)


def _bucket_thresholds():
    max_exact = N_BUCKETS // 2
    d = np.arange(0, 2 * REL_MAX_DIST + 2)
    df = np.maximum(d.astype(np.float32), np.float32(1.0))
    large = max_exact + (np.log(df / np.float32(max_exact))
                         / np.float32(math.log(REL_MAX_DIST / max_exact))
                         * np.float32(N_BUCKETS - max_exact)).astype(np.int32)
    bucket = np.where(d < max_exact, d, np.minimum(large, N_BUCKETS - 1))
    return [int(np.argmax(bucket >= b)) for b in range(1, N_BUCKETS)]


_THRESHOLDS = _bucket_thresholds()


def _bias_of_dist(dist, tab_ref, head):
    val = jnp.full(dist.shape, tab_ref[0, head], F32)
    for b in range(1, N_BUCKETS):
        val = jnp.where(dist >= _THRESHOLDS[b - 1], tab_ref[b, head], val)
    return val


DIFF_T = 256


def _diff_bias_tiles(seq):
    first_const = -(-(_THRESHOLDS[-1] + DIFF_T - 1) // DIFF_T)
    return min(seq // DIFF_T, first_const + 1)


def _bias_diff_kernel(tab_ref, o_ref):
    h = pl.program_id(0)
    delta = pl.program_id(1)
    j = lax.broadcasted_iota(jnp.int32, (DIFF_T, DIFF_T), 0)
    i = lax.broadcasted_iota(jnp.int32, (DIFF_T, DIFF_T), 1)
    dist = jnp.maximum(delta * DIFF_T + i - j, 0)
    o_ref[0, 0] = _bias_of_dist(dist, tab_ref, HA + h) * LOG2E


def _bias_dil_kernel(tab_ref, o_ref):
    p = pl.program_id(0)
    hp = pl.program_id(1)
    dil = jnp.where(p == 0, DIL_PATTERNS[0][1],
                    jnp.where(p == 1, DIL_PATTERNS[1][1], DIL_PATTERNS[2][1]))
    j = lax.broadcasted_iota(jnp.int32, (2 * BLK, BLK), 0)
    i = lax.broadcasted_iota(jnp.int32, (2 * BLK, BLK), 1)
    rel = i + BLK - j
    band = (rel >= 0) & (rel <= BLK)
    dist = jnp.maximum(rel, 0) * dil
    for sub in range(2):
        bias = _bias_of_dist(dist, tab_ref, 2 * hp + sub) * LOG2E
        lanes = slice(sub * BLK, (sub + 1) * BLK)
        o_ref[0, 0, 0, :, lanes] = jnp.where(band & (j >= BLK), bias, NEG)
        o_ref[0, 1, 0, :, lanes] = jnp.where(band, bias, NEG)


def _bias_tiles(rel_bias, seq):
    smem = pl.BlockSpec(memory_space=pltpu.SMEM)
    n_delta = _diff_bias_tiles(seq)
    bias_b = pl.pallas_call(
        _bias_diff_kernel,
        grid=(HB, n_delta),
        in_specs=[smem],
        out_specs=pl.BlockSpec((1, 1, DIFF_T, DIFF_T), lambda h, t: (h, t, 0, 0)),
        out_shape=jax.ShapeDtypeStruct((HB, n_delta, DIFF_T, DIFF_T), F32),
        compiler_params=_cparams(("arbitrary", "arbitrary")),
        name="bias_diff",
    )(rel_bias)
    n_pat = len(DIL_PATTERNS)
    bias_a = pl.pallas_call(
        _bias_dil_kernel,
        grid=(n_pat, HA // 2),
        in_specs=[smem],
        out_specs=pl.BlockSpec((1, 2, 1, 2 * BLK, 2 * BLK), lambda p, h: (p, 0, h, 0, 0)),
        out_shape=jax.ShapeDtypeStruct((n_pat, 2, HA // 2, 2 * BLK, 2 * BLK), F32),
        compiler_params=_cparams(("arbitrary", "arbitrary")),
        name="bias_dil",
    )(rel_bias)
    return bias_a, bias_b


DIL_CHAINS = 4
DIL_STRIDE = 4
DIL_ONES = 16
DIL_COMBINE_ROWS = 512


def _dil_kernel(q_ref, k_ref, v_ref, bias_ref, y_ref,
                q32_ref, k32_ref, v32_ref, q4_ref, k4_ref, v4_ref, qd_ref, kd_ref, vtd_ref,
                ot_ref, lt_ref, on_ref, ln_ref, sa_ref, sb_ref):
    seq = q_ref.shape[1]
    pad = BLK
    quarter = seq // DIL_STRIDE
    nat = (q32_ref, k32_ref, v32_ref)
    by4 = (q4_ref, k4_ref, v4_ref)
    for src_ref, nat_ref, by4_ref in zip((q_ref, k_ref, v_ref), nat, by4):
        nat_ref[...] = src_ref[0].astype(F32)
        for rho in range(DIL_STRIDE):
            by4_ref[rho * quarter:(rho + 1) * quarter, :] = nat_ref[pl.ds(rho, quarter, stride=DIL_STRIDE), :]

    def residue(which, dil, r):
        if dil == 1:
            return nat[which][...]
        if dil == DIL_STRIDE:
            return by4[which][r * quarter:(r + 1) * quarter, :]
        a, rho = divmod(r, DIL_STRIDE)
        return by4[which][pl.ds(rho * quarter + a, seq // dil, stride=DIL_STRIDE), :]

    kd_ref[0:pad, :] = jnp.zeros((pad, LANES), BF16)
    vtd_ref[:, 0:pad] = jnp.zeros((LANES, pad), BF16)
    low = lax.broadcasted_iota(jnp.int32, (BLK, LANES), 1) < HEAD_DIM
    top = lax.broadcasted_iota(jnp.int32, (LANES, BLK), 0) < HEAD_DIM
    ones = jnp.ones((DIL_ONES, 2 * BLK), BF16)

    for p, (_, dil) in enumerate(DIL_PATTERNS):
        sub_len = seq // dil
        nb = sub_len // BLK
        for r in range(dil):
            qd_ref[r * sub_len:(r + 1) * sub_len, :] = residue(0, dil, r).astype(BF16)
            kd_ref[pad + r * sub_len:pad + (r + 1) * sub_len, :] = residue(1, dil, r).astype(BF16)
            vtd_ref[:, pad + r * sub_len:pad + (r + 1) * sub_len] = residue(2, dil, r).astype(BF16).T

        def produce(j, s_ref, p=p, nb=nb):
            for c in range(DIL_CHAINS):
                chain = j * DIL_CHAINS + c
                base = pl.multiple_of(chain * BLK, BLK)
                var = jnp.where((chain & (nb - 1)) == 0, 0, 1)
                q = qd_ref[pl.ds(base, BLK), :]
                zero = jnp.zeros_like(q)
                qs = jnp.concatenate([jnp.where(low, q, zero), jnp.where(low, zero, q)], axis=0)
                kc = kd_ref[pl.ds(base, 2 * BLK), :]
                sc = lax.dot_general(kc, qs, (((1,), (1,)), ((), ())), preferred_element_type=F32)
                s_ref[c] = sc + bias_ref[p, var]

        def consume(j, s_ref):
            for c in range(DIL_CHAINS):
                base = pl.multiple_of((j * DIL_CHAINS + c) * BLK, BLK)
                sc = s_ref[c]
                m = jnp.max(sc, axis=0, keepdims=True)
                pexp = jnp.exp2(sc - m).astype(BF16)
                vt = jnp.concatenate([vtd_ref[:, pl.ds(base, 2 * BLK)], ones], axis=0)
                pv = jnp.dot(vt, pexp, preferred_element_type=F32)
                den = pv[LANES:LANES + 1, :]
                o = pv[0:LANES, :] / den
                lse = m + jnp.log2(den)
                chain = j * DIL_CHAINS + c
                ot_ref[chain] = jnp.where(top, o[:, :BLK], o[:, BLK:])
                lt_ref[chain] = jnp.where(top, lse[:, :BLK], lse[:, BLK:])

        n_trips = seq // BLK // DIL_CHAINS
        produce(0, sa_ref)

        def two_trips(i, carry, produce=produce, consume=consume):
            j = 2 * i
            produce(j + 1, sb_ref)
            consume(j, sa_ref)
            produce(j + 2, sa_ref)
            consume(j + 1, sb_ref)
            return carry

        lax.fori_loop(0, n_trips // 2 - 1, two_trips, 0)
        produce(n_trips - 1, sb_ref)
        consume(n_trips - 2, sa_ref)
        consume(n_trips - 1, sb_ref)
        for chain in range(seq // BLK):
            r, n = divmod(chain, nb)
            if dil <= DIL_STRIDE:
                rows = pl.ds(n * BLK * dil + r, BLK, stride=dil)
                on_ref[p, rows, :] = ot_ref[chain].T
                ln_ref[p, rows, :] = lt_ref[chain].T
            else:
                a, rho = divmod(r, DIL_STRIDE)
                rows = pl.ds(rho * quarter + n * BLK * DIL_STRIDE + a, BLK, stride=DIL_STRIDE)
                q32_ref[rows, :] = ot_ref[chain].T
                k32_ref[rows, :] = lt_ref[chain].T
        if dil > DIL_STRIDE:
            for rho in range(DIL_STRIDE):
                rows = pl.ds(rho, quarter, stride=DIL_STRIDE)
                on_ref[p, rows, :] = q32_ref[rho * quarter:(rho + 1) * quarter, :]
                ln_ref[p, rows, :] = k32_ref[rho * quarter:(rho + 1) * quarter, :]

    def combine(i, carry):
        rows = pl.ds(pl.multiple_of(i * DIL_COMBINE_ROWS, DIL_COMBINE_ROWS), DIL_COMBINE_ROWS)
        l1, l2, l3 = ln_ref[0, rows, :], ln_ref[1, rows, :], ln_ref[2, rows, :]
        mx = jnp.maximum(jnp.maximum(l1, l2), l3)
        e1 = jnp.exp2(l1 - mx)
        e2 = jnp.exp2(l2 - mx)
        e3 = jnp.exp2(l3 - mx)
        num = e1 * on_ref[0, rows, :] + e2 * on_ref[1, rows, :] + e3 * on_ref[2, rows, :]
        y_ref[0, rows, :] = (num / (e1 + e2 + e3)).astype(y_ref.dtype)
        return carry

    lax.fori_loop(0, seq // DIL_COMBINE_ROWS, combine, 0)


def _dilated_mixer(qa, ka, va, bias_a):
    b, s, w = qa.shape
    n_pat = len(DIL_PATTERNS)
    dils = [d for _, d in DIL_PATTERNS]
    assert all(d in (1, DIL_STRIDE, DIL_STRIDE ** 2) for d in dils) and dils[-1] == max(dils)
    pair = pl.BlockSpec((1, s, LANES), lambda i, h: (i, 0, h))
    return pl.pallas_call(
        _dil_kernel,
        grid=(b, HA // 2),
        in_specs=[pair, pair, pair,
                  pl.BlockSpec((n_pat, 2, None, 2 * BLK, 2 * BLK), lambda i, h: (0, 0, h, 0, 0))],
        out_specs=pair,
        out_shape=jax.ShapeDtypeStruct((b, s, w), BF16),
        scratch_shapes=[
            pltpu.VMEM((s, LANES), F32), pltpu.VMEM((s, LANES), F32), pltpu.VMEM((s, LANES), F32),
            pltpu.VMEM((s, LANES), F32), pltpu.VMEM((s, LANES), F32), pltpu.VMEM((s, LANES), F32),
            pltpu.VMEM((s, LANES), BF16), pltpu.VMEM((BLK + s, LANES), BF16),
            pltpu.VMEM((LANES, BLK + s), BF16),
            pltpu.VMEM((s // BLK, LANES, BLK), F32), pltpu.VMEM((s // BLK, LANES, BLK), F32),
            pltpu.VMEM((n_pat, s, LANES), F32), pltpu.VMEM((n_pat, s, LANES), F32),
            pltpu.VMEM((DIL_CHAINS, 2 * BLK, 2 * BLK), F32),
            pltpu.VMEM((DIL_CHAINS, 2 * BLK, 2 * BLK), F32),
        ],
        compiler_params=_cparams(("arbitrary", "arbitrary")),
        name="dilated",
    )(qa, ka, va, bias_a)


DIFF_TQ = 2 * DIFF_T
DIFF_HEADS = 2
DIFF_ONES = 16


def _diff_kernel(q_ref, k_ref, vt_ref, bias_ref, lam_ref, sg_ref, o_ref,
                 qs_ref, m_ref, acc_ref, s0_ref, s1_ref, *, lam_init, n_bias):
    qi = pl.program_id(2)
    t = DIFF_T
    dv = 2 * HEAD_DIM
    n_blk = 2 * DIFF_TQ // t
    width = 2 * DIFF_TQ
    low = lax.broadcasted_iota(jnp.int32, (DIFF_TQ, LANES), 1) < HEAD_DIM
    for hd in range(DIFF_HEADS):
        q = q_ref[0, :, hd * LANES:(hd + 1) * LANES]
        zero = jnp.zeros_like(q)
        qs_ref[hd * width:hd * width + DIFF_TQ, :] = jnp.where(low, q, zero)
        qs_ref[hd * width + DIFF_TQ:(hd + 1) * width, :] = jnp.where(low, zero, q)
    m_ref[...] = jnp.full(m_ref.shape, NEG, F32)
    acc_ref[...] = jnp.zeros(acc_ref.shape, F32)
    ones = jnp.ones((DIFF_ONES, t), BF16)

    def produce(kt, s_ref, blocks):
        off = pl.multiple_of(kt * t, t)
        for hd in range(DIFF_HEADS):
            k = k_ref[0, pl.ds(off, t), hd * LANES:(hd + 1) * LANES]
            for blk in blocks:
                lanes = slice(hd * width + blk * t, hd * width + (blk + 1) * t)
                s = lax.dot_general(k, qs_ref[lanes, :], (((1,), (1,)), ((), ())),
                                    preferred_element_type=F32)
                delta = 2 * qi + blk % 2 - kt
                s_ref[hd * n_blk + blk] = s + bias_ref[hd, jnp.minimum(delta, n_bias - 1)]

    def consume(kt, s_ref, modes):
        off = pl.multiple_of(kt * t, t)
        for hd in range(DIFF_HEADS):
            vt = jnp.concatenate([vt_ref[0, hd * dv:(hd + 1) * dv, pl.ds(off, t)], ones], axis=0)
            for blk, mode in enumerate(modes):
                if mode == "skip":
                    continue
                g = hd * n_blk + blk
                lanes = slice(g * t, (g + 1) * t)
                s = s_ref[g]
                if mode == "diag":
                    key = lax.broadcasted_iota(jnp.int32, (t, t), 0)
                    qry = lax.broadcasted_iota(jnp.int32, (t, t), 1)
                    s = jnp.where(qry >= key, s, NEG)
                m_old = m_ref[:, lanes]
                m_new = jnp.maximum(m_old, jnp.max(s, axis=0, keepdims=True))
                alpha = jnp.exp2(m_old - m_new)
                p = jnp.exp2(s - m_new).astype(BF16)
                pv = jnp.dot(vt, p, preferred_element_type=F32)
                acc_ref[g] = alpha * acc_ref[g] + pv
                m_ref[:, lanes] = m_new

    every = tuple(range(n_blk))
    full = ("full",) * n_blk
    produce(0, s0_ref, every)

    def body(j, carry):
        kt = 2 * j
        produce(kt + 1, s1_ref, every)
        consume(kt, s0_ref, full)
        produce(kt + 2, s0_ref, every)
        consume(kt + 1, s1_ref, full)
        return carry

    lax.fori_loop(0, qi, body, 0)
    produce(2 * qi + 1, s1_ref, (1, 3))
    consume(2 * qi, s0_ref, ("diag", "full", "diag", "full"))
    consume(2 * qi + 1, s1_ref, ("skip", "diag", "skip", "diag"))

    lv = lam_ref[...]
    s01 = jnp.sum(lv[0:1] * lv[1:2], axis=-1, keepdims=True)
    s23 = jnp.sum(lv[2:3] * lv[3:4], axis=-1, keepdims=True)
    lam = jnp.exp(s01) - jnp.exp(s23) + lam_init
    for hd in range(DIFF_HEADS):
        o_t = [acc_ref[hd * n_blk + blk, 0:dv, :] / acc_ref[hd * n_blk + blk, dv:dv + 1, :]
               for blk in range(n_blk)]
        half = n_blk // 2
        ob = jnp.concatenate([o_t[blk] - lam * o_t[half + blk] for blk in range(half)],
                             axis=1)
        ms = jnp.mean(ob * ob, axis=0, keepdims=True)
        y = ob * lax.rsqrt(ms + EPS) * sg_ref[...]
        o_ref[0, :, hd * LANES:(hd + 1) * LANES] = (y * (1.0 - lam_init)).T.astype(o_ref.dtype)


def _diff_attention(qb, kb, vbt, bias_b, lambda_vec, subln_g, layer):
    b, s, w = qb.shape
    t = DIFF_T
    dv = 2 * HEAD_DIM
    n_bias = bias_b.shape[1]
    lam_init = 0.8 - 0.6 * math.exp(-0.3 * layer)
    heads = DIFF_HEADS
    n_lane_blocks = heads * 2 * DIFF_TQ // t
    tile = pl.BlockSpec((1, DIFF_TQ, heads * LANES), lambda i, h, n: (i, n, h))
    return pl.pallas_call(
        functools.partial(_diff_kernel, lam_init=lam_init, n_bias=n_bias),
        grid=(b, HB // heads, s // DIFF_TQ),
        in_specs=[
            tile,
            pl.BlockSpec((1, s, heads * LANES), lambda i, h, n: (i, 0, h)),
            pl.BlockSpec((1, heads * dv, s), lambda i, h, n: (i, h, 0)),
            pl.BlockSpec((heads, n_bias, t, t), lambda i, h, n: (h, 0, 0, 0)),
            pl.BlockSpec(lambda_vec.shape, lambda i, h, n: (0, 0)),
            pl.BlockSpec((dv, 1), lambda i, h, n: (0, 0)),
        ],
        out_specs=tile,
        out_shape=jax.ShapeDtypeStruct((b, s, w), BF16),
        scratch_shapes=[pltpu.VMEM((heads * 2 * DIFF_TQ, LANES), BF16),
                        pltpu.VMEM((1, heads * 2 * DIFF_TQ), F32),
                        pltpu.VMEM((n_lane_blocks, dv + DIFF_ONES, t), F32),
                        pltpu.VMEM((n_lane_blocks, t, t), F32),
                        pltpu.VMEM((n_lane_blocks, t, t), F32)],
        compiler_params=_cparams(("arbitrary", "arbitrary", "arbitrary")),
        name="diff_attn",
    )(qb, kb, vbt, bias_b, lambda_vec, subln_g.reshape(dv, 1))


MERGE_TM = 512
CONV_HALO = 32
CONV_ROWS = 128


def _conv_tile(prev_ref, cur_ref, w_ref, cb_ref, lg_ref, lb_ref, sh_ref, y_ref, first_tile):
    halo = prev_ref[0]
    rows_in = cur_ref.shape[1]
    sh_ref[0, 0:CONV_HALO, :] = jnp.where(first_tile, jnp.zeros_like(halo), halo)
    sh_ref[0, CONV_HALO:, :] = cur_ref[0]
    for b in range(1, SUBLANES):
        sh_ref[b] = pltpu.roll(sh_ref[0], b, axis=0)
    for r0 in range(0, rows_in, CONV_ROWS):
        acc = jnp.zeros((CONV_ROWS, C_CONV), F32)
        for tap in range(CONV_K):
            back = CONV_K - 1 - tap
            a, b = divmod(back, SUBLANES)
            lo = r0 + CONV_HALO - a * SUBLANES
            acc = acc + sh_ref[b, lo:lo + CONV_ROWS, :] * w_ref[tap:tap + 1, :]
        y_ref[r0:r0 + CONV_ROWS, :] = acc
    y = y_ref[...] + cb_ref[...]
    mu = jnp.mean(y, axis=-1, keepdims=True)
    var = jnp.mean(jnp.square(y - mu), axis=-1, keepdims=True)
    z = (y - mu) * lax.rsqrt(var + EPS) * lg_ref[...] + lb_ref[...]
    return z * _sigmoid(z)


def _merge_kernel(x_ref, mod_ref, g_ref, ya_ref, yb_ref, uprev_ref, u_ref,
                  cw_ref, cb_ref, lg_ref, lb_ref, wg_ref, bg_ref, wb_ref, wo_ref, o_ref,
                  sh_ref, yc_ref):
    x = x_ref[0]
    d = x.shape[-1]
    h = _norm_mod(x, g_ref[...], mod_ref[0, 3:4, :], mod_ref[0, 4:5, :])
    hb = h.astype(BF16)
    y_c = _conv_tile(uprev_ref, u_ref, cw_ref, cb_ref, lg_ref, lb_ref, sh_ref, yc_ref,
                     pl.program_id(1) == 0).astype(BF16)
    mixed = jnp.zeros(x.shape, F32)
    for i, y in enumerate((ya_ref[0], yb_ref[0], y_c)):
        z = jnp.dot(hb, wg_ref[:, i * d:(i + 1) * d], preferred_element_type=F32)
        gate = _sigmoid(z + bg_ref[:, i * d:(i + 1) * d])
        mixed = mixed + gate * jnp.dot(y, wb_ref[i], preferred_element_type=F32)
    y = jnp.dot(mixed.astype(BF16), wo_ref[...], preferred_element_type=F32)
    o_ref[0] = x + mod_ref[0, 5:6, :] * y


def _merge(x, mod_l, g, y_a, y_b, u, conv_w, conv_b, ln_g, ln_b, layer,
           w_gate, b_gate, w_branch, w_out):
    b, s, d = x.shape
    c = u.shape[-1]
    per_tile = MERGE_TM // CONV_HALO
    tok = pl.BlockSpec((1, MERGE_TM, d), lambda i, j: (i, j, 0))
    br = pl.BlockSpec((1, MERGE_TM, MIX_W), lambda i, j: (i, j, 0))
    return pl.pallas_call(
        _merge_kernel,
        grid=(b, s // MERGE_TM),
        in_specs=[
            tok,
            pl.BlockSpec((1, 9, d), lambda i, j: (i, 0, 0)),
            _resident((1, d)),
            br, br,
            pl.BlockSpec((1, CONV_HALO, c), lambda i, j: (i, jnp.maximum(j * per_tile - 1, 0), 0)),
            br,
            _resident((CONV_K, c)), _resident((1, c)), _resident((1, c)), _resident((1, c)),
            _resident(w_gate.shape[1:], (layer,)),
            _resident((1, N_BRANCH * d)),
            _resident(w_branch.shape[1:], (layer,)),
            _resident(w_out.shape[1:], (layer,)),
        ],
        out_specs=tok,
        out_shape=jax.ShapeDtypeStruct(x.shape, F32),
        scratch_shapes=[pltpu.VMEM((SUBLANES, CONV_HALO + MERGE_TM, c), F32),
                        pltpu.VMEM((MERGE_TM, c), F32)],
        compiler_params=_cparams(("arbitrary", "arbitrary")),
        name="merge",
    )(x, mod_l, g.reshape(1, d), y_a, y_b, u, u, conv_w, conv_b.reshape(1, c),
      ln_g.reshape(1, c), ln_b.reshape(1, c), w_gate, b_gate.reshape(1, N_BRANCH * d),
      w_branch, w_out)


def _qk_gain_rows(qk_gain):
    ga_q = jnp.tile(qk_gain[0], HA)
    ga_k = jnp.tile(qk_gain[1], HA)
    gb_q = jnp.tile(jnp.concatenate([qk_gain[2], qk_gain[3]]), HB)
    gb_k = jnp.tile(jnp.concatenate([qk_gain[4], qk_gain[5]]), HB)
    return jnp.stack([ga_q, ga_k, gb_q, gb_k])


def kernel(x, c, rel_bias, w_ada, b_ada, norm_g, w_ffn_in, w_ffn_out, w_in, qk_gain, lambda_vec,
           subln_g, conv_w, conv_b, conv_ln_g, conv_ln_b, w_branch, w_gate, b_gate, w_out):
    b, s, d = x.shape
    mod = _ada_mod(c, w_ada, b_ada)
    bias_a, bias_b = _bias_tiles(rel_bias, s)
    w_ffn_in, w_ffn_out, w_in = w_ffn_in.astype(BF16), w_ffn_out.astype(BF16), w_in.astype(BF16)
    w_gate, w_branch, w_out = w_gate.astype(BF16), w_branch.astype(BF16), w_out.astype(BF16)
    for l in range(DEPTH):
        mod_l = mod[l]
        x = _ffn(x, mod_l, norm_g[l, 0], w_ffn_in, w_ffn_out, (l, 0), 0)
        qa, ka, va, qb, kb, vbt, u = _proj(x, mod_l, norm_g[l, 1], w_in, l,
                                          _qk_gain_rows(qk_gain[l]))
        y_a = _dilated_mixer(qa, ka, va, bias_a)
        y_b = _diff_attention(qb, kb, vbt, bias_b, lambda_vec[l], subln_g[l], l)
        x = _merge(x, mod_l, norm_g[l, 1], y_a, y_b, u, conv_w[l], conv_b[l], conv_ln_g[l],
                   conv_ln_b[l], l, w_gate, b_gate[l], w_branch, w_out)
        x = _ffn(x, mod_l, norm_g[l, 2], w_ffn_in, w_ffn_out, (l, 1), 6)
    return x
```

```python
import functools
import math

import numpy as np
import jax
import jax.numpy as jnp
from jax import lax
from jax.experimental import pallas as pl
from jax.experimental.pallas import tpu as pltpu

D_MODEL = 1024
DEPTH = 2
HEAD_DIM = 64
HA = 8
DIL_PATTERNS = ((128, 1), (512, 4), (2048, 16))
HB = 4
C_CONV = 512
CONV_K = 31
D_FF = 2816
N_BUCKETS = 32
REL_MAX_DIST = 2048
BLK = 128
MIX_W = 512
N_BRANCH = 3
EPS = 1e-6
NEG = -1e30
LOG2E = math.log2(math.e)

LANES = 128
SUBLANES = 8
VMEM_LIMIT = 56 * 1024 * 1024

F32 = jnp.float32
BF16 = jnp.bfloat16


def _cparams(sem):
    return pltpu.CompilerParams(dimension_semantics=sem, vmem_limit_bytes=VMEM_LIMIT)


def _sigmoid(x):
    return 1.0 / (1.0 + jnp.exp(-x))


def _resident(shape, lead=()):
    lead = tuple(lead)
    zeros = (0,) * len(shape)
    return pl.BlockSpec((None,) * len(lead) + tuple(shape), lambda *_: lead + zeros,
                        pipeline_mode=pl.Buffered(1))


def _norm_mod(x, g, shift, scale):
    ms = jnp.mean(x * x, axis=-1, keepdims=True)
    y = x * lax.rsqrt(ms + EPS) * g
    return y * (1.0 + scale) + shift


ADA_TN = 2304


def _ada_kernel(c_ref, w_ref, b_ref, o_ref):
    c = c_ref[...]
    a = c * _sigmoid(c)
    o_ref[0] = jnp.dot(a, w_ref[0], preferred_element_type=F32,
                       precision=lax.Precision.HIGHEST) + b_ref[0]


def _ada_mod(c, w_ada, b_ada):
    b, d = c.shape
    rows = 8
    c_pad = jnp.pad(c, ((0, rows - b), (0, 0)))
    n = w_ada.shape[-1]
    out = pl.pallas_call(
        _ada_kernel,
        grid=(DEPTH, n // ADA_TN),
        in_specs=[
            pl.BlockSpec((rows, d), lambda l, j: (0, 0)),
            pl.BlockSpec((1, d, ADA_TN), lambda l, j: (l, 0, j)),
            pl.BlockSpec((1, 1, ADA_TN), lambda l, j: (l, 0, j)),
        ],
        out_specs=pl.BlockSpec((1, rows, ADA_TN), lambda l, j: (l, 0, j)),
        out_shape=jax.ShapeDtypeStruct((DEPTH, rows, n), F32),
        compiler_params=_cparams(("arbitrary", "arbitrary")),
        name="ada_mod",
    )(c_pad, w_ada, b_ada.reshape(DEPTH, 1, n))
    return out[:, :b].reshape(DEPTH, b, 9, d)


FFN_TM = 512
FFN_TF = 256


def _ffn_kernel(x_ref, mod_ref, g_ref, wup_ref, wdn_ref, o_ref, act_ref, *, k0):
    x = x_ref[0]
    h = _norm_mod(x, g_ref[...], mod_ref[0, k0:k0 + 1, :], mod_ref[0, k0 + 1:k0 + 2, :])
    hb = h.astype(BF16)
    for j in range(D_FF // FFN_TF):
        lo = j * FFN_TF
        gate = jnp.dot(hb, wup_ref[:, lo:lo + FFN_TF], preferred_element_type=F32)
        up = jnp.dot(hb, wup_ref[:, D_FF + lo:D_FF + lo + FFN_TF], preferred_element_type=F32)
        act_ref[:, lo:lo + FFN_TF] = (gate * _sigmoid(gate) * up).astype(BF16)
    y = jnp.dot(act_ref[...], wdn_ref[...], preferred_element_type=F32)
    o_ref[0] = x + (0.5 * mod_ref[0, k0 + 2:k0 + 3, :]) * y


def _ffn(x, mod_l, g, w_up, w_dn, lead, k0):
    b, s, d = x.shape
    tok = pl.BlockSpec((1, FFN_TM, d), lambda i, j: (i, j, 0))
    return pl.pallas_call(
        functools.partial(_ffn_kernel, k0=k0),
        grid=(b, s // FFN_TM),
        in_specs=[
            tok,
            pl.BlockSpec((1, 9, d), lambda i, j: (i, 0, 0)),
            _resident((1, d)),
            _resident((d, 2 * D_FF), lead),
            _resident((D_FF, d), lead),
        ],
        out_specs=tok,
        out_shape=jax.ShapeDtypeStruct(x.shape, F32),
        scratch_shapes=[pltpu.VMEM((FFN_TM, D_FF), BF16)],
        compiler_params=_cparams(("arbitrary", "arbitrary")),
        name="ffn",
    )(x, mod_l, g.reshape(1, d), w_up, w_dn)


PROJ_TM = 512


def _head_rmsnorm(acc, gain):
    rows = acc.shape[0]
    low = lax.broadcasted_iota(jnp.int32, (rows, LANES), 1) < HEAD_DIM
    outs = []
    for c in range(acc.shape[1] // LANES):
        xc = acc[:, c * LANES:(c + 1) * LANES]
        sq = xc * xc
        s_lo = jnp.sum(jnp.where(low, sq, 0.0), axis=-1, keepdims=True)
        s_hi = jnp.sum(jnp.where(low, 0.0, sq), axis=-1, keepdims=True)
        r_lo = lax.rsqrt(s_lo * (1.0 / HEAD_DIM) + EPS)
        r_hi = lax.rsqrt(s_hi * (1.0 / HEAD_DIM) + EPS)
        outs.append(xc * jnp.where(low, r_lo, r_hi) * gain[:, c * LANES:(c + 1) * LANES])
    return jnp.concatenate(outs, axis=-1)


def _proj_kernel(x_ref, mod_ref, g_ref, w_ref, gain_ref,
                 qa_ref, ka_ref, va_ref, qbt_ref, kb_ref, vbt_ref, u_ref):
    x = x_ref[0]
    h = _norm_mod(x, g_ref[...], mod_ref[0, 3:4, :], mod_ref[0, 4:5, :])
    hb = h.astype(BF16)
    w = MIX_W

    def col(j):
        return jnp.dot(hb, w_ref[:, j * w:(j + 1) * w], preferred_element_type=F32)

    q_scale = 1.0 / math.sqrt(HEAD_DIM)
    qa_ref[0] = (_head_rmsnorm(col(0), gain_ref[0:1, :]) * (q_scale * LOG2E)).astype(BF16)
    ka_ref[0] = _head_rmsnorm(col(1), gain_ref[1:2, :]).astype(BF16)
    va_ref[0] = col(2).astype(BF16)
    qbt_ref[0] = (_head_rmsnorm(col(3), gain_ref[2:3, :]) * (q_scale * LOG2E)).T.astype(BF16)
    kb_ref[0] = _head_rmsnorm(col(4), gain_ref[3:4, :]).astype(BF16)
    vbt_ref[0] = col(5).T.astype(BF16)
    u_ref[0] = col(6) * _sigmoid(col(7))


def _proj(x, mod_l, g, w_in, layer, gains):
    b, s, d = x.shape
    tok_in = pl.BlockSpec((1, PROJ_TM, d), lambda i, j: (i, j, 0))
    tok_out = pl.BlockSpec((1, PROJ_TM, MIX_W), lambda i, j: (i, j, 0))
    bf = jax.ShapeDtypeStruct((b, s, MIX_W), BF16)
    bf_t = jax.ShapeDtypeStruct((b, MIX_W, s), BF16)
    tr_out = pl.BlockSpec((1, MIX_W, PROJ_TM), lambda i, j: (i, 0, j))
    return pl.pallas_call(
        _proj_kernel,
        grid=(b, s // PROJ_TM),
        in_specs=[
            tok_in,
            pl.BlockSpec((1, 9, d), lambda i, j: (i, 0, 0)),
            _resident((1, d)),
            _resident(w_in.shape[1:], (layer,)),
            _resident(gains.shape),
        ],
        out_specs=[tok_out] * 3 + [tr_out, tok_out, tr_out, tok_out],
        out_shape=[bf, bf, bf, bf_t, bf, bf_t, jax.ShapeDtypeStruct((b, s, MIX_W), F32)],
        compiler_params=_cparams(("arbitrary", "arbitrary")),
        name="proj",
    )(x, mod_l, g.reshape(1, d), w_in, gains)


def _bucket_thresholds():
    max_exact = N_BUCKETS // 2
    d = np.arange(0, 2 * REL_MAX_DIST + 2)
    df = np.maximum(d.astype(np.float32), np.float32(1.0))
    large = max_exact + (np.log(df / np.float32(max_exact))
                         / np.float32(math.log(REL_MAX_DIST / max_exact))
                         * np.float32(N_BUCKETS - max_exact)).astype(np.int32)
    bucket = np.where(d < max_exact, d, np.minimum(large, N_BUCKETS - 1))
    return [int(np.argmax(bucket >= b)) for b in range(1, N_BUCKETS)]


_THRESHOLDS = _bucket_thresholds()


def _bias_of_dist(dist, tab_ref, head):
    val = jnp.full(dist.shape, tab_ref[0, head], F32)
    for b in range(1, N_BUCKETS):
        val = jnp.where(dist >= _THRESHOLDS[b - 1], tab_ref[b, head], val)
    return val


DIFF_T = 256


def _diff_bias_tiles(seq):
    first_const = -(-(_THRESHOLDS[-1] + DIFF_T - 1) // DIFF_T)
    return min(seq // DIFF_T, first_const + 1)


def _bias_diff_kernel(tab_ref, o_ref):
    h = pl.program_id(0)
    delta = pl.program_id(1)
    j = lax.broadcasted_iota(jnp.int32, (DIFF_T, DIFF_T), 0)
    i = lax.broadcasted_iota(jnp.int32, (DIFF_T, DIFF_T), 1)
    dist = jnp.maximum(delta * DIFF_T + i - j, 0)
    o_ref[0, 0] = _bias_of_dist(dist, tab_ref, HA + h) * LOG2E


def _bias_dil_kernel(tab_ref, o_ref):
    p = pl.program_id(0)
    hp = pl.program_id(1)
    dil = jnp.where(p == 0, DIL_PATTERNS[0][1],
                    jnp.where(p == 1, DIL_PATTERNS[1][1], DIL_PATTERNS[2][1]))
    j = lax.broadcasted_iota(jnp.int32, (2 * BLK, BLK), 0)
    i = lax.broadcasted_iota(jnp.int32, (2 * BLK, BLK), 1)
    rel = i + BLK - j
    band = (rel >= 0) & (rel <= BLK)
    dist = jnp.maximum(rel, 0) * dil
    for sub in range(2):
        bias = _bias_of_dist(dist, tab_ref, 2 * hp + sub) * LOG2E
        lanes = slice(sub * BLK, (sub + 1) * BLK)
        o_ref[0, 0, 0, :, lanes] = jnp.where(band & (j >= BLK), bias, NEG)
        o_ref[0, 1, 0, :, lanes] = jnp.where(band, bias, NEG)


def _bias_tiles(rel_bias, seq):
    smem = pl.BlockSpec(memory_space=pltpu.SMEM)
    n_delta = _diff_bias_tiles(seq)
    bias_b = pl.pallas_call(
        _bias_diff_kernel,
        grid=(HB, n_delta),
        in_specs=[smem],
        out_specs=pl.BlockSpec((1, 1, DIFF_T, DIFF_T), lambda h, t: (h, t, 0, 0)),
        out_shape=jax.ShapeDtypeStruct((HB, n_delta, DIFF_T, DIFF_T), F32),
        compiler_params=_cparams(("arbitrary", "arbitrary")),
        name="bias_diff",
    )(rel_bias)
    n_pat = len(DIL_PATTERNS)
    bias_a = pl.pallas_call(
        _bias_dil_kernel,
        grid=(n_pat, HA // 2),
        in_specs=[smem],
        out_specs=pl.BlockSpec((1, 2, 1, 2 * BLK, 2 * BLK), lambda p, h: (p, 0, h, 0, 0)),
        out_shape=jax.ShapeDtypeStruct((n_pat, 2, HA // 2, 2 * BLK, 2 * BLK), F32),
        compiler_params=_cparams(("arbitrary", "arbitrary")),
        name="bias_dil",
    )(rel_bias)
    return bias_a, bias_b


DIL_CHAINS = 4
DIL_STRIDE = 4
DIL_ONES = 16
DIL_COMBINE_ROWS = 512


def _dil_kernel(q_ref, k_ref, v_ref, bias_ref, y_ref,
                q32_ref, k32_ref, v32_ref, q4_ref, k4_ref, v4_ref, qd_ref, kd_ref, vtd_ref,
                ot_ref, lt_ref, on_ref, ln_ref, sa_ref, sb_ref):
    seq = q_ref.shape[1]
    pad = BLK
    quarter = seq // DIL_STRIDE
    nat = (q32_ref, k32_ref, v32_ref)
    by4 = (q4_ref, k4_ref, v4_ref)
    for src_ref, nat_ref, by4_ref in zip((q_ref, k_ref, v_ref), nat, by4):
        nat_ref[...] = src_ref[0].astype(F32)
        for rho in range(DIL_STRIDE):
            by4_ref[rho * quarter:(rho + 1) * quarter, :] = nat_ref[pl.ds(rho, quarter, stride=DIL_STRIDE), :]

    def residue(which, dil, r):
        if dil == 1:
            return nat[which][...]
        if dil == DIL_STRIDE:
            return by4[which][r * quarter:(r + 1) * quarter, :]
        a, rho = divmod(r, DIL_STRIDE)
        return by4[which][pl.ds(rho * quarter + a, seq // dil, stride=DIL_STRIDE), :]

    kd_ref[0:pad, :] = jnp.zeros((pad, LANES), BF16)
    vtd_ref[:, 0:pad] = jnp.zeros((LANES, pad), BF16)
    low = lax.broadcasted_iota(jnp.int32, (BLK, LANES), 1) < HEAD_DIM
    top = lax.broadcasted_iota(jnp.int32, (LANES, BLK), 0) < HEAD_DIM
    ones = jnp.ones((DIL_ONES, 2 * BLK), BF16)

    for p, (_, dil) in enumerate(DIL_PATTERNS):
        sub_len = seq // dil
        nb = sub_len // BLK
        for r in range(dil):
            qd_ref[r * sub_len:(r + 1) * sub_len, :] = residue(0, dil, r).astype(BF16)
            kd_ref[pad + r * sub_len:pad + (r + 1) * sub_len, :] = residue(1, dil, r).astype(BF16)
            vtd_ref[:, pad + r * sub_len:pad + (r + 1) * sub_len] = residue(2, dil, r).astype(BF16).T

        def produce(j, s_ref, p=p, nb=nb):
            for c in range(DIL_CHAINS):
                chain = j * DIL_CHAINS + c
                base = pl.multiple_of(chain * BLK, BLK)
                var = jnp.where((chain & (nb - 1)) == 0, 0, 1)
                q = qd_ref[pl.ds(base, BLK), :]
                zero = jnp.zeros_like(q)
                qs = jnp.concatenate([jnp.where(low, q, zero), jnp.where(low, zero, q)], axis=0)
                kc = kd_ref[pl.ds(base, 2 * BLK), :]
                sc = lax.dot_general(kc, qs, (((1,), (1,)), ((), ())), preferred_element_type=F32)
                s_ref[c] = sc + bias_ref[p, var]

        def consume(j, s_ref):
            for c in range(DIL_CHAINS):
                base = pl.multiple_of((j * DIL_CHAINS + c) * BLK, BLK)
                sc = s_ref[c]
                m = jnp.max(sc, axis=0, keepdims=True)
                pexp = jnp.exp2(sc - m).astype(BF16)
                vt = jnp.concatenate([vtd_ref[:, pl.ds(base, 2 * BLK)], ones], axis=0)
                pv = jnp.dot(vt, pexp, preferred_element_type=F32)
                den = pv[LANES:LANES + 1, :]
                o = pv[0:LANES, :] / den
                lse = m + jnp.log2(den)
                chain = j * DIL_CHAINS + c
                ot_ref[chain] = jnp.where(top, o[:, :BLK], o[:, BLK:])
                lt_ref[chain] = jnp.where(top, lse[:, :BLK], lse[:, BLK:])

        n_trips = seq // BLK // DIL_CHAINS
        produce(0, sa_ref)

        def two_trips(i, carry, produce=produce, consume=consume):
            j = 2 * i
            produce(j + 1, sb_ref)
            consume(j, sa_ref)
            produce(j + 2, sa_ref)
            consume(j + 1, sb_ref)
            return carry

        lax.fori_loop(0, n_trips // 2 - 1, two_trips, 0)
        produce(n_trips - 1, sb_ref)
        consume(n_trips - 2, sa_ref)
        consume(n_trips - 1, sb_ref)
        for chain in range(seq // BLK):
            r, n = divmod(chain, nb)
            if dil <= DIL_STRIDE:
                rows = pl.ds(n * BLK * dil + r, BLK, stride=dil)
                on_ref[p, rows, :] = ot_ref[chain].T
                ln_ref[p, rows, :] = lt_ref[chain].T
            else:
                a, rho = divmod(r, DIL_STRIDE)
                rows = pl.ds(rho * quarter + n * BLK * DIL_STRIDE + a, BLK, stride=DIL_STRIDE)
                q32_ref[rows, :] = ot_ref[chain].T
                k32_ref[rows, :] = lt_ref[chain].T
        if dil > DIL_STRIDE:
            for rho in range(DIL_STRIDE):
                rows = pl.ds(rho, quarter, stride=DIL_STRIDE)
                on_ref[p, rows, :] = q32_ref[rho * quarter:(rho + 1) * quarter, :]
                ln_ref[p, rows, :] = k32_ref[rho * quarter:(rho + 1) * quarter, :]

    def combine(i, carry):
        rows = pl.ds(pl.multiple_of(i * DIL_COMBINE_ROWS, DIL_COMBINE_ROWS), DIL_COMBINE_ROWS)
        l1, l2, l3 = ln_ref[0, rows, :], ln_ref[1, rows, :], ln_ref[2, rows, :]
        mx = jnp.maximum(jnp.maximum(l1, l2), l3)
        e1 = jnp.exp2(l1 - mx)
        e2 = jnp.exp2(l2 - mx)
        e3 = jnp.exp2(l3 - mx)
        num = e1 * on_ref[0, rows, :] + e2 * on_ref[1, rows, :] + e3 * on_ref[2, rows, :]
        y_ref[0, rows, :] = (num / (e1 + e2 + e3)).astype(y_ref.dtype)
        return carry

    lax.fori_loop(0, seq // DIL_COMBINE_ROWS, combine, 0)


def _dilated_mixer(qa, ka, va, bias_a):
    b, s, w = qa.shape
    n_pat = len(DIL_PATTERNS)
    dils = [d for _, d in DIL_PATTERNS]
    assert all(d in (1, DIL_STRIDE, DIL_STRIDE ** 2) for d in dils) and dils[-1] == max(dils)
    pair = pl.BlockSpec((1, s, LANES), lambda i, h: (i, 0, h))
    return pl.pallas_call(
        _dil_kernel,
        grid=(b, HA // 2),
        in_specs=[pair, pair, pair,
                  pl.BlockSpec((n_pat, 2, None, 2 * BLK, 2 * BLK), lambda i, h: (0, 0, h, 0, 0))],
        out_specs=pair,
        out_shape=jax.ShapeDtypeStruct((b, s, w), BF16),
        scratch_shapes=[
            pltpu.VMEM((s, LANES), F32), pltpu.VMEM((s, LANES), F32), pltpu.VMEM((s, LANES), F32),
            pltpu.VMEM((s, LANES), F32), pltpu.VMEM((s, LANES), F32), pltpu.VMEM((s, LANES), F32),
            pltpu.VMEM((s, LANES), BF16), pltpu.VMEM((BLK + s, LANES), BF16),
            pltpu.VMEM((LANES, BLK + s), BF16),
            pltpu.VMEM((s // BLK, LANES, BLK), F32), pltpu.VMEM((s // BLK, LANES, BLK), F32),
            pltpu.VMEM((n_pat, s, LANES), F32), pltpu.VMEM((n_pat, s, LANES), F32),
            pltpu.VMEM((DIL_CHAINS, 2 * BLK, 2 * BLK), F32),
            pltpu.VMEM((DIL_CHAINS, 2 * BLK, 2 * BLK), F32),
        ],
        compiler_params=_cparams(("arbitrary", "arbitrary")),
        name="dilated",
    )(qa, ka, va, bias_a)


DIFF_TQ = 2 * DIFF_T
DIFF_HEADS = 2
DIFF_ONES = 16


def _diff_kernel(qt_ref, k_ref, vt_ref, bias_ref, lam_ref, sg_ref, o_ref,
                 qs_ref, m_ref, acc_ref, s0_ref, s1_ref, *, lam_init, n_bias):
    qi = pl.program_id(2)
    t = DIFF_T
    dv = 2 * HEAD_DIM
    n_blk = 2 * DIFF_TQ // t
    width = 2 * DIFF_TQ
    first = lax.broadcasted_iota(jnp.int32, (LANES, DIFF_TQ), 0) < HEAD_DIM
    for hd in range(DIFF_HEADS):
        q = qt_ref[0, hd * LANES:(hd + 1) * LANES, :]
        zero = jnp.zeros_like(q)
        qs_ref[:, hd * width:hd * width + DIFF_TQ] = jnp.where(first, q, zero)
        qs_ref[:, hd * width + DIFF_TQ:(hd + 1) * width] = jnp.where(first, zero, q)
    m_ref[...] = jnp.full(m_ref.shape, NEG, F32)
    acc_ref[...] = jnp.zeros(acc_ref.shape, F32)
    ones = jnp.ones((DIFF_ONES, t), BF16)

    def produce(kt, s_ref, blocks):
        off = pl.multiple_of(kt * t, t)
        for hd in range(DIFF_HEADS):
            k = k_ref[0, pl.ds(off, t), hd * LANES:(hd + 1) * LANES]
            for blk in blocks:
                lanes = slice(hd * width + blk * t, hd * width + (blk + 1) * t)
                s = jnp.dot(k, qs_ref[:, lanes], preferred_element_type=F32)
                delta = 2 * qi + blk % 2 - kt
                s_ref[hd * n_blk + blk] = s + bias_ref[hd, jnp.minimum(delta, n_bias - 1)]

    def consume(kt, s_ref, modes):
        off = pl.multiple_of(kt * t, t)
        for hd in range(DIFF_HEADS):
            vt = jnp.concatenate([vt_ref[0, hd * dv:(hd + 1) * dv, pl.ds(off, t)], ones], axis=0)
            for blk, mode in enumerate(modes):
                if mode == "skip":
                    continue
                g = hd * n_blk + blk
                lanes = slice(g * t, (g + 1) * t)
                s = s_ref[g]
                if mode == "diag":
                    key = lax.broadcasted_iota(jnp.int32, (t, t), 0)
                    qry = lax.broadcasted_iota(jnp.int32, (t, t), 1)
                    s = jnp.where(qry >= key, s, NEG)
                m_old = m_ref[:, lanes]
                m_new = jnp.maximum(m_old, jnp.max(s, axis=0, keepdims=True))
                alpha = jnp.exp2(m_old - m_new)
                p = jnp.exp2(s - m_new).astype(BF16)
                pv = jnp.dot(vt, p, preferred_element_type=F32)
                acc_ref[g] = alpha * acc_ref[g] + pv
                m_ref[:, lanes] = m_new

    every = tuple(range(n_blk))
    full = ("full",) * n_blk
    produce(0, s0_ref, every)

    def body(j, carry):
        kt = 2 * j
        produce(kt + 1, s1_ref, every)
        consume(kt, s0_ref, full)
        produce(kt + 2, s0_ref, every)
        consume(kt + 1, s1_ref, full)
        return carry

    lax.fori_loop(0, qi, body, 0)
    produce(2 * qi + 1, s1_ref, (1, 3))
    consume(2 * qi, s0_ref, ("diag", "full", "diag", "full"))
    consume(2 * qi + 1, s1_ref, ("skip", "diag", "skip", "diag"))

    lv = lam_ref[...]
    s01 = jnp.sum(lv[0:1] * lv[1:2], axis=-1, keepdims=True)
    s23 = jnp.sum(lv[2:3] * lv[3:4], axis=-1, keepdims=True)
    lam = jnp.exp(s01) - jnp.exp(s23) + lam_init
    for hd in range(DIFF_HEADS):
        o_t = [acc_ref[hd * n_blk + blk, 0:dv, :] / acc_ref[hd * n_blk + blk, dv:dv + 1, :]
               for blk in range(n_blk)]
        half = n_blk // 2
        ob = jnp.concatenate([o_t[blk] - lam * o_t[half + blk] for blk in range(half)],
                             axis=1)
        ms = jnp.mean(ob * ob, axis=0, keepdims=True)
        y = ob * lax.rsqrt(ms + EPS) * sg_ref[...]
        o_ref[0, :, hd * LANES:(hd + 1) * LANES] = (y * (1.0 - lam_init)).T.astype(o_ref.dtype)


def _diff_attention(qbt, kb, vbt, bias_b, lambda_vec, subln_g, layer):
    b, s, w = kb.shape
    t = DIFF_T
    dv = 2 * HEAD_DIM
    n_bias = bias_b.shape[1]
    lam_init = 0.8 - 0.6 * math.exp(-0.3 * layer)
    heads = DIFF_HEADS
    n_lane_blocks = heads * 2 * DIFF_TQ // t
    tile = pl.BlockSpec((1, DIFF_TQ, heads * LANES), lambda i, h, n: (i, n, h))
    return pl.pallas_call(
        functools.partial(_diff_kernel, lam_init=lam_init, n_bias=n_bias),
        grid=(b, HB // heads, s // DIFF_TQ),
        in_specs=[
            pl.BlockSpec((1, heads * LANES, DIFF_TQ), lambda i, h, n: (i, h, n)),
            pl.BlockSpec((1, s, heads * LANES), lambda i, h, n: (i, 0, h)),
            pl.BlockSpec((1, heads * dv, s), lambda i, h, n: (i, h, 0)),
            pl.BlockSpec((heads, n_bias, t, t), lambda i, h, n: (h, 0, 0, 0)),
            pl.BlockSpec(lambda_vec.shape, lambda i, h, n: (0, 0)),
            pl.BlockSpec((dv, 1), lambda i, h, n: (0, 0)),
        ],
        out_specs=tile,
        out_shape=jax.ShapeDtypeStruct((b, s, w), BF16),
        scratch_shapes=[pltpu.VMEM((LANES, heads * 2 * DIFF_TQ), BF16),
                        pltpu.VMEM((1, heads * 2 * DIFF_TQ), F32),
                        pltpu.VMEM((n_lane_blocks, dv + DIFF_ONES, t), F32),
                        pltpu.VMEM((n_lane_blocks, t, t), F32),
                        pltpu.VMEM((n_lane_blocks, t, t), F32)],
        compiler_params=_cparams(("arbitrary", "arbitrary", "arbitrary")),
        name="diff_attn",
    )(qbt, kb, vbt, bias_b, lambda_vec, subln_g.reshape(dv, 1))


MERGE_TM = 512
CONV_HALO = 32
CONV_ROWS = 128


def _conv_tile(prev_ref, cur_ref, w_ref, cb_ref, lg_ref, lb_ref, sh_ref, y_ref, first_tile):
    halo = prev_ref[0]
    rows_in = cur_ref.shape[1]
    sh_ref[0, 0:CONV_HALO, :] = jnp.where(first_tile, jnp.zeros_like(halo), halo)
    sh_ref[0, CONV_HALO:, :] = cur_ref[0]
    for b in range(1, SUBLANES):
        sh_ref[b] = pltpu.roll(sh_ref[0], b, axis=0)
    for r0 in range(0, rows_in, CONV_ROWS):
        acc = jnp.zeros((CONV_ROWS, C_CONV), F32)
        for tap in range(CONV_K):
            back = CONV_K - 1 - tap
            a, b = divmod(back, SUBLANES)
            lo = r0 + CONV_HALO - a * SUBLANES
            acc = acc + sh_ref[b, lo:lo + CONV_ROWS, :] * w_ref[tap:tap + 1, :]
        y_ref[r0:r0 + CONV_ROWS, :] = acc
    y = y_ref[...] + cb_ref[...]
    mu = jnp.mean(y, axis=-1, keepdims=True)
    var = jnp.mean(jnp.square(y - mu), axis=-1, keepdims=True)
    z = (y - mu) * lax.rsqrt(var + EPS) * lg_ref[...] + lb_ref[...]
    return z * _sigmoid(z)


def _merge_kernel(x_ref, mod_ref, g_ref, ya_ref, yb_ref, uprev_ref, u_ref,
                  cw_ref, cb_ref, lg_ref, lb_ref, wg_ref, bg_ref, wb_ref, wo_ref, o_ref,
                  sh_ref, yc_ref):
    x = x_ref[0]
    d = x.shape[-1]
    h = _norm_mod(x, g_ref[...], mod_ref[0, 3:4, :], mod_ref[0, 4:5, :])
    hb = h.astype(BF16)
    y_c = _conv_tile(uprev_ref, u_ref, cw_ref, cb_ref, lg_ref, lb_ref, sh_ref, yc_ref,
                     pl.program_id(1) == 0).astype(BF16)
    mixed = jnp.zeros(x.shape, F32)
    for i, y in enumerate((ya_ref[0], yb_ref[0], y_c)):
        z = jnp.dot(hb, wg_ref[:, i * d:(i + 1) * d], preferred_element_type=F32)
        gate = _sigmoid(z + bg_ref[:, i * d:(i + 1) * d])
        mixed = mixed + gate * jnp.dot(y, wb_ref[i], preferred_element_type=F32)
    y = jnp.dot(mixed.astype(BF16), wo_ref[...], preferred_element_type=F32)
    o_ref[0] = x + mod_ref[0, 5:6, :] * y


def _merge(x, mod_l, g, y_a, y_b, u, conv_w, conv_b, ln_g, ln_b, layer,
           w_gate, b_gate, w_branch, w_out):
    b, s, d = x.shape
    c = u.shape[-1]
    per_tile = MERGE_TM // CONV_HALO
    tok = pl.BlockSpec((1, MERGE_TM, d), lambda i, j: (i, j, 0))
    br = pl.BlockSpec((1, MERGE_TM, MIX_W), lambda i, j: (i, j, 0))
    return pl.pallas_call(
        _merge_kernel,
        grid=(b, s // MERGE_TM),
        in_specs=[
            tok,
            pl.BlockSpec((1, 9, d), lambda i, j: (i, 0, 0)),
            _resident((1, d)),
            br, br,
            pl.BlockSpec((1, CONV_HALO, c), lambda i, j: (i, jnp.maximum(j * per_tile - 1, 0), 0)),
            br,
            _resident((CONV_K, c)), _resident((1, c)), _resident((1, c)), _resident((1, c)),
            _resident(w_gate.shape[1:], (layer,)),
            _resident((1, N_BRANCH * d)),
            _resident(w_branch.shape[1:], (layer,)),
            _resident(w_out.shape[1:], (layer,)),
        ],
        out_specs=tok,
        out_shape=jax.ShapeDtypeStruct(x.shape, F32),
        scratch_shapes=[pltpu.VMEM((SUBLANES, CONV_HALO + MERGE_TM, c), F32),
                        pltpu.VMEM((MERGE_TM, c), F32)],
        compiler_params=_cparams(("arbitrary", "arbitrary")),
        name="merge",
    )(x, mod_l, g.reshape(1, d), y_a, y_b, u, u, conv_w, conv_b.reshape(1, c),
      ln_g.reshape(1, c), ln_b.reshape(1, c), w_gate, b_gate.reshape(1, N_BRANCH * d),
      w_branch, w_out)


def _qk_gain_rows(qk_gain):
    ga_q = jnp.tile(qk_gain[0], HA)
    ga_k = jnp.tile(qk_gain[1], HA)
    gb_q = jnp.tile(jnp.concatenate([qk_gain[2], qk_gain[3]]), HB)
    gb_k = jnp.tile(jnp.concatenate([qk_gain[4], qk_gain[5]]), HB)
    return jnp.stack([ga_q, ga_k, gb_q, gb_k])


def kernel(x, c, rel_bias, w_ada, b_ada, norm_g, w_ffn_in, w_ffn_out, w_in, qk_gain, lambda_vec,
           subln_g, conv_w, conv_b, conv_ln_g, conv_ln_b, w_branch, w_gate, b_gate, w_out):
    b, s, d = x.shape
    mod = _ada_mod(c, w_ada, b_ada)
    bias_a, bias_b = _bias_tiles(rel_bias, s)
    w_ffn_in, w_ffn_out, w_in = w_ffn_in.astype(BF16), w_ffn_out.astype(BF16), w_in.astype(BF16)
    w_gate, w_branch, w_out = w_gate.astype(BF16), w_branch.astype(BF16), w_out.astype(BF16)
    for l in range(DEPTH):
        mod_l = mod[l]
        x = _ffn(x, mod_l, norm_g[l, 0], w_ffn_in, w_ffn_out, (l, 0), 0)
        qa, ka, va, qbt, kb, vbt, u = _proj(x, mod_l, norm_g[l, 1], w_in, l,
                                          _qk_gain_rows(qk_gain[l]))
        y_a = _dilated_mixer(qa, ka, va, bias_a)
        y_b = _diff_attention(qbt, kb, vbt, bias_b, lambda_vec[l], subln_g[l], l)
        x = _merge(x, mod_l, norm_g[l, 1], y_a, y_b, u, conv_w[l], conv_b[l], conv_ln_g[l],
                   conv_ln_b[l], l, w_gate, b_gate[l], w_branch, w_out)
        x = _ffn(x, mod_l, norm_g[l, 2], w_ffn_in, w_ffn_out, (l, 1), 6)
    return x
```

```python
import functools
import math

import numpy as np
import jax
import jax.numpy as jnp
from jax import lax
from jax.experimental import pallas as pl
from jax.experimental.pallas import tpu as pltpu

D_MODEL = 1024
DEPTH = 2
HEAD_DIM = 64
HA = 8
DIL_PATTERNS = ((128, 1), (512, 4), (2048, 16))
HB = 4
C_CONV = 512
CONV_K = 31
D_FF = 2816
N_BUCKETS = 32
REL_MAX_DIST = 2048
BLK = 128
MIX_W = 512
N_BRANCH = 3
EPS = 1e-6
NEG = -1e30
LOG2E = math.log2(math.e)

LANES = 128
SUBLANES = 8
VMEM_LIMIT = 56 * 1024 * 1024

F32 = jnp.float32
BF16 = jnp.bfloat16


def _cparams(sem):
    return pltpu.CompilerParams(dimension_semantics=sem, vmem_limit_bytes=VMEM_LIMIT)


def _sigmoid(x):
    return 1.0 / (1.0 + jnp.exp(-x))


def _resident(shape, lead=()):
    lead = tuple(lead)
    zeros = (0,) * len(shape)
    return pl.BlockSpec((None,) * len(lead) + tuple(shape), lambda *_: lead + zeros,
                        pipeline_mode=pl.Buffered(1))


def _norm_mod(x, g, shift, scale):
    ms = jnp.mean(x * x, axis=-1, keepdims=True)
    y = x * lax.rsqrt(ms + EPS) * g
    return y * (1.0 + scale) + shift


ADA_TN = 2304


def _ada_kernel(c_ref, w_ref, b_ref, o_ref):
    c = c_ref[...]
    a = c * _sigmoid(c)
    o_ref[0] = jnp.dot(a, w_ref[0], preferred_element_type=F32,
                       precision=lax.Precision.HIGHEST) + b_ref[0]


def _ada_mod(c, w_ada, b_ada):
    b, d = c.shape
    rows = 8
    c_pad = jnp.pad(c, ((0, rows - b), (0, 0)))
    n = w_ada.shape[-1]
    out = pl.pallas_call(
        _ada_kernel,
        grid=(DEPTH, n // ADA_TN),
        in_specs=[
            pl.BlockSpec((rows, d), lambda l, j: (0, 0)),
            pl.BlockSpec((1, d, ADA_TN), lambda l, j: (l, 0, j)),
            pl.BlockSpec((1, 1, ADA_TN), lambda l, j: (l, 0, j)),
        ],
        out_specs=pl.BlockSpec((1, rows, ADA_TN), lambda l, j: (l, 0, j)),
        out_shape=jax.ShapeDtypeStruct((DEPTH, rows, n), F32),
        compiler_params=_cparams(("arbitrary", "arbitrary")),
        name="ada_mod",
    )(c_pad, w_ada, b_ada.reshape(DEPTH, 1, n))
    return out[:, :b].reshape(DEPTH, b, 9, d)


FFN_TM = 1024
FFN_TF = 256


def _ffn_kernel(x_ref, mod_ref, g_ref, wup_ref, wdn_ref, o_ref, act_ref, *, k0):
    x = x_ref[0]
    h = _norm_mod(x, g_ref[...], mod_ref[0, k0:k0 + 1, :], mod_ref[0, k0 + 1:k0 + 2, :])
    hb = h.astype(BF16)
    for j in range(D_FF // FFN_TF):
        lo = j * FFN_TF
        gate = jnp.dot(hb, wup_ref[:, lo:lo + FFN_TF], preferred_element_type=F32)
        up = jnp.dot(hb, wup_ref[:, D_FF + lo:D_FF + lo + FFN_TF], preferred_element_type=F32)
        act_ref[:, lo:lo + FFN_TF] = (gate * _sigmoid(gate) * up).astype(BF16)
    y = jnp.dot(act_ref[...], wdn_ref[...], preferred_element_type=F32)
    o_ref[0] = x + (0.5 * mod_ref[0, k0 + 2:k0 + 3, :]) * y


def _ffn(x, mod_l, g, w_up, w_dn, lead, k0):
    b, s, d = x.shape
    tok = pl.BlockSpec((1, FFN_TM, d), lambda i, j: (i, j, 0))
    return pl.pallas_call(
        functools.partial(_ffn_kernel, k0=k0),
        grid=(b, s // FFN_TM),
        in_specs=[
            tok,
            pl.BlockSpec((1, 9, d), lambda i, j: (i, 0, 0)),
            _resident((1, d)),
            _resident((d, 2 * D_FF), lead),
            _resident((D_FF, d), lead),
        ],
        out_specs=tok,
        out_shape=jax.ShapeDtypeStruct(x.shape, F32),
        scratch_shapes=[pltpu.VMEM((FFN_TM, D_FF), BF16)],
        compiler_params=_cparams(("arbitrary", "arbitrary")),
        name="ffn",
    )(x, mod_l, g.reshape(1, d), w_up, w_dn)


PROJ_TM = 512


def _head_rmsnorm(acc, gain):
    rows = acc.shape[0]
    low = lax.broadcasted_iota(jnp.int32, (rows, LANES), 1) < HEAD_DIM
    outs = []
    for c in range(acc.shape[1] // LANES):
        xc = acc[:, c * LANES:(c + 1) * LANES]
        sq = xc * xc
        s_lo = jnp.sum(jnp.where(low, sq, 0.0), axis=-1, keepdims=True)
        s_hi = jnp.sum(jnp.where(low, 0.0, sq), axis=-1, keepdims=True)
        r_lo = lax.rsqrt(s_lo * (1.0 / HEAD_DIM) + EPS)
        r_hi = lax.rsqrt(s_hi * (1.0 / HEAD_DIM) + EPS)
        outs.append(xc * jnp.where(low, r_lo, r_hi) * gain[:, c * LANES:(c + 1) * LANES])
    return jnp.concatenate(outs, axis=-1)


def _proj_kernel(x_ref, mod_ref, g_ref, w_ref, gain_ref,
                 qa_ref, ka_ref, va_ref, qbt_ref, kb_ref, vbt_ref, u_ref):
    x = x_ref[0]
    h = _norm_mod(x, g_ref[...], mod_ref[0, 3:4, :], mod_ref[0, 4:5, :])
    hb = h.astype(BF16)
    w = MIX_W

    def col(j):
        return jnp.dot(hb, w_ref[:, j * w:(j + 1) * w], preferred_element_type=F32)

    q_scale = 1.0 / math.sqrt(HEAD_DIM)
    qa_ref[0] = (_head_rmsnorm(col(0), gain_ref[0:1, :]) * (q_scale * LOG2E)).astype(BF16)
    ka_ref[0] = _head_rmsnorm(col(1), gain_ref[1:2, :]).astype(BF16)
    va_ref[0] = col(2).astype(BF16)
    qbt_ref[0] = (_head_rmsnorm(col(3), gain_ref[2:3, :]) * (q_scale * LOG2E)).T.astype(BF16)
    kb_ref[0] = _head_rmsnorm(col(4), gain_ref[3:4, :]).astype(BF16)
    vbt_ref[0] = col(5).T.astype(BF16)
    u_ref[0] = col(6) * _sigmoid(col(7))


def _proj(x, mod_l, g, w_in, layer, gains):
    b, s, d = x.shape
    tok_in = pl.BlockSpec((1, PROJ_TM, d), lambda i, j: (i, j, 0))
    tok_out = pl.BlockSpec((1, PROJ_TM, MIX_W), lambda i, j: (i, j, 0))
    bf = jax.ShapeDtypeStruct((b, s, MIX_W), BF16)
    bf_t = jax.ShapeDtypeStruct((b, MIX_W, s), BF16)
    tr_out = pl.BlockSpec((1, MIX_W, PROJ_TM), lambda i, j: (i, 0, j))
    return pl.pallas_call(
        _proj_kernel,
        grid=(b, s // PROJ_TM),
        in_specs=[
            tok_in,
            pl.BlockSpec((1, 9, d), lambda i, j: (i, 0, 0)),
            _resident((1, d)),
            _resident(w_in.shape[1:], (layer,)),
            _resident(gains.shape),
        ],
        out_specs=[tok_out] * 3 + [tr_out, tok_out, tr_out, tok_out],
        out_shape=[bf, bf, bf, bf_t, bf, bf_t, jax.ShapeDtypeStruct((b, s, MIX_W), F32)],
        compiler_params=_cparams(("arbitrary", "arbitrary")),
        name="proj",
    )(x, mod_l, g.reshape(1, d), w_in, gains)


def _bucket_thresholds():
    max_exact = N_BUCKETS // 2
    d = np.arange(0, 2 * REL_MAX_DIST + 2)
    df = np.maximum(d.astype(np.float32), np.float32(1.0))
    large = max_exact + (np.log(df / np.float32(max_exact))
                         / np.float32(math.log(REL_MAX_DIST / max_exact))
                         * np.float32(N_BUCKETS - max_exact)).astype(np.int32)
    bucket = np.where(d < max_exact, d, np.minimum(large, N_BUCKETS - 1))
    return [int(np.argmax(bucket >= b)) for b in range(1, N_BUCKETS)]


_THRESHOLDS = _bucket_thresholds()


def _bias_of_dist(dist, tab_ref, head):
    val = jnp.full(dist.shape, tab_ref[0, head], F32)
    for b in range(1, N_BUCKETS):
        val = jnp.where(dist >= _THRESHOLDS[b - 1], tab_ref[b, head], val)
    return val


DIFF_T = 256


def _diff_bias_tiles(seq):
    first_const = -(-(_THRESHOLDS[-1] + DIFF_T - 1) // DIFF_T)
    return min(seq // DIFF_T, first_const + 1)


def _bias_diff_kernel(tab_ref, o_ref):
    h = pl.program_id(0)
    delta = pl.program_id(1)
    j = lax.broadcasted_iota(jnp.int32, (DIFF_T, DIFF_T), 0)
    i = lax.broadcasted_iota(jnp.int32, (DIFF_T, DIFF_T), 1)
    dist = jnp.maximum(delta * DIFF_T + i - j, 0)
    o_ref[0, 0] = _bias_of_dist(dist, tab_ref, HA + h) * LOG2E


def _bias_dil_kernel(tab_ref, o_ref):
    p = pl.program_id(0)
    hp = pl.program_id(1)
    dil = jnp.where(p == 0, DIL_PATTERNS[0][1],
                    jnp.where(p == 1, DIL_PATTERNS[1][1], DIL_PATTERNS[2][1]))
    j = lax.broadcasted_iota(jnp.int32, (2 * BLK, BLK), 0)
    i = lax.broadcasted_iota(jnp.int32, (2 * BLK, BLK), 1)
    rel = i + BLK - j
    band = (rel >= 0) & (rel <= BLK)
    dist = jnp.maximum(rel, 0) * dil
    for sub in range(2):
        bias = _bias_of_dist(dist, tab_ref, 2 * hp + sub) * LOG2E
        lanes = slice(sub * BLK, (sub + 1) * BLK)
        o_ref[0, 0, 0, :, lanes] = jnp.where(band & (j >= BLK), bias, NEG)
        o_ref[0, 1, 0, :, lanes] = jnp.where(band, bias, NEG)


def _bias_tiles(rel_bias, seq):
    smem = pl.BlockSpec(memory_space=pltpu.SMEM)
    n_delta = _diff_bias_tiles(seq)
    bias_b = pl.pallas_call(
        _bias_diff_kernel,
        grid=(HB, n_delta),
        in_specs=[smem],
        out_specs=pl.BlockSpec((1, 1, DIFF_T, DIFF_T), lambda h, t: (h, t, 0, 0)),
        out_shape=jax.ShapeDtypeStruct((HB, n_delta, DIFF_T, DIFF_T), F32),
        compiler_params=_cparams(("arbitrary", "arbitrary")),
        name="bias_diff",
    )(rel_bias)
    n_pat = len(DIL_PATTERNS)
    bias_a = pl.pallas_call(
        _bias_dil_kernel,
        grid=(n_pat, HA // 2),
        in_specs=[smem],
        out_specs=pl.BlockSpec((1, 2, 1, 2 * BLK, 2 * BLK), lambda p, h: (p, 0, h, 0, 0)),
        out_shape=jax.ShapeDtypeStruct((n_pat, 2, HA // 2, 2 * BLK, 2 * BLK), F32),
        compiler_params=_cparams(("arbitrary", "arbitrary")),
        name="bias_dil",
    )(rel_bias)
    return bias_a, bias_b


DIL_CHAINS = 4
DIL_STRIDE = 4
DIL_ONES = 16
DIL_COMBINE_ROWS = 512


def _dil_kernel(q_ref, k_ref, v_ref, bias_ref, y_ref,
                q32_ref, k32_ref, v32_ref, q4_ref, k4_ref, v4_ref, qd_ref, kd_ref, vtd_ref,
                ot_ref, lt_ref, on_ref, ln_ref, sa_ref, sb_ref):
    seq = q_ref.shape[1]
    pad = BLK
    quarter = seq // DIL_STRIDE
    nat = (q32_ref, k32_ref, v32_ref)
    by4 = (q4_ref, k4_ref, v4_ref)
    for src_ref, nat_ref, by4_ref in zip((q_ref, k_ref, v_ref), nat, by4):
        nat_ref[...] = src_ref[0].astype(F32)
        for rho in range(DIL_STRIDE):
            by4_ref[rho * quarter:(rho + 1) * quarter, :] = nat_ref[pl.ds(rho, quarter, stride=DIL_STRIDE), :]

    def residue(which, dil, r):
        if dil == 1:
            return nat[which][...]
        if dil == DIL_STRIDE:
            return by4[which][r * quarter:(r + 1) * quarter, :]
        a, rho = divmod(r, DIL_STRIDE)
        return by4[which][pl.ds(rho * quarter + a, seq // dil, stride=DIL_STRIDE), :]

    kd_ref[0:pad, :] = jnp.zeros((pad, LANES), BF16)
    vtd_ref[:, 0:pad] = jnp.zeros((LANES, pad), BF16)
    low = lax.broadcasted_iota(jnp.int32, (BLK, LANES), 1) < HEAD_DIM
    top = lax.broadcasted_iota(jnp.int32, (LANES, BLK), 0) < HEAD_DIM
    ones = jnp.ones((DIL_ONES, 2 * BLK), BF16)

    for p, (_, dil) in enumerate(DIL_PATTERNS):
        sub_len = seq // dil
        nb = sub_len // BLK
        for r in range(dil):
            qd_ref[r * sub_len:(r + 1) * sub_len, :] = residue(0, dil, r).astype(BF16)
            kd_ref[pad + r * sub_len:pad + (r + 1) * sub_len, :] = residue(1, dil, r).astype(BF16)
            vtd_ref[:, pad + r * sub_len:pad + (r + 1) * sub_len] = residue(2, dil, r).astype(BF16).T

        def produce(j, s_ref, p=p, nb=nb):
            for c in range(DIL_CHAINS):
                chain = j * DIL_CHAINS + c
                base = pl.multiple_of(chain * BLK, BLK)
                var = jnp.where((chain & (nb - 1)) == 0, 0, 1)
                q = qd_ref[pl.ds(base, BLK), :]
                zero = jnp.zeros_like(q)
                qs = jnp.concatenate([jnp.where(low, q, zero), jnp.where(low, zero, q)], axis=0)
                kc = kd_ref[pl.ds(base, 2 * BLK), :]
                sc = lax.dot_general(kc, qs, (((1,), (1,)), ((), ())), preferred_element_type=F32)
                s_ref[c] = sc + bias_ref[p, var]

        def consume(j, s_ref):
            for c in range(DIL_CHAINS):
                base = pl.multiple_of((j * DIL_CHAINS + c) * BLK, BLK)
                sc = s_ref[c]
                m = jnp.max(sc, axis=0, keepdims=True)
                pexp = jnp.exp2(sc - m).astype(BF16)
                vt = jnp.concatenate([vtd_ref[:, pl.ds(base, 2 * BLK)], ones], axis=0)
                pv = jnp.dot(vt, pexp, preferred_element_type=F32)
                den = pv[LANES:LANES + 1, :]
                o = pv[0:LANES, :] / den
                lse = m + jnp.log2(den)
                chain = j * DIL_CHAINS + c
                ot_ref[chain] = jnp.where(top, o[:, :BLK], o[:, BLK:])
                lt_ref[chain] = jnp.where(top, lse[:, :BLK], lse[:, BLK:])

        n_trips = seq // BLK // DIL_CHAINS
        produce(0, sa_ref)

        def two_trips(i, carry, produce=produce, consume=consume):
            j = 2 * i
            produce(j + 1, sb_ref)
            consume(j, sa_ref)
            produce(j + 2, sa_ref)
            consume(j + 1, sb_ref)
            return carry

        lax.fori_loop(0, n_trips // 2 - 1, two_trips, 0)
        produce(n_trips - 1, sb_ref)
        consume(n_trips - 2, sa_ref)
        consume(n_trips - 1, sb_ref)
        for chain in range(seq // BLK):
            r, n = divmod(chain, nb)
            if dil <= DIL_STRIDE:
                rows = pl.ds(n * BLK * dil + r, BLK, stride=dil)
                on_ref[p, rows, :] = ot_ref[chain].T
                ln_ref[p, rows, :] = lt_ref[chain].T
            else:
                a, rho = divmod(r, DIL_STRIDE)
                rows = pl.ds(rho * quarter + n * BLK * DIL_STRIDE + a, BLK, stride=DIL_STRIDE)
                q32_ref[rows, :] = ot_ref[chain].T
                k32_ref[rows, :] = lt_ref[chain].T
        if dil > DIL_STRIDE:
            for rho in range(DIL_STRIDE):
                rows = pl.ds(rho, quarter, stride=DIL_STRIDE)
                on_ref[p, rows, :] = q32_ref[rho * quarter:(rho + 1) * quarter, :]
                ln_ref[p, rows, :] = k32_ref[rho * quarter:(rho + 1) * quarter, :]

    def combine(i, carry):
        rows = pl.ds(pl.multiple_of(i * DIL_COMBINE_ROWS, DIL_COMBINE_ROWS), DIL_COMBINE_ROWS)
        l1, l2, l3 = ln_ref[0, rows, :], ln_ref[1, rows, :], ln_ref[2, rows, :]
        mx = jnp.maximum(jnp.maximum(l1, l2), l3)
        e1 = jnp.exp2(l1 - mx)
        e2 = jnp.exp2(l2 - mx)
        e3 = jnp.exp2(l3 - mx)
        num = e1 * on_ref[0, rows, :] + e2 * on_ref[1, rows, :] + e3 * on_ref[2, rows, :]
        y_ref[0, rows, :] = (num / (e1 + e2 + e3)).astype(y_ref.dtype)
        return carry

    lax.fori_loop(0, seq // DIL_COMBINE_ROWS, combine, 0)


def _dilated_mixer(qa, ka, va, bias_a):
    b, s, w = qa.shape
    n_pat = len(DIL_PATTERNS)
    dils = [d for _, d in DIL_PATTERNS]
    assert all(d in (1, DIL_STRIDE, DIL_STRIDE ** 2) for d in dils) and dils[-1] == max(dils)
    pair = pl.BlockSpec((1, s, LANES), lambda i, h: (i, 0, h))
    return pl.pallas_call(
        _dil_kernel,
        grid=(b, HA // 2),
        in_specs=[pair, pair, pair,
                  pl.BlockSpec((n_pat, 2, None, 2 * BLK, 2 * BLK), lambda i, h: (0, 0, h, 0, 0))],
        out_specs=pair,
        out_shape=jax.ShapeDtypeStruct((b, s, w), BF16),
        scratch_shapes=[
            pltpu.VMEM((s, LANES), F32), pltpu.VMEM((s, LANES), F32), pltpu.VMEM((s, LANES), F32),
            pltpu.VMEM((s, LANES), F32), pltpu.VMEM((s, LANES), F32), pltpu.VMEM((s, LANES), F32),
            pltpu.VMEM((s, LANES), BF16), pltpu.VMEM((BLK + s, LANES), BF16),
            pltpu.VMEM((LANES, BLK + s), BF16),
            pltpu.VMEM((s // BLK, LANES, BLK), F32), pltpu.VMEM((s // BLK, LANES, BLK), F32),
            pltpu.VMEM((n_pat, s, LANES), F32), pltpu.VMEM((n_pat, s, LANES), F32),
            pltpu.VMEM((DIL_CHAINS, 2 * BLK, 2 * BLK), F32),
            pltpu.VMEM((DIL_CHAINS, 2 * BLK, 2 * BLK), F32),
        ],
        compiler_params=_cparams(("arbitrary", "arbitrary")),
        name="dilated",
    )(qa, ka, va, bias_a)


DIFF_TQ = 2 * DIFF_T
DIFF_HEADS = 2
DIFF_ONES = 16


def _diff_kernel(qt_ref, k_ref, vt_ref, bias_ref, lam_ref, sg_ref, o_ref,
                 qs_ref, m_ref, acc_ref, s0_ref, s1_ref, *, lam_init, n_bias):
    qi = pl.program_id(2)
    t = DIFF_T
    dv = 2 * HEAD_DIM
    n_blk = 2 * DIFF_TQ // t
    width = 2 * DIFF_TQ
    first = lax.broadcasted_iota(jnp.int32, (LANES, DIFF_TQ), 0) < HEAD_DIM
    for hd in range(DIFF_HEADS):
        q = qt_ref[0, hd * LANES:(hd + 1) * LANES, :]
        zero = jnp.zeros_like(q)
        qs_ref[:, hd * width:hd * width + DIFF_TQ] = jnp.where(first, q, zero)
        qs_ref[:, hd * width + DIFF_TQ:(hd + 1) * width] = jnp.where(first, zero, q)
    m_ref[...] = jnp.full(m_ref.shape, NEG, F32)
    acc_ref[...] = jnp.zeros(acc_ref.shape, F32)
    ones = jnp.ones((DIFF_ONES, t), BF16)

    def produce(kt, s_ref, blocks):
        off = pl.multiple_of(kt * t, t)
        for hd in range(DIFF_HEADS):
            k = k_ref[0, pl.ds(off, t), hd * LANES:(hd + 1) * LANES]
            for blk in blocks:
                lanes = slice(hd * width + blk * t, hd * width + (blk + 1) * t)
                s = jnp.dot(k, qs_ref[:, lanes], preferred_element_type=F32)
                delta = 2 * qi + blk % 2 - kt
                s_ref[hd * n_blk + blk] = s + bias_ref[hd, jnp.minimum(delta, n_bias - 1)]

    def consume(kt, s_ref, modes):
        off = pl.multiple_of(kt * t, t)
        for hd in range(DIFF_HEADS):
            vt = jnp.concatenate([vt_ref[0, hd * dv:(hd + 1) * dv, pl.ds(off, t)], ones], axis=0)
            for blk, mode in enumerate(modes):
                if mode == "skip":
                    continue
                g = hd * n_blk + blk
                lanes = slice(g * t, (g + 1) * t)
                s = s_ref[g]
                if mode == "diag":
                    key = lax.broadcasted_iota(jnp.int32, (t, t), 0)
                    qry = lax.broadcasted_iota(jnp.int32, (t, t), 1)
                    s = jnp.where(qry >= key, s, NEG)
                m_old = m_ref[:, lanes]
                m_new = jnp.maximum(m_old, jnp.max(s, axis=0, keepdims=True))
                alpha = jnp.exp2(m_old - m_new)
                p = jnp.exp2(s - m_new).astype(BF16)
                pv = jnp.dot(vt, p, preferred_element_type=F32)
                acc_ref[g] = alpha * acc_ref[g] + pv
                m_ref[:, lanes] = m_new

    every = tuple(range(n_blk))
    full = ("full",) * n_blk
    produce(0, s0_ref, every)

    def body(j, carry):
        kt = 2 * j
        produce(kt + 1, s1_ref, every)
        consume(kt, s0_ref, full)
        produce(kt + 2, s0_ref, every)
        consume(kt + 1, s1_ref, full)
        return carry

    def body_twice(j, carry):
        return body(2 * j + 1, body(2 * j, carry))

    pairs = lax.shift_right_logical(qi, 1)
    lax.fori_loop(0, pairs, body_twice, 0)
    lax.fori_loop(2 * pairs, qi, body, 0)
    produce(2 * qi + 1, s1_ref, (1, 3))
    consume(2 * qi, s0_ref, ("diag", "full", "diag", "full"))
    consume(2 * qi + 1, s1_ref, ("skip", "diag", "skip", "diag"))

    lv = lam_ref[...]
    s01 = jnp.sum(lv[0:1] * lv[1:2], axis=-1, keepdims=True)
    s23 = jnp.sum(lv[2:3] * lv[3:4], axis=-1, keepdims=True)
    lam = jnp.exp(s01) - jnp.exp(s23) + lam_init
    for hd in range(DIFF_HEADS):
        o_t = [acc_ref[hd * n_blk + blk, 0:dv, :] / acc_ref[hd * n_blk + blk, dv:dv + 1, :]
               for blk in range(n_blk)]
        half = n_blk // 2
        ob = jnp.concatenate([o_t[blk] - lam * o_t[half + blk] for blk in range(half)],
                             axis=1)
        ms = jnp.mean(ob * ob, axis=0, keepdims=True)
        y = ob * lax.rsqrt(ms + EPS) * sg_ref[...]
        o_ref[0, :, hd * LANES:(hd + 1) * LANES] = (y * (1.0 - lam_init)).T.astype(o_ref.dtype)


def _diff_attention(qbt, kb, vbt, bias_b, lambda_vec, subln_g, layer):
    b, s, w = kb.shape
    t = DIFF_T
    dv = 2 * HEAD_DIM
    n_bias = bias_b.shape[1]
    lam_init = 0.8 - 0.6 * math.exp(-0.3 * layer)
    heads = DIFF_HEADS
    n_lane_blocks = heads * 2 * DIFF_TQ // t
    tile = pl.BlockSpec((1, DIFF_TQ, heads * LANES), lambda i, h, n: (i, n, h))
    return pl.pallas_call(
        functools.partial(_diff_kernel, lam_init=lam_init, n_bias=n_bias),
        grid=(b, HB // heads, s // DIFF_TQ),
        in_specs=[
            pl.BlockSpec((1, heads * LANES, DIFF_TQ), lambda i, h, n: (i, h, n)),
            pl.BlockSpec((1, s, heads * LANES), lambda i, h, n: (i, 0, h)),
            pl.BlockSpec((1, heads * dv, s), lambda i, h, n: (i, h, 0)),
            pl.BlockSpec((heads, n_bias, t, t), lambda i, h, n: (h, 0, 0, 0)),
            pl.BlockSpec(lambda_vec.shape, lambda i, h, n: (0, 0)),
            pl.BlockSpec((dv, 1), lambda i, h, n: (0, 0)),
        ],
        out_specs=tile,
        out_shape=jax.ShapeDtypeStruct((b, s, w), BF16),
        scratch_shapes=[pltpu.VMEM((LANES, heads * 2 * DIFF_TQ), BF16),
                        pltpu.VMEM((1, heads * 2 * DIFF_TQ), F32),
                        pltpu.VMEM((n_lane_blocks, dv + DIFF_ONES, t), F32),
                        pltpu.VMEM((n_lane_blocks, t, t), F32),
                        pltpu.VMEM((n_lane_blocks, t, t), F32)],
        compiler_params=_cparams(("arbitrary", "arbitrary", "arbitrary")),
        name="diff_attn",
    )(qbt, kb, vbt, bias_b, lambda_vec, subln_g.reshape(dv, 1))


MERGE_TM = 512
CONV_HALO = 32
CONV_ROWS = 128


def _conv_tile(prev_ref, cur_ref, w_ref, cb_ref, lg_ref, lb_ref, sh_ref, y_ref, first_tile):
    halo = prev_ref[0]
    rows_in = cur_ref.shape[1]
    sh_ref[0, 0:CONV_HALO, :] = jnp.where(first_tile, jnp.zeros_like(halo), halo)
    sh_ref[0, CONV_HALO:, :] = cur_ref[0]
    for b in range(1, SUBLANES):
        sh_ref[b] = pltpu.roll(sh_ref[0], b, axis=0)
    for r0 in range(0, rows_in, CONV_ROWS):
        acc = jnp.zeros((CONV_ROWS, C_CONV), F32)
        for tap in range(CONV_K):
            back = CONV_K - 1 - tap
            a, b = divmod(back, SUBLANES)
            lo = r0 + CONV_HALO - a * SUBLANES
            acc = acc + sh_ref[b, lo:lo + CONV_ROWS, :] * w_ref[tap:tap + 1, :]
        y_ref[r0:r0 + CONV_ROWS, :] = acc
    y = y_ref[...] + cb_ref[...]
    mu = jnp.mean(y, axis=-1, keepdims=True)
    var = jnp.mean(jnp.square(y - mu), axis=-1, keepdims=True)
    z = (y - mu) * lax.rsqrt(var + EPS) * lg_ref[...] + lb_ref[...]
    return z * _sigmoid(z)


def _merge_kernel(x_ref, mod_ref, g_ref, ya_ref, yb_ref, uprev_ref, u_ref,
                  cw_ref, cb_ref, lg_ref, lb_ref, wg_ref, bg_ref, wb_ref, wo_ref, o_ref,
                  sh_ref, yc_ref):
    x = x_ref[0]
    d = x.shape[-1]
    h = _norm_mod(x, g_ref[...], mod_ref[0, 3:4, :], mod_ref[0, 4:5, :])
    hb = h.astype(BF16)
    y_c = _conv_tile(uprev_ref, u_ref, cw_ref, cb_ref, lg_ref, lb_ref, sh_ref, yc_ref,
                     pl.program_id(1) == 0).astype(BF16)
    mixed = jnp.zeros(x.shape, F32)
    for i, y in enumerate((ya_ref[0], yb_ref[0], y_c)):
        z = jnp.dot(hb, wg_ref[:, i * d:(i + 1) * d], preferred_element_type=F32)
        gate = _sigmoid(z + bg_ref[:, i * d:(i + 1) * d])
        mixed = mixed + gate * jnp.dot(y, wb_ref[i], preferred_element_type=F32)
    y = jnp.dot(mixed.astype(BF16), wo_ref[...], preferred_element_type=F32)
    o_ref[0] = x + mod_ref[0, 5:6, :] * y


def _merge(x, mod_l, g, y_a, y_b, u, conv_w, conv_b, ln_g, ln_b, layer,
           w_gate, b_gate, w_branch, w_out):
    b, s, d = x.shape
    c = u.shape[-1]
    per_tile = MERGE_TM // CONV_HALO
    tok = pl.BlockSpec((1, MERGE_TM, d), lambda i, j: (i, j, 0))
    br = pl.BlockSpec((1, MERGE_TM, MIX_W), lambda i, j: (i, j, 0))
    return pl.pallas_call(
        _merge_kernel,
        grid=(b, s // MERGE_TM),
        in_specs=[
            tok,
            pl.BlockSpec((1, 9, d), lambda i, j: (i, 0, 0)),
            _resident((1, d)),
            br, br,
            pl.BlockSpec((1, CONV_HALO, c), lambda i, j: (i, jnp.maximum(j * per_tile - 1, 0), 0)),
            br,
            _resident((CONV_K, c)), _resident((1, c)), _resident((1, c)), _resident((1, c)),
            _resident(w_gate.shape[1:], (layer,)),
            _resident((1, N_BRANCH * d)),
            _resident(w_branch.shape[1:], (layer,)),
            _resident(w_out.shape[1:], (layer,)),
        ],
        out_specs=tok,
        out_shape=jax.ShapeDtypeStruct(x.shape, F32),
        scratch_shapes=[pltpu.VMEM((SUBLANES, CONV_HALO + MERGE_TM, c), F32),
                        pltpu.VMEM((MERGE_TM, c), F32)],
        compiler_params=_cparams(("arbitrary", "arbitrary")),
        name="merge",
    )(x, mod_l, g.reshape(1, d), y_a, y_b, u, u, conv_w, conv_b.reshape(1, c),
      ln_g.reshape(1, c), ln_b.reshape(1, c), w_gate, b_gate.reshape(1, N_BRANCH * d),
      w_branch, w_out)


def _qk_gain_rows(qk_gain):
    ga_q = jnp.tile(qk_gain[0], HA)
    ga_k = jnp.tile(qk_gain[1], HA)
    gb_q = jnp.tile(jnp.concatenate([qk_gain[2], qk_gain[3]]), HB)
    gb_k = jnp.tile(jnp.concatenate([qk_gain[4], qk_gain[5]]), HB)
    return jnp.stack([ga_q, ga_k, gb_q, gb_k])


def kernel(x, c, rel_bias, w_ada, b_ada, norm_g, w_ffn_in, w_ffn_out, w_in, qk_gain, lambda_vec,
           subln_g, conv_w, conv_b, conv_ln_g, conv_ln_b, w_branch, w_gate, b_gate, w_out):
    b, s, d = x.shape
    mod = _ada_mod(c, w_ada, b_ada)
    bias_a, bias_b = _bias_tiles(rel_bias, s)
    w_ffn_in, w_ffn_out, w_in = w_ffn_in.astype(BF16), w_ffn_out.astype(BF16), w_in.astype(BF16)
    w_gate, w_branch, w_out = w_gate.astype(BF16), w_branch.astype(BF16), w_out.astype(BF16)
    for l in range(DEPTH):
        mod_l = mod[l]
        x = _ffn(x, mod_l, norm_g[l, 0], w_ffn_in, w_ffn_out, (l, 0), 0)
        qa, ka, va, qbt, kb, vbt, u = _proj(x, mod_l, norm_g[l, 1], w_in, l,
                                          _qk_gain_rows(qk_gain[l]))
        y_a = _dilated_mixer(qa, ka, va, bias_a)
        y_b = _diff_attention(qbt, kb, vbt, bias_b, lambda_vec[l], subln_g[l], l)
        x = _merge(x, mod_l, norm_g[l, 1], y_a, y_b, u, conv_w[l], conv_b[l], conv_ln_g[l],
                   conv_ln_b[l], l, w_gate, b_gate[l], w_branch, w_out)
        x = _ffn(x, mod_l, norm_g[l, 2], w_ffn_in, w_ffn_out, (l, 1), 6)
    return x
```

```python
import functools
import math

import numpy as np
import jax
import jax.numpy as jnp
from jax import lax
from jax.experimental import pallas as pl
from jax.experimental.pallas import tpu as pltpu

D_MODEL = 1024
DEPTH = 2
HEAD_DIM = 64
HA = 8
DIL_PATTERNS = ((128, 1), (512, 4), (2048, 16))
HB = 4
C_CONV = 512
CONV_K = 31
D_FF = 2816
N_BUCKETS = 32
REL_MAX_DIST = 2048
BLK = 128
MIX_W = 512
N_BRANCH = 3
EPS = 1e-6
NEG = -1e30
LOG2E = math.log2(math.e)

LANES = 128
SUBLANES = 8
VMEM_LIMIT = 56 * 1024 * 1024

F32 = jnp.float32
BF16 = jnp.bfloat16


def _cparams(sem):
    return pltpu.CompilerParams(dimension_semantics=sem, vmem_limit_bytes=VMEM_LIMIT)


def _sigmoid(x):
    return 1.0 / (1.0 + jnp.exp(-x))


def _resident(shape, lead=()):
    lead = tuple(lead)
    zeros = (0,) * len(shape)
    return pl.BlockSpec((None,) * len(lead) + tuple(shape), lambda *_: lead + zeros,
                        pipeline_mode=pl.Buffered(1))


def _norm_mod(x, g, shift, scale):
    ms = jnp.mean(x * x, axis=-1, keepdims=True)
    y = x * lax.rsqrt(ms + EPS) * g
    return y * (1.0 + scale) + shift


ADA_TN = 2304


def _ada_kernel(c_ref, w_ref, b_ref, o_ref):
    c = c_ref[...]
    a = c * _sigmoid(c)
    o_ref[0] = jnp.dot(a, w_ref[0], preferred_element_type=F32,
                       precision=lax.Precision.HIGHEST) + b_ref[0]


def _ada_mod(c, w_ada, b_ada):
    b, d = c.shape
    rows = 8
    c_pad = jnp.pad(c, ((0, rows - b), (0, 0)))
    n = w_ada.shape[-1]
    out = pl.pallas_call(
        _ada_kernel,
        grid=(DEPTH, n // ADA_TN),
        in_specs=[
            pl.BlockSpec((rows, d), lambda l, j: (0, 0)),
            pl.BlockSpec((1, d, ADA_TN), lambda l, j: (l, 0, j)),
            pl.BlockSpec((1, 1, ADA_TN), lambda l, j: (l, 0, j)),
        ],
        out_specs=pl.BlockSpec((1, rows, ADA_TN), lambda l, j: (l, 0, j)),
        out_shape=jax.ShapeDtypeStruct((DEPTH, rows, n), F32),
        compiler_params=_cparams(("arbitrary", "arbitrary")),
        name="ada_mod",
    )(c_pad, w_ada, b_ada.reshape(DEPTH, 1, n))
    return out[:, :b].reshape(DEPTH, b, 9, d)


FFN_TM = 1024
FFN_TF = 256


def _ffn_kernel(x_ref, mod_ref, g_ref, wup_ref, wdn_ref, o_ref, act_ref, *, k0):
    x = x_ref[0]
    h = _norm_mod(x, g_ref[...], mod_ref[0, k0:k0 + 1, :], mod_ref[0, k0 + 1:k0 + 2, :])
    hb = h.astype(BF16)
    for j in range(D_FF // FFN_TF):
        lo = j * FFN_TF
        gate = jnp.dot(hb, wup_ref[:, lo:lo + FFN_TF], preferred_element_type=F32)
        up = jnp.dot(hb, wup_ref[:, D_FF + lo:D_FF + lo + FFN_TF], preferred_element_type=F32)
        act_ref[:, lo:lo + FFN_TF] = (gate * _sigmoid(gate) * up).astype(BF16)
    y = jnp.dot(act_ref[...], wdn_ref[...], preferred_element_type=F32)
    o_ref[0] = x + (0.5 * mod_ref[0, k0 + 2:k0 + 3, :]) * y


def _ffn(x, mod_l, g, w_up, w_dn, lead, k0):
    b, s, d = x.shape
    tok = pl.BlockSpec((1, FFN_TM, d), lambda i, j: (i, j, 0))
    return pl.pallas_call(
        functools.partial(_ffn_kernel, k0=k0),
        grid=(b, s // FFN_TM),
        in_specs=[
            tok,
            pl.BlockSpec((1, 9, d), lambda i, j: (i, 0, 0)),
            _resident((1, d)),
            _resident((d, 2 * D_FF), lead),
            _resident((D_FF, d), lead),
        ],
        out_specs=tok,
        out_shape=jax.ShapeDtypeStruct(x.shape, F32),
        scratch_shapes=[pltpu.VMEM((FFN_TM, D_FF), BF16)],
        compiler_params=_cparams(("arbitrary", "arbitrary")),
        name="ffn",
    )(x, mod_l, g.reshape(1, d), w_up, w_dn)


PROJ_TM = 512


def _head_rmsnorm(acc, gain):
    rows = acc.shape[0]
    low = lax.broadcasted_iota(jnp.int32, (rows, LANES), 1) < HEAD_DIM
    outs = []
    for c in range(acc.shape[1] // LANES):
        xc = acc[:, c * LANES:(c + 1) * LANES]
        sq = xc * xc
        s_lo = jnp.sum(jnp.where(low, sq, 0.0), axis=-1, keepdims=True)
        s_hi = jnp.sum(jnp.where(low, 0.0, sq), axis=-1, keepdims=True)
        r_lo = lax.rsqrt(s_lo * (1.0 / HEAD_DIM) + EPS)
        r_hi = lax.rsqrt(s_hi * (1.0 / HEAD_DIM) + EPS)
        outs.append(xc * jnp.where(low, r_lo, r_hi) * gain[:, c * LANES:(c + 1) * LANES])
    return jnp.concatenate(outs, axis=-1)


def _proj_kernel(x_ref, mod_ref, g_ref, w_ref, gain_ref,
                 qa_ref, ka_ref, va_ref, qbt_ref, kb_ref, vbt_ref, u_ref):
    x = x_ref[0]
    h = _norm_mod(x, g_ref[...], mod_ref[0, 3:4, :], mod_ref[0, 4:5, :])
    hb = h.astype(BF16)
    w = MIX_W

    def col(j):
        return jnp.dot(hb, w_ref[:, j * w:(j + 1) * w], preferred_element_type=F32)

    q_scale = 1.0 / math.sqrt(HEAD_DIM)
    qa_ref[0] = (_head_rmsnorm(col(0), gain_ref[0:1, :]) * (q_scale * LOG2E)).astype(BF16)
    ka_ref[0] = _head_rmsnorm(col(1), gain_ref[1:2, :]).astype(BF16)
    va_ref[0] = col(2).astype(BF16)
    qbt_ref[0] = (_head_rmsnorm(col(3), gain_ref[2:3, :]) * (q_scale * LOG2E)).T.astype(BF16)
    kb_ref[0] = _head_rmsnorm(col(4), gain_ref[3:4, :]).astype(BF16)
    vbt_ref[0] = col(5).T.astype(BF16)
    u_ref[0] = col(6) * _sigmoid(col(7))


def _proj(x, mod_l, g, w_in, layer, gains):
    b, s, d = x.shape
    tok_in = pl.BlockSpec((1, PROJ_TM, d), lambda i, j: (i, j, 0))
    tok_out = pl.BlockSpec((1, PROJ_TM, MIX_W), lambda i, j: (i, j, 0))
    bf = jax.ShapeDtypeStruct((b, s, MIX_W), BF16)
    bf_t = jax.ShapeDtypeStruct((b, MIX_W, s), BF16)
    tr_out = pl.BlockSpec((1, MIX_W, PROJ_TM), lambda i, j: (i, 0, j))
    return pl.pallas_call(
        _proj_kernel,
        grid=(b, s // PROJ_TM),
        in_specs=[
            tok_in,
            pl.BlockSpec((1, 9, d), lambda i, j: (i, 0, 0)),
            _resident((1, d)),
            _resident(w_in.shape[1:], (layer,)),
            _resident(gains.shape),
        ],
        out_specs=[tok_out] * 3 + [tr_out, tok_out, tr_out, tok_out],
        out_shape=[bf, bf, bf, bf_t, bf, bf_t, jax.ShapeDtypeStruct((b, s, MIX_W), F32)],
        compiler_params=_cparams(("arbitrary", "arbitrary")),
        name="proj",
    )(x, mod_l, g.reshape(1, d), w_in, gains)


def _bucket_thresholds():
    max_exact = N_BUCKETS // 2
    d = np.arange(0, 2 * REL_MAX_DIST + 2)
    df = np.maximum(d.astype(np.float32), np.float32(1.0))
    large = max_exact + (np.log(df / np.float32(max_exact))
                         / np.float32(math.log(REL_MAX_DIST / max_exact))
                         * np.float32(N_BUCKETS - max_exact)).astype(np.int32)
    bucket = np.where(d < max_exact, d, np.minimum(large, N_BUCKETS - 1))
    return [int(np.argmax(bucket >= b)) for b in range(1, N_BUCKETS)]


_THRESHOLDS = _bucket_thresholds()


def _bias_of_dist(dist, tab_ref, head):
    val = jnp.full(dist.shape, tab_ref[0, head], F32)
    for b in range(1, N_BUCKETS):
        val = jnp.where(dist >= _THRESHOLDS[b - 1], tab_ref[b, head], val)
    return val


DIFF_T = 256


def _diff_bias_tiles(seq):
    first_const = -(-(_THRESHOLDS[-1] + DIFF_T - 1) // DIFF_T)
    return min(seq // DIFF_T, first_const + 1)


def _bias_diff_kernel(tab_ref, o_ref):
    h = pl.program_id(0)
    delta = pl.program_id(1)
    j = lax.broadcasted_iota(jnp.int32, (DIFF_T, DIFF_T), 0)
    i = lax.broadcasted_iota(jnp.int32, (DIFF_T, DIFF_T), 1)
    dist = jnp.maximum(delta * DIFF_T + i - j, 0)
    o_ref[0, 0] = _bias_of_dist(dist, tab_ref, HA + h) * LOG2E


def _bias_dil_kernel(tab_ref, o_ref):
    p = pl.program_id(0)
    hp = pl.program_id(1)
    dil = jnp.where(p == 0, DIL_PATTERNS[0][1],
                    jnp.where(p == 1, DIL_PATTERNS[1][1], DIL_PATTERNS[2][1]))
    j = lax.broadcasted_iota(jnp.int32, (2 * BLK, BLK), 0)
    i = lax.broadcasted_iota(jnp.int32, (2 * BLK, BLK), 1)
    rel = i + BLK - j
    band = (rel >= 0) & (rel <= BLK)
    dist = jnp.maximum(rel, 0) * dil
    for sub in range(2):
        bias = _bias_of_dist(dist, tab_ref, 2 * hp + sub) * LOG2E
        lanes = slice(sub * BLK, (sub + 1) * BLK)
        o_ref[0, 0, 0, :, lanes] = jnp.where(band & (j >= BLK), bias, NEG)
        o_ref[0, 1, 0, :, lanes] = jnp.where(band, bias, NEG)


def _bias_tiles(rel_bias, seq):
    smem = pl.BlockSpec(memory_space=pltpu.SMEM)
    n_delta = _diff_bias_tiles(seq)
    bias_b = pl.pallas_call(
        _bias_diff_kernel,
        grid=(HB, n_delta),
        in_specs=[smem],
        out_specs=pl.BlockSpec((1, 1, DIFF_T, DIFF_T), lambda h, t: (h, t, 0, 0)),
        out_shape=jax.ShapeDtypeStruct((HB, n_delta, DIFF_T, DIFF_T), F32),
        compiler_params=_cparams(("arbitrary", "arbitrary")),
        name="bias_diff",
    )(rel_bias)
    n_pat = len(DIL_PATTERNS)
    bias_a = pl.pallas_call(
        _bias_dil_kernel,
        grid=(n_pat, HA // 2),
        in_specs=[smem],
        out_specs=pl.BlockSpec((1, 2, 1, 2 * BLK, 2 * BLK), lambda p, h: (p, 0, h, 0, 0)),
        out_shape=jax.ShapeDtypeStruct((n_pat, 2, HA // 2, 2 * BLK, 2 * BLK), F32),
        compiler_params=_cparams(("arbitrary", "arbitrary")),
        name="bias_dil",
    )(rel_bias)
    return bias_a, bias_b


DIL_CHAINS = 4
DIL_STRIDE = 4
DIL_ONES = 16
DIL_COMBINE_ROWS = 512


def _dil_kernel(q_ref, k_ref, v_ref, bias_ref, y_ref,
                q32_ref, k32_ref, v32_ref, q4_ref, k4_ref, v4_ref, qd_ref, kd_ref, vtd_ref,
                ot_ref, lt_ref, on_ref, ln_ref, sa_ref, sb_ref):
    seq = q_ref.shape[1]
    pad = BLK
    quarter = seq // DIL_STRIDE
    nat = (q32_ref, k32_ref, v32_ref)
    by4 = (q4_ref, k4_ref, v4_ref)
    for src_ref, nat_ref, by4_ref in zip((q_ref, k_ref, v_ref), nat, by4):
        nat_ref[...] = src_ref[0].astype(F32)
        for rho in range(DIL_STRIDE):
            by4_ref[rho * quarter:(rho + 1) * quarter, :] = nat_ref[pl.ds(rho, quarter, stride=DIL_STRIDE), :]

    def residue(which, dil, r):
        if dil == 1:
            return nat[which][...]
        if dil == DIL_STRIDE:
            return by4[which][r * quarter:(r + 1) * quarter, :]
        a, rho = divmod(r, DIL_STRIDE)
        return by4[which][pl.ds(rho * quarter + a, seq // dil, stride=DIL_STRIDE), :]

    kd_ref[0:pad, :] = jnp.zeros((pad, LANES), BF16)
    vtd_ref[:, 0:pad] = jnp.zeros((LANES, pad), BF16)
    low = lax.broadcasted_iota(jnp.int32, (BLK, LANES), 1) < HEAD_DIM
    top = lax.broadcasted_iota(jnp.int32, (LANES, BLK), 0) < HEAD_DIM
    ones = jnp.ones((DIL_ONES, 2 * BLK), BF16)

    for p, (_, dil) in enumerate(DIL_PATTERNS):
        sub_len = seq // dil
        nb = sub_len // BLK
        for r in range(dil):
            qd_ref[r * sub_len:(r + 1) * sub_len, :] = residue(0, dil, r).astype(BF16)
            kd_ref[pad + r * sub_len:pad + (r + 1) * sub_len, :] = residue(1, dil, r).astype(BF16)
            vtd_ref[:, pad + r * sub_len:pad + (r + 1) * sub_len] = residue(2, dil, r).astype(BF16).T

        def produce(j, s_ref, p=p, nb=nb):
            for c in range(DIL_CHAINS):
                chain = j * DIL_CHAINS + c
                base = pl.multiple_of(chain * BLK, BLK)
                var = jnp.where((chain & (nb - 1)) == 0, 0, 1)
                q = qd_ref[pl.ds(base, BLK), :]
                zero = jnp.zeros_like(q)
                qs = jnp.concatenate([jnp.where(low, q, zero), jnp.where(low, zero, q)], axis=0)
                kc = kd_ref[pl.ds(base, 2 * BLK), :]
                sc = lax.dot_general(kc, qs, (((1,), (1,)), ((), ())), preferred_element_type=F32)
                s_ref[c] = sc + bias_ref[p, var]

        def consume(j, s_ref):
            for c in range(DIL_CHAINS):
                base = pl.multiple_of((j * DIL_CHAINS + c) * BLK, BLK)
                sc = s_ref[c]
                m = jnp.max(sc, axis=0, keepdims=True)
                pexp = jnp.exp2(sc - m).astype(BF16)
                vt = jnp.concatenate([vtd_ref[:, pl.ds(base, 2 * BLK)], ones], axis=0)
                pv = jnp.dot(vt, pexp, preferred_element_type=F32)
                den = pv[LANES:LANES + 1, :]
                o = pv[0:LANES, :] / den
                lse = m + jnp.log2(den)
                chain = j * DIL_CHAINS + c
                ot_ref[chain] = jnp.where(top, o[:, :BLK], o[:, BLK:])
                lt_ref[chain] = jnp.where(top, lse[:, :BLK], lse[:, BLK:])

        n_trips = seq // BLK // DIL_CHAINS
        produce(0, sa_ref)

        def two_trips(i, carry, produce=produce, consume=consume):
            j = 2 * i
            produce(j + 1, sb_ref)
            consume(j, sa_ref)
            produce(j + 2, sa_ref)
            consume(j + 1, sb_ref)
            return carry

        lax.fori_loop(0, n_trips // 2 - 1, two_trips, 0)
        produce(n_trips - 1, sb_ref)
        consume(n_trips - 2, sa_ref)
        consume(n_trips - 1, sb_ref)
        for chain in range(seq // BLK):
            r, n = divmod(chain, nb)
            if dil <= DIL_STRIDE:
                rows = pl.ds(n * BLK * dil + r, BLK, stride=dil)
                on_ref[p, rows, :] = ot_ref[chain].T
                ln_ref[p, rows, :] = lt_ref[chain].T
            else:
                a, rho = divmod(r, DIL_STRIDE)
                rows = pl.ds(rho * quarter + n * BLK * DIL_STRIDE + a, BLK, stride=DIL_STRIDE)
                q32_ref[rows, :] = ot_ref[chain].T
                k32_ref[rows, :] = lt_ref[chain].T
        if dil > DIL_STRIDE:
            for rho in range(DIL_STRIDE):
                rows = pl.ds(rho, quarter, stride=DIL_STRIDE)
                on_ref[p, rows, :] = q32_ref[rho * quarter:(rho + 1) * quarter, :]
                ln_ref[p, rows, :] = k32_ref[rho * quarter:(rho + 1) * quarter, :]

    def combine(i, carry):
        rows = pl.ds(pl.multiple_of(i * DIL_COMBINE_ROWS, DIL_COMBINE_ROWS), DIL_COMBINE_ROWS)
        l1, l2, l3 = ln_ref[0, rows, :], ln_ref[1, rows, :], ln_ref[2, rows, :]
        mx = jnp.maximum(jnp.maximum(l1, l2), l3)
        e1 = jnp.exp2(l1 - mx)
        e2 = jnp.exp2(l2 - mx)
        e3 = jnp.exp2(l3 - mx)
        num = e1 * on_ref[0, rows, :] + e2 * on_ref[1, rows, :] + e3 * on_ref[2, rows, :]
        y_ref[0, rows, :] = (num / (e1 + e2 + e3)).astype(y_ref.dtype)
        return carry

    lax.fori_loop(0, seq // DIL_COMBINE_ROWS, combine, 0)


def _dilated_mixer(qa, ka, va, bias_a):
    b, s, w = qa.shape
    n_pat = len(DIL_PATTERNS)
    dils = [d for _, d in DIL_PATTERNS]
    assert all(d in (1, DIL_STRIDE, DIL_STRIDE ** 2) for d in dils) and dils[-1] == max(dils)
    pair = pl.BlockSpec((1, s, LANES), lambda i, h: (i, 0, h))
    return pl.pallas_call(
        _dil_kernel,
        grid=(b, HA // 2),
        in_specs=[pair, pair, pair,
                  pl.BlockSpec((n_pat, 2, None, 2 * BLK, 2 * BLK), lambda i, h: (0, 0, h, 0, 0))],
        out_specs=pair,
        out_shape=jax.ShapeDtypeStruct((b, s, w), BF16),
        scratch_shapes=[
            pltpu.VMEM((s, LANES), F32), pltpu.VMEM((s, LANES), F32), pltpu.VMEM((s, LANES), F32),
            pltpu.VMEM((s, LANES), F32), pltpu.VMEM((s, LANES), F32), pltpu.VMEM((s, LANES), F32),
            pltpu.VMEM((s, LANES), BF16), pltpu.VMEM((BLK + s, LANES), BF16),
            pltpu.VMEM((LANES, BLK + s), BF16),
            pltpu.VMEM((s // BLK, LANES, BLK), F32), pltpu.VMEM((s // BLK, LANES, BLK), F32),
            pltpu.VMEM((n_pat, s, LANES), F32), pltpu.VMEM((n_pat, s, LANES), F32),
            pltpu.VMEM((DIL_CHAINS, 2 * BLK, 2 * BLK), F32),
            pltpu.VMEM((DIL_CHAINS, 2 * BLK, 2 * BLK), F32),
        ],
        compiler_params=_cparams(("arbitrary", "arbitrary")),
        name="dilated",
    )(qa, ka, va, bias_a)


DIFF_TQ = 2 * DIFF_T
DIFF_HEADS = 2
DIFF_ONES = 16


def _diff_kernel(qt_ref, k_ref, vt_ref, bias_ref, lam_ref, sg_ref, o_ref,
                 qs_ref, m_ref, acc_ref, s0_ref, s1_ref, *, lam_init, n_bias):
    qi = pl.program_id(2)
    t = DIFF_T
    dv = 2 * HEAD_DIM
    n_blk = 2 * DIFF_TQ // t
    width = 2 * DIFF_TQ
    first = lax.broadcasted_iota(jnp.int32, (LANES, DIFF_TQ), 0) < HEAD_DIM
    for hd in range(DIFF_HEADS):
        q = qt_ref[0, hd * LANES:(hd + 1) * LANES, :]
        zero = jnp.zeros_like(q)
        qs_ref[:, hd * width:hd * width + DIFF_TQ] = jnp.where(first, q, zero)
        qs_ref[:, hd * width + DIFF_TQ:(hd + 1) * width] = jnp.where(first, zero, q)
    m_ref[...] = jnp.full(m_ref.shape, NEG, F32)
    acc_ref[...] = jnp.zeros(acc_ref.shape, F32)
    ones = jnp.ones((DIFF_ONES, t), BF16)

    def produce(kt, s_ref, blocks):
        off = pl.multiple_of(kt * t, t)
        for hd in range(DIFF_HEADS):
            k = k_ref[0, pl.ds(off, t), hd * LANES:(hd + 1) * LANES]
            for blk in blocks:
                lanes = slice(hd * width + blk * t, hd * width + (blk + 1) * t)
                s = jnp.dot(k, qs_ref[:, lanes], preferred_element_type=F32)
                delta = 2 * qi + blk % 2 - kt
                s_ref[hd * n_blk + blk] = s + bias_ref[hd, jnp.minimum(delta, n_bias - 1)]

    def consume(kt, s_ref, modes):
        off = pl.multiple_of(kt * t, t)
        for hd in range(DIFF_HEADS):
            vt = jnp.concatenate([vt_ref[0, hd * dv:(hd + 1) * dv, pl.ds(off, t)], ones], axis=0)
            for blk, mode in enumerate(modes):
                if mode == "skip":
                    continue
                g = hd * n_blk + blk
                lanes = slice(g * t, (g + 1) * t)
                s = s_ref[g]
                if mode == "diag":
                    key = lax.broadcasted_iota(jnp.int32, (t, t), 0)
                    qry = lax.broadcasted_iota(jnp.int32, (t, t), 1)
                    s = jnp.where(qry >= key, s, NEG)
                m_old = m_ref[:, lanes]
                m_new = jnp.maximum(m_old, jnp.max(s, axis=0, keepdims=True))
                alpha = jnp.exp2(m_old - m_new)
                p = jnp.exp2(s - m_new).astype(BF16)
                pv = jnp.dot(vt, p, preferred_element_type=F32)
                acc_ref[g] = alpha * acc_ref[g] + pv
                m_ref[:, lanes] = m_new

    every = tuple(range(n_blk))
    full = ("full",) * n_blk
    produce(0, s0_ref, every)

    def body(j, carry):
        kt = 2 * j
        produce(kt + 1, s1_ref, every)
        consume(kt, s0_ref, full)
        produce(kt + 2, s0_ref, every)
        consume(kt + 1, s1_ref, full)
        return carry

    def body_x2(j, carry):
        return body(2 * j + 1, body(2 * j, carry))

    def body_x4(j, carry):
        return body_x2(2 * j + 1, body_x2(2 * j, carry))

    quads = lax.shift_right_logical(qi, 2)
    lax.fori_loop(0, quads, body_x4, 0)
    pairs = lax.shift_right_logical(qi, 1)
    lax.fori_loop(2 * quads, pairs, body_x2, 0)
    lax.fori_loop(2 * pairs, qi, body, 0)
    produce(2 * qi + 1, s1_ref, (1, 3))
    consume(2 * qi, s0_ref, ("diag", "full", "diag", "full"))
    consume(2 * qi + 1, s1_ref, ("skip", "diag", "skip", "diag"))

    lv = lam_ref[...]
    s01 = jnp.sum(lv[0:1] * lv[1:2], axis=-1, keepdims=True)
    s23 = jnp.sum(lv[2:3] * lv[3:4], axis=-1, keepdims=True)
    lam = jnp.exp(s01) - jnp.exp(s23) + lam_init
    for hd in range(DIFF_HEADS):
        o_t = [acc_ref[hd * n_blk + blk, 0:dv, :] / acc_ref[hd * n_blk + blk, dv:dv + 1, :]
               for blk in range(n_blk)]
        half = n_blk // 2
        ob = jnp.concatenate([o_t[blk] - lam * o_t[half + blk] for blk in range(half)],
                             axis=1)
        ms = jnp.mean(ob * ob, axis=0, keepdims=True)
        y = ob * lax.rsqrt(ms + EPS) * sg_ref[...]
        o_ref[0, :, hd * LANES:(hd + 1) * LANES] = (y * (1.0 - lam_init)).T.astype(o_ref.dtype)


def _diff_attention(qbt, kb, vbt, bias_b, lambda_vec, subln_g, layer):
    b, s, w = kb.shape
    t = DIFF_T
    dv = 2 * HEAD_DIM
    n_bias = bias_b.shape[1]
    lam_init = 0.8 - 0.6 * math.exp(-0.3 * layer)
    heads = DIFF_HEADS
    n_lane_blocks = heads * 2 * DIFF_TQ // t
    tile = pl.BlockSpec((1, DIFF_TQ, heads * LANES), lambda i, h, n: (i, n, h))
    return pl.pallas_call(
        functools.partial(_diff_kernel, lam_init=lam_init, n_bias=n_bias),
        grid=(b, HB // heads, s // DIFF_TQ),
        in_specs=[
            pl.BlockSpec((1, heads * LANES, DIFF_TQ), lambda i, h, n: (i, h, n)),
            pl.BlockSpec((1, s, heads * LANES), lambda i, h, n: (i, 0, h)),
            pl.BlockSpec((1, heads * dv, s), lambda i, h, n: (i, h, 0)),
            pl.BlockSpec((heads, n_bias, t, t), lambda i, h, n: (h, 0, 0, 0)),
            pl.BlockSpec(lambda_vec.shape, lambda i, h, n: (0, 0)),
            pl.BlockSpec((dv, 1), lambda i, h, n: (0, 0)),
        ],
        out_specs=tile,
        out_shape=jax.ShapeDtypeStruct((b, s, w), BF16),
        scratch_shapes=[pltpu.VMEM((LANES, heads * 2 * DIFF_TQ), BF16),
                        pltpu.VMEM((1, heads * 2 * DIFF_TQ), F32),
                        pltpu.VMEM((n_lane_blocks, dv + DIFF_ONES, t), F32),
                        pltpu.VMEM((n_lane_blocks, t, t), F32),
                        pltpu.VMEM((n_lane_blocks, t, t), F32)],
        compiler_params=_cparams(("arbitrary", "arbitrary", "arbitrary")),
        name="diff_attn",
    )(qbt, kb, vbt, bias_b, lambda_vec, subln_g.reshape(dv, 1))


MERGE_TM = 512
CONV_HALO = 32
CONV_ROWS = 128


def _conv_tile(prev_ref, cur_ref, w_ref, cb_ref, lg_ref, lb_ref, sh_ref, y_ref, first_tile):
    halo = prev_ref[0]
    rows_in = cur_ref.shape[1]
    sh_ref[0, 0:CONV_HALO, :] = jnp.where(first_tile, jnp.zeros_like(halo), halo)
    sh_ref[0, CONV_HALO:, :] = cur_ref[0]
    for b in range(1, SUBLANES):
        sh_ref[b] = pltpu.roll(sh_ref[0], b, axis=0)
    for r0 in range(0, rows_in, CONV_ROWS):
        acc = jnp.zeros((CONV_ROWS, C_CONV), F32)
        for tap in range(CONV_K):
            back = CONV_K - 1 - tap
            a, b = divmod(back, SUBLANES)
            lo = r0 + CONV_HALO - a * SUBLANES
            acc = acc + sh_ref[b, lo:lo + CONV_ROWS, :] * w_ref[tap:tap + 1, :]
        y_ref[r0:r0 + CONV_ROWS, :] = acc
    y = y_ref[...] + cb_ref[...]
    mu = jnp.mean(y, axis=-1, keepdims=True)
    var = jnp.mean(jnp.square(y - mu), axis=-1, keepdims=True)
    z = (y - mu) * lax.rsqrt(var + EPS) * lg_ref[...] + lb_ref[...]
    return z * _sigmoid(z)


def _merge_kernel(x_ref, mod_ref, g_ref, ya_ref, yb_ref, uprev_ref, u_ref,
                  cw_ref, cb_ref, lg_ref, lb_ref, wg_ref, bg_ref, wb_ref, wo_ref, o_ref,
                  sh_ref, yc_ref):
    x = x_ref[0]
    d = x.shape[-1]
    h = _norm_mod(x, g_ref[...], mod_ref[0, 3:4, :], mod_ref[0, 4:5, :])
    hb = h.astype(BF16)
    y_c = _conv_tile(uprev_ref, u_ref, cw_ref, cb_ref, lg_ref, lb_ref, sh_ref, yc_ref,
                     pl.program_id(1) == 0).astype(BF16)
    mixed = jnp.zeros(x.shape, F32)
    for i, y in enumerate((ya_ref[0], yb_ref[0], y_c)):
        z = jnp.dot(hb, wg_ref[:, i * d:(i + 1) * d], preferred_element_type=F32)
        gate = _sigmoid(z + bg_ref[:, i * d:(i + 1) * d])
        mixed = mixed + gate * jnp.dot(y, wb_ref[i], preferred_element_type=F32)
    y = jnp.dot(mixed.astype(BF16), wo_ref[...], preferred_element_type=F32)
    o_ref[0] = x + mod_ref[0, 5:6, :] * y


def _merge(x, mod_l, g, y_a, y_b, u, conv_w, conv_b, ln_g, ln_b, layer,
           w_gate, b_gate, w_branch, w_out):
    b, s, d = x.shape
    c = u.shape[-1]
    per_tile = MERGE_TM // CONV_HALO
    tok = pl.BlockSpec((1, MERGE_TM, d), lambda i, j: (i, j, 0))
    br = pl.BlockSpec((1, MERGE_TM, MIX_W), lambda i, j: (i, j, 0))
    return pl.pallas_call(
        _merge_kernel,
        grid=(b, s // MERGE_TM),
        in_specs=[
            tok,
            pl.BlockSpec((1, 9, d), lambda i, j: (i, 0, 0)),
            _resident((1, d)),
            br, br,
            pl.BlockSpec((1, CONV_HALO, c), lambda i, j: (i, jnp.maximum(j * per_tile - 1, 0), 0)),
            br,
            _resident((CONV_K, c)), _resident((1, c)), _resident((1, c)), _resident((1, c)),
            _resident(w_gate.shape[1:], (layer,)),
            _resident((1, N_BRANCH * d)),
            _resident(w_branch.shape[1:], (layer,)),
            _resident(w_out.shape[1:], (layer,)),
        ],
        out_specs=tok,
        out_shape=jax.ShapeDtypeStruct(x.shape, F32),
        scratch_shapes=[pltpu.VMEM((SUBLANES, CONV_HALO + MERGE_TM, c), F32),
                        pltpu.VMEM((MERGE_TM, c), F32)],
        compiler_params=_cparams(("arbitrary", "arbitrary")),
        name="merge",
    )(x, mod_l, g.reshape(1, d), y_a, y_b, u, u, conv_w, conv_b.reshape(1, c),
      ln_g.reshape(1, c), ln_b.reshape(1, c), w_gate, b_gate.reshape(1, N_BRANCH * d),
      w_branch, w_out)


def _qk_gain_rows(qk_gain):
    ga_q = jnp.tile(qk_gain[0], HA)
    ga_k = jnp.tile(qk_gain[1], HA)
    gb_q = jnp.tile(jnp.concatenate([qk_gain[2], qk_gain[3]]), HB)
    gb_k = jnp.tile(jnp.concatenate([qk_gain[4], qk_gain[5]]), HB)
    return jnp.stack([ga_q, ga_k, gb_q, gb_k])


def kernel(x, c, rel_bias, w_ada, b_ada, norm_g, w_ffn_in, w_ffn_out, w_in, qk_gain, lambda_vec,
           subln_g, conv_w, conv_b, conv_ln_g, conv_ln_b, w_branch, w_gate, b_gate, w_out):
    b, s, d = x.shape
    mod = _ada_mod(c, w_ada, b_ada)
    bias_a, bias_b = _bias_tiles(rel_bias, s)
    w_ffn_in, w_ffn_out, w_in = w_ffn_in.astype(BF16), w_ffn_out.astype(BF16), w_in.astype(BF16)
    w_gate, w_branch, w_out = w_gate.astype(BF16), w_branch.astype(BF16), w_out.astype(BF16)
    for l in range(DEPTH):
        mod_l = mod[l]
        x = _ffn(x, mod_l, norm_g[l, 0], w_ffn_in, w_ffn_out, (l, 0), 0)
        qa, ka, va, qbt, kb, vbt, u = _proj(x, mod_l, norm_g[l, 1], w_in, l,
                                          _qk_gain_rows(qk_gain[l]))
        y_a = _dilated_mixer(qa, ka, va, bias_a)
        y_b = _diff_attention(qbt, kb, vbt, bias_b, lambda_vec[l], subln_g[l], l)
        x = _merge(x, mod_l, norm_g[l, 1], y_a, y_b, u, conv_w[l], conv_b[l], conv_ln_g[l],
                   conv_ln_b[l], l, w_gate, b_gate[l], w_branch, w_out)
        x = _ffn(x, mod_l, norm_g[l, 2], w_ffn_in, w_ffn_out, (l, 1), 6)
    return x
```

```python
import functools
import math

import numpy as np
import jax
import jax.numpy as jnp
from jax import lax
from jax.experimental import pallas as pl
from jax.experimental.pallas import tpu as pltpu

D_MODEL = 1024
DEPTH = 2
HEAD_DIM = 64
HA = 8
DIL_PATTERNS = ((128, 1), (512, 4), (2048, 16))
HB = 4
C_CONV = 512
CONV_K = 31
D_FF = 2816
N_BUCKETS = 32
REL_MAX_DIST = 2048
BLK = 128
MIX_W = 512
N_BRANCH = 3
EPS = 1e-6
NEG = -1e30
LOG2E = math.log2(math.e)

LANES = 128
SUBLANES = 8
VMEM_LIMIT = 56 * 1024 * 1024

F32 = jnp.float32
BF16 = jnp.bfloat16


def _cparams(sem):
    return pltpu.CompilerParams(dimension_semantics=sem, vmem_limit_bytes=VMEM_LIMIT)


def _sigmoid(x):
    return 1.0 / (1.0 + jnp.exp(-x))


def _resident(shape, lead=()):
    lead = tuple(lead)
    zeros = (0,) * len(shape)
    return pl.BlockSpec((None,) * len(lead) + tuple(shape), lambda *_: lead + zeros,
                        pipeline_mode=pl.Buffered(1))


def _norm_mod(x, g, shift, scale):
    ms = jnp.mean(x * x, axis=-1, keepdims=True)
    y = x * lax.rsqrt(ms + EPS) * g
    return y * (1.0 + scale) + shift


ADA_TN = 2304


def _ada_kernel(c_ref, w_ref, b_ref, o_ref):
    c = c_ref[...]
    a = c * _sigmoid(c)
    o_ref[0] = jnp.dot(a, w_ref[0], preferred_element_type=F32,
                       precision=lax.Precision.HIGHEST) + b_ref[0]


def _ada_mod(c, w_ada, b_ada):
    b, d = c.shape
    rows = 8
    c_pad = jnp.pad(c, ((0, rows - b), (0, 0)))
    n = w_ada.shape[-1]
    out = pl.pallas_call(
        _ada_kernel,
        grid=(DEPTH, n // ADA_TN),
        in_specs=[
            pl.BlockSpec((rows, d), lambda l, j: (0, 0)),
            pl.BlockSpec((1, d, ADA_TN), lambda l, j: (l, 0, j)),
            pl.BlockSpec((1, 1, ADA_TN), lambda l, j: (l, 0, j)),
        ],
        out_specs=pl.BlockSpec((1, rows, ADA_TN), lambda l, j: (l, 0, j)),
        out_shape=jax.ShapeDtypeStruct((DEPTH, rows, n), F32),
        compiler_params=_cparams(("arbitrary", "arbitrary")),
        name="ada_mod",
    )(c_pad, w_ada, b_ada.reshape(DEPTH, 1, n))
    return out[:, :b].reshape(DEPTH, b, 9, d)


FFN_TM = 1024
FFN_TF = 256
assert D_FF % FFN_TF == 0


def _ffn_kernel(x_ref, mod_ref, g_ref, wup_ref, wdn_ref, o_ref, act_ref, *, k0):
    x = x_ref[0]
    h = _norm_mod(x, g_ref[...], mod_ref[0, k0:k0 + 1, :], mod_ref[0, k0 + 1:k0 + 2, :])
    hb = h.astype(BF16)
    for j in range(D_FF // FFN_TF):
        lo = j * FFN_TF
        gate = jnp.dot(hb, wup_ref[:, lo:lo + FFN_TF], preferred_element_type=F32)
        up = jnp.dot(hb, wup_ref[:, D_FF + lo:D_FF + lo + FFN_TF], preferred_element_type=F32)
        act_ref[:, lo:lo + FFN_TF] = (gate * _sigmoid(gate) * up).astype(BF16)
    y = jnp.dot(act_ref[...], wdn_ref[...], preferred_element_type=F32)
    o_ref[0] = x + (0.5 * mod_ref[0, k0 + 2:k0 + 3, :]) * y


def _ffn(x, mod_l, g, w_up, w_dn, lead, k0):
    b, s, d = x.shape
    tok = pl.BlockSpec((1, FFN_TM, d), lambda i, j: (i, j, 0))
    return pl.pallas_call(
        functools.partial(_ffn_kernel, k0=k0),
        grid=(b, s // FFN_TM),
        in_specs=[
            tok,
            pl.BlockSpec((1, 9, d), lambda i, j: (i, 0, 0)),
            _resident((1, d)),
            _resident((d, 2 * D_FF), lead),
            _resident((D_FF, d), lead),
        ],
        out_specs=tok,
        out_shape=jax.ShapeDtypeStruct(x.shape, F32),
        scratch_shapes=[pltpu.VMEM((FFN_TM, D_FF), BF16)],
        compiler_params=_cparams(("arbitrary", "arbitrary")),
        name="ffn",
    )(x, mod_l, g.reshape(1, d), w_up, w_dn)


PROJ_TM = 1024


def _head_rmsnorm(acc, gain):
    rows = acc.shape[0]
    low = lax.broadcasted_iota(jnp.int32, (rows, LANES), 1) < HEAD_DIM
    outs = []
    for c in range(acc.shape[1] // LANES):
        xc = acc[:, c * LANES:(c + 1) * LANES]
        sq = xc * xc
        s_lo = jnp.sum(jnp.where(low, sq, 0.0), axis=-1, keepdims=True)
        s_hi = jnp.sum(jnp.where(low, 0.0, sq), axis=-1, keepdims=True)
        r_lo = lax.rsqrt(s_lo * (1.0 / HEAD_DIM) + EPS)
        r_hi = lax.rsqrt(s_hi * (1.0 / HEAD_DIM) + EPS)
        outs.append(xc * jnp.where(low, r_lo, r_hi) * gain[:, c * LANES:(c + 1) * LANES])
    return jnp.concatenate(outs, axis=-1)


def _proj_kernel(x_ref, mod_ref, g_ref, w_ref, gain_ref,
                 qa_ref, ka_ref, va_ref, qbt_ref, kb_ref, vbt_ref, u_ref):
    x = x_ref[0]
    h = _norm_mod(x, g_ref[...], mod_ref[0, 3:4, :], mod_ref[0, 4:5, :])
    hb = h.astype(BF16)
    w = MIX_W

    def col(j):
        return jnp.dot(hb, w_ref[:, j * w:(j + 1) * w], preferred_element_type=F32)

    q_scale = 1.0 / math.sqrt(HEAD_DIM)
    qa_ref[0] = (_head_rmsnorm(col(0), gain_ref[0:1, :]) * (q_scale * LOG2E)).astype(BF16)
    ka_ref[0] = _head_rmsnorm(col(1), gain_ref[1:2, :]).astype(BF16)
    va_ref[0] = col(2).astype(BF16)
    qbt_ref[0] = (_head_rmsnorm(col(3), gain_ref[2:3, :]) * (q_scale * LOG2E)).T.astype(BF16)
    kb_ref[0] = _head_rmsnorm(col(4), gain_ref[3:4, :]).astype(BF16)
    vbt_ref[0] = col(5).T.astype(BF16)
    u_ref[0] = col(6) * _sigmoid(col(7))


def _proj(x, mod_l, g, w_in, layer, gains):
    b, s, d = x.shape
    tok_in = pl.BlockSpec((1, PROJ_TM, d), lambda i, j: (i, j, 0))
    tok_out = pl.BlockSpec((1, PROJ_TM, MIX_W), lambda i, j: (i, j, 0))
    bf = jax.ShapeDtypeStruct((b, s, MIX_W), BF16)
    bf_t = jax.ShapeDtypeStruct((b, MIX_W, s), BF16)
    tr_out = pl.BlockSpec((1, MIX_W, PROJ_TM), lambda i, j: (i, 0, j))
    return pl.pallas_call(
        _proj_kernel,
        grid=(b, s // PROJ_TM),
        in_specs=[
            tok_in,
            pl.BlockSpec((1, 9, d), lambda i, j: (i, 0, 0)),
            _resident((1, d)),
            _resident(w_in.shape[1:], (layer,)),
            _resident(gains.shape),
        ],
        out_specs=[tok_out] * 3 + [tr_out, tok_out, tr_out, tok_out],
        out_shape=[bf, bf, bf, bf_t, bf, bf_t, jax.ShapeDtypeStruct((b, s, MIX_W), F32)],
        compiler_params=_cparams(("arbitrary", "arbitrary")),
        name="proj",
    )(x, mod_l, g.reshape(1, d), w_in, gains)


def _bucket_thresholds():
    max_exact = N_BUCKETS // 2
    d = np.arange(0, 2 * REL_MAX_DIST + 2)
    df = np.maximum(d.astype(np.float32), np.float32(1.0))
    large = max_exact + (np.log(df / np.float32(max_exact))
                         / np.float32(math.log(REL_MAX_DIST / max_exact))
                         * np.float32(N_BUCKETS - max_exact)).astype(np.int32)
    bucket = np.where(d < max_exact, d, np.minimum(large, N_BUCKETS - 1))
    return [int(np.argmax(bucket >= b)) for b in range(1, N_BUCKETS)]


_THRESHOLDS = _bucket_thresholds()


def _bias_of_dist(dist, tab_ref, head):
    val = jnp.full(dist.shape, tab_ref[0, head], F32)
    for b in range(1, N_BUCKETS):
        val = jnp.where(dist >= _THRESHOLDS[b - 1], tab_ref[b, head], val)
    return val


DIFF_T = 256


def _diff_bias_tiles(seq):
    first_const = -(-(_THRESHOLDS[-1] + DIFF_T - 1) // DIFF_T)
    return min(seq // DIFF_T, first_const + 1)


def _bias_diff_kernel(tab_ref, o_ref):
    h = pl.program_id(0)
    delta = pl.program_id(1)
    j = lax.broadcasted_iota(jnp.int32, (DIFF_T, DIFF_T), 0)
    i = lax.broadcasted_iota(jnp.int32, (DIFF_T, DIFF_T), 1)
    dist = jnp.maximum(delta * DIFF_T + i - j, 0)
    o_ref[0, 0] = _bias_of_dist(dist, tab_ref, HA + h) * LOG2E


def _bias_dil_kernel(tab_ref, o_ref):
    p = pl.program_id(0)
    hp = pl.program_id(1)
    dil = jnp.where(p == 0, DIL_PATTERNS[0][1],
                    jnp.where(p == 1, DIL_PATTERNS[1][1], DIL_PATTERNS[2][1]))
    j = lax.broadcasted_iota(jnp.int32, (2 * BLK, BLK), 0)
    i = lax.broadcasted_iota(jnp.int32, (2 * BLK, BLK), 1)
    rel = i + BLK - j
    band = (rel >= 0) & (rel <= BLK)
    dist = jnp.maximum(rel, 0) * dil
    for sub in range(2):
        bias = _bias_of_dist(dist, tab_ref, 2 * hp + sub) * LOG2E
        lanes = slice(sub * BLK, (sub + 1) * BLK)
        o_ref[0, 0, 0, :, lanes] = jnp.where(band & (j >= BLK), bias, NEG)
        o_ref[0, 1, 0, :, lanes] = jnp.where(band, bias, NEG)


def _bias_tiles(rel_bias, seq):
    smem = pl.BlockSpec(memory_space=pltpu.SMEM)
    n_delta = _diff_bias_tiles(seq)
    bias_b = pl.pallas_call(
        _bias_diff_kernel,
        grid=(HB, n_delta),
        in_specs=[smem],
        out_specs=pl.BlockSpec((1, 1, DIFF_T, DIFF_T), lambda h, t: (h, t, 0, 0)),
        out_shape=jax.ShapeDtypeStruct((HB, n_delta, DIFF_T, DIFF_T), F32),
        compiler_params=_cparams(("arbitrary", "arbitrary")),
        name="bias_diff",
    )(rel_bias)
    n_pat = len(DIL_PATTERNS)
    bias_a = pl.pallas_call(
        _bias_dil_kernel,
        grid=(n_pat, HA // 2),
        in_specs=[smem],
        out_specs=pl.BlockSpec((1, 2, 1, 2 * BLK, 2 * BLK), lambda p, h: (p, 0, h, 0, 0)),
        out_shape=jax.ShapeDtypeStruct((n_pat, 2, HA // 2, 2 * BLK, 2 * BLK), F32),
        compiler_params=_cparams(("arbitrary", "arbitrary")),
        name="bias_dil",
    )(rel_bias)
    return bias_a, bias_b


DIL_CHAINS = 8
DIL_STRIDE = 4
DIL_ONES = 16
DIL_COMBINE_ROWS = 512


def _dil_kernel(q_ref, k_ref, v_ref, bias_ref, y_ref,
                q32_ref, k32_ref, v32_ref, q4_ref, k4_ref, v4_ref, qd_ref, kd_ref, vtd_ref,
                ot_ref, lt_ref, on_ref, ln_ref, sa_ref, sb_ref):
    seq = q_ref.shape[1]
    pad = BLK
    quarter = seq // DIL_STRIDE
    nat = (q32_ref, k32_ref, v32_ref)
    by4 = (q4_ref, k4_ref, v4_ref)
    for src_ref, nat_ref, by4_ref in zip((q_ref, k_ref, v_ref), nat, by4):
        nat_ref[...] = src_ref[0].astype(F32)
        for rho in range(DIL_STRIDE):
            by4_ref[rho * quarter:(rho + 1) * quarter, :] = nat_ref[pl.ds(rho, quarter, stride=DIL_STRIDE), :]

    def residue(which, dil, r):
        if dil == 1:
            return nat[which][...]
        if dil == DIL_STRIDE:
            return by4[which][r * quarter:(r + 1) * quarter, :]
        a, rho = divmod(r, DIL_STRIDE)
        return by4[which][pl.ds(rho * quarter + a, seq // dil, stride=DIL_STRIDE), :]

    for p in range(len(DIL_PATTERNS)):
        kd_ref[p, 0:pad, :] = jnp.zeros((pad, LANES), BF16)
        vtd_ref[p, :, 0:pad] = jnp.zeros((LANES, pad), BF16)
    low = lax.broadcasted_iota(jnp.int32, (BLK, LANES), 1) < HEAD_DIM
    top = lax.broadcasted_iota(jnp.int32, (LANES, BLK), 0) < HEAD_DIM
    ones = jnp.ones((DIL_ONES, 2 * BLK), BF16)

    for p, (_, dil) in enumerate(DIL_PATTERNS):
        sub_len = seq // dil
        nb = sub_len // BLK
        for r in range(dil):
            qd_ref[p, r * sub_len:(r + 1) * sub_len, :] = residue(0, dil, r).astype(BF16)
            kd_ref[p, pad + r * sub_len:pad + (r + 1) * sub_len, :] = residue(1, dil, r).astype(BF16)
            vtd_ref[p, :, pad + r * sub_len:pad + (r + 1) * sub_len] = residue(2, dil, r).astype(BF16).T

        def produce(j, s_ref, p=p, nb=nb):
            for c in range(DIL_CHAINS):
                chain = j * DIL_CHAINS + c
                base = pl.multiple_of(chain * BLK, BLK)
                var = jnp.where((chain & (nb - 1)) == 0, 0, 1)
                q = qd_ref[p, pl.ds(base, BLK), :]
                zero = jnp.zeros_like(q)
                qs = jnp.concatenate([jnp.where(low, q, zero), jnp.where(low, zero, q)], axis=0)
                kc = kd_ref[p, pl.ds(base, 2 * BLK), :]
                sc = lax.dot_general(kc, qs, (((1,), (1,)), ((), ())), preferred_element_type=F32)
                s_ref[c] = sc + bias_ref[p, var]

        def consume(j, s_ref, p=p):
            for c in range(DIL_CHAINS):
                base = pl.multiple_of((j * DIL_CHAINS + c) * BLK, BLK)
                sc = s_ref[c]
                m = jnp.max(sc, axis=0, keepdims=True)
                pexp = jnp.exp2(sc - m).astype(BF16)
                vt = jnp.concatenate([vtd_ref[p, :, pl.ds(base, 2 * BLK)], ones], axis=0)
                pv = jnp.dot(vt, pexp, preferred_element_type=F32)
                den = pv[LANES:LANES + 1, :]
                o = pv[0:LANES, :] / den
                lse = m + jnp.log2(den)
                chain = j * DIL_CHAINS + c
                ot_ref[chain] = jnp.where(top, o[:, :BLK], o[:, BLK:])
                lt_ref[chain] = jnp.where(top, lse[:, :BLK], lse[:, BLK:])

        n_trips = seq // BLK // DIL_CHAINS
        produce(0, sa_ref)

        def two_trips(i, carry, produce=produce, consume=consume):
            j = 2 * i
            produce(j + 1, sb_ref)
            consume(j, sa_ref)
            produce(j + 2, sa_ref)
            consume(j + 1, sb_ref)
            return carry

        lax.fori_loop(0, n_trips // 2 - 1, two_trips, 0)
        produce(n_trips - 1, sb_ref)
        consume(n_trips - 2, sa_ref)
        consume(n_trips - 1, sb_ref)
        for chain in range(seq // BLK):
            r, n = divmod(chain, nb)
            if dil <= DIL_STRIDE:
                rows = pl.ds(n * BLK * dil + r, BLK, stride=dil)
                on_ref[p, rows, :] = ot_ref[chain].T
                ln_ref[p, rows, :] = lt_ref[chain].T
            else:
                a, rho = divmod(r, DIL_STRIDE)
                rows = pl.ds(rho * quarter + n * BLK * DIL_STRIDE + a, BLK, stride=DIL_STRIDE)
                q32_ref[rows, :] = ot_ref[chain].T
                k32_ref[rows, :] = lt_ref[chain].T
        if dil > DIL_STRIDE:
            for rho in range(DIL_STRIDE):
                rows = pl.ds(rho, quarter, stride=DIL_STRIDE)
                on_ref[p, rows, :] = q32_ref[rho * quarter:(rho + 1) * quarter, :]
                ln_ref[p, rows, :] = k32_ref[rho * quarter:(rho + 1) * quarter, :]

    def combine(i, carry):
        rows = pl.ds(pl.multiple_of(i * DIL_COMBINE_ROWS, DIL_COMBINE_ROWS), DIL_COMBINE_ROWS)
        l1, l2, l3 = ln_ref[0, rows, :], ln_ref[1, rows, :], ln_ref[2, rows, :]
        mx = jnp.maximum(jnp.maximum(l1, l2), l3)
        e1 = jnp.exp2(l1 - mx)
        e2 = jnp.exp2(l2 - mx)
        e3 = jnp.exp2(l3 - mx)
        num = e1 * on_ref[0, rows, :] + e2 * on_ref[1, rows, :] + e3 * on_ref[2, rows, :]
        y_ref[0, rows, :] = (num / (e1 + e2 + e3)).astype(y_ref.dtype)
        return carry

    lax.fori_loop(0, seq // DIL_COMBINE_ROWS, combine, 0)


def _dilated_mixer(qa, ka, va, bias_a):
    b, s, w = qa.shape
    n_pat = len(DIL_PATTERNS)
    dils = [d for _, d in DIL_PATTERNS]
    assert all(d in (1, DIL_STRIDE, DIL_STRIDE ** 2) for d in dils) and dils[-1] == max(dils)
    pair = pl.BlockSpec((1, s, LANES), lambda i, h: (i, 0, h))
    return pl.pallas_call(
        _dil_kernel,
        grid=(b, HA // 2),
        in_specs=[pair, pair, pair,
                  pl.BlockSpec((n_pat, 2, None, 2 * BLK, 2 * BLK), lambda i, h: (0, 0, h, 0, 0))],
        out_specs=pair,
        out_shape=jax.ShapeDtypeStruct((b, s, w), BF16),
        scratch_shapes=[
            pltpu.VMEM((s, LANES), F32), pltpu.VMEM((s, LANES), F32), pltpu.VMEM((s, LANES), F32),
            pltpu.VMEM((s, LANES), F32), pltpu.VMEM((s, LANES), F32), pltpu.VMEM((s, LANES), F32),
            pltpu.VMEM((n_pat, s, LANES), BF16), pltpu.VMEM((n_pat, BLK + s, LANES), BF16),
            pltpu.VMEM((n_pat, LANES, BLK + s), BF16),
            pltpu.VMEM((s // BLK, LANES, BLK), F32), pltpu.VMEM((s // BLK, LANES, BLK), F32),
            pltpu.VMEM((n_pat, s, LANES), F32), pltpu.VMEM((n_pat, s, LANES), F32),
            pltpu.VMEM((DIL_CHAINS, 2 * BLK, 2 * BLK), F32),
            pltpu.VMEM((DIL_CHAINS, 2 * BLK, 2 * BLK), F32),
        ],
        compiler_params=_cparams(("arbitrary", "arbitrary")),
        name="dilated",
    )(qa, ka, va, bias_a)


DIFF_TQ = 2 * DIFF_T
DIFF_HEADS = 2
DIFF_ONES = 16


def _diff_kernel(qt_ref, k_ref, vt_ref, bias_ref, lam_ref, sg_ref, o_ref,
                 qs_ref, m_ref, acc_ref, s0_ref, s1_ref, *, lam_init, n_bias):
    qi = pl.program_id(2)
    t = DIFF_T
    dv = 2 * HEAD_DIM
    n_blk = 2 * DIFF_TQ // t
    width = 2 * DIFF_TQ
    first = lax.broadcasted_iota(jnp.int32, (LANES, DIFF_TQ), 0) < HEAD_DIM
    for hd in range(DIFF_HEADS):
        q = qt_ref[0, hd * LANES:(hd + 1) * LANES, :]
        zero = jnp.zeros_like(q)
        qs_ref[:, hd * width:hd * width + DIFF_TQ] = jnp.where(first, q, zero)
        qs_ref[:, hd * width + DIFF_TQ:(hd + 1) * width] = jnp.where(first, zero, q)
    m_ref[...] = jnp.full(m_ref.shape, NEG, F32)
    acc_ref[...] = jnp.zeros(acc_ref.shape, F32)
    ones = jnp.ones((DIFF_ONES, t), BF16)

    def produce(kt, s_ref, blocks):
        off = pl.multiple_of(kt * t, t)
        for hd in range(DIFF_HEADS):
            k = k_ref[0, pl.ds(off, t), hd * LANES:(hd + 1) * LANES]
            for blk in blocks:
                lanes = slice(hd * width + blk * t, hd * width + (blk + 1) * t)
                s = jnp.dot(k, qs_ref[:, lanes], preferred_element_type=F32)
                delta = 2 * qi + blk % 2 - kt
                s_ref[hd * n_blk + blk] = s + bias_ref[hd, jnp.minimum(delta, n_bias - 1)]

    def consume(kt, s_ref, modes):
        off = pl.multiple_of(kt * t, t)
        for hd in range(DIFF_HEADS):
            vt = jnp.concatenate([vt_ref[0, hd * dv:(hd + 1) * dv, pl.ds(off, t)], ones], axis=0)
            for blk, mode in enumerate(modes):
                if mode == "skip":
                    continue
                g = hd * n_blk + blk
                lanes = slice(g * t, (g + 1) * t)
                s = s_ref[g]
                if mode == "diag":
                    key = lax.broadcasted_iota(jnp.int32, (t, t), 0)
                    qry = lax.broadcasted_iota(jnp.int32, (t, t), 1)
                    s = jnp.where(qry >= key, s, NEG)
                m_old = m_ref[:, lanes]
                m_new = jnp.maximum(m_old, jnp.max(s, axis=0, keepdims=True))
                alpha = jnp.exp2(m_old - m_new)
                p = jnp.exp2(s - m_new).astype(BF16)
                pv = jnp.dot(vt, p, preferred_element_type=F32)
                acc_ref[g] = alpha * acc_ref[g] + pv
                m_ref[:, lanes] = m_new

    every = tuple(range(n_blk))
    full = ("full",) * n_blk
    produce(0, s0_ref, every)

    def body(j, carry):
        kt = 2 * j
        produce(kt + 1, s1_ref, every)
        consume(kt, s0_ref, full)
        produce(kt + 2, s0_ref, every)
        consume(kt + 1, s1_ref, full)
        return carry

    def body_twice(j, carry):
        return body(2 * j + 1, body(2 * j, carry))

    pairs = lax.shift_right_logical(qi, 1)
    lax.fori_loop(0, pairs, body_twice, 0)
    lax.fori_loop(2 * pairs, qi, body, 0)
    produce(2 * qi + 1, s1_ref, (1, 3))
    consume(2 * qi, s0_ref, ("diag", "full", "diag", "full"))
    consume(2 * qi + 1, s1_ref, ("skip", "diag", "skip", "diag"))

    lv = lam_ref[...]
    s01 = jnp.sum(lv[0:1] * lv[1:2], axis=-1, keepdims=True)
    s23 = jnp.sum(lv[2:3] * lv[3:4], axis=-1, keepdims=True)
    lam = jnp.exp(s01) - jnp.exp(s23) + lam_init
    for hd in range(DIFF_HEADS):
        o_t = [acc_ref[hd * n_blk + blk, 0:dv, :] / acc_ref[hd * n_blk + blk, dv:dv + 1, :]
               for blk in range(n_blk)]
        half = n_blk // 2
        ob = jnp.concatenate([o_t[blk] - lam * o_t[half + blk] for blk in range(half)],
                             axis=1)
        ms = jnp.mean(ob * ob, axis=0, keepdims=True)
        y = ob * lax.rsqrt(ms + EPS) * sg_ref[...]
        o_ref[0, :, hd * LANES:(hd + 1) * LANES] = (y * (1.0 - lam_init)).T.astype(o_ref.dtype)


def _diff_attention(qbt, kb, vbt, bias_b, lambda_vec, subln_g, layer):
    b, s, w = kb.shape
    t = DIFF_T
    dv = 2 * HEAD_DIM
    n_bias = bias_b.shape[1]
    lam_init = 0.8 - 0.6 * math.exp(-0.3 * layer)
    heads = DIFF_HEADS
    n_lane_blocks = heads * 2 * DIFF_TQ // t
    tile = pl.BlockSpec((1, DIFF_TQ, heads * LANES), lambda i, h, n: (i, n, h))
    return pl.pallas_call(
        functools.partial(_diff_kernel, lam_init=lam_init, n_bias=n_bias),
        grid=(b, HB // heads, s // DIFF_TQ),
        in_specs=[
            pl.BlockSpec((1, heads * LANES, DIFF_TQ), lambda i, h, n: (i, h, n)),
            pl.BlockSpec((1, s, heads * LANES), lambda i, h, n: (i, 0, h)),
            pl.BlockSpec((1, heads * dv, s), lambda i, h, n: (i, h, 0)),
            pl.BlockSpec((heads, n_bias, t, t), lambda i, h, n: (h, 0, 0, 0)),
            pl.BlockSpec(lambda_vec.shape, lambda i, h, n: (0, 0)),
            pl.BlockSpec((dv, 1), lambda i, h, n: (0, 0)),
        ],
        out_specs=tile,
        out_shape=jax.ShapeDtypeStruct((b, s, w), BF16),
        scratch_shapes=[pltpu.VMEM((LANES, heads * 2 * DIFF_TQ), BF16),
                        pltpu.VMEM((1, heads * 2 * DIFF_TQ), F32),
                        pltpu.VMEM((n_lane_blocks, dv + DIFF_ONES, t), F32),
                        pltpu.VMEM((n_lane_blocks, t, t), F32),
                        pltpu.VMEM((n_lane_blocks, t, t), F32)],
        compiler_params=_cparams(("arbitrary", "arbitrary", "arbitrary")),
        name="diff_attn",
    )(qbt, kb, vbt, bias_b, lambda_vec, subln_g.reshape(dv, 1))


MERGE_TM = 512
CONV_HALO = 32
CONV_ROWS = 128


def _conv_tile(prev_ref, cur_ref, w_ref, cb_ref, lg_ref, lb_ref, sh_ref, y_ref, first_tile):
    halo = prev_ref[0]
    rows_in = cur_ref.shape[1]
    sh_ref[0, 0:CONV_HALO, :] = jnp.where(first_tile, jnp.zeros_like(halo), halo)
    sh_ref[0, CONV_HALO:, :] = cur_ref[0]
    for b in range(1, SUBLANES):
        sh_ref[b] = pltpu.roll(sh_ref[0], b, axis=0)
    for r0 in range(0, rows_in, CONV_ROWS):
        acc = jnp.zeros((CONV_ROWS, C_CONV), F32)
        for tap in range(CONV_K):
            back = CONV_K - 1 - tap
            a, b = divmod(back, SUBLANES)
            lo = r0 + CONV_HALO - a * SUBLANES
            acc = acc + sh_ref[b, lo:lo + CONV_ROWS, :] * w_ref[tap:tap + 1, :]
        y_ref[r0:r0 + CONV_ROWS, :] = acc
    y = y_ref[...] + cb_ref[...]
    mu = jnp.mean(y, axis=-1, keepdims=True)
    var = jnp.mean(jnp.square(y - mu), axis=-1, keepdims=True)
    z = (y - mu) * lax.rsqrt(var + EPS) * lg_ref[...] + lb_ref[...]
    return z * _sigmoid(z)


def _merge_kernel(x_ref, mod_ref, g_ref, ya_ref, yb_ref, uprev_ref, u_ref,
                  cw_ref, cb_ref, lg_ref, lb_ref, wg_ref, bg_ref, wb_ref, wo_ref, o_ref,
                  sh_ref, yc_ref):
    x = x_ref[0]
    d = x.shape[-1]
    h = _norm_mod(x, g_ref[...], mod_ref[0, 3:4, :], mod_ref[0, 4:5, :])
    hb = h.astype(BF16)
    y_c = _conv_tile(uprev_ref, u_ref, cw_ref, cb_ref, lg_ref, lb_ref, sh_ref, yc_ref,
                     pl.program_id(1) == 0).astype(BF16)
    mixed = jnp.zeros(x.shape, F32)
    for i, y in enumerate((ya_ref[0], yb_ref[0], y_c)):
        z = jnp.dot(hb, wg_ref[:, i * d:(i + 1) * d], preferred_element_type=F32)
        gate = _sigmoid(z + bg_ref[:, i * d:(i + 1) * d])
        mixed = mixed + gate * jnp.dot(y, wb_ref[i], preferred_element_type=F32)
    y = jnp.dot(mixed.astype(BF16), wo_ref[...], preferred_element_type=F32)
    o_ref[0] = x + mod_ref[0, 5:6, :] * y


def _merge(x, mod_l, g, y_a, y_b, u, conv_w, conv_b, ln_g, ln_b, layer,
           w_gate, b_gate, w_branch, w_out):
    b, s, d = x.shape
    c = u.shape[-1]
    per_tile = MERGE_TM // CONV_HALO
    tok = pl.BlockSpec((1, MERGE_TM, d), lambda i, j: (i, j, 0))
    br = pl.BlockSpec((1, MERGE_TM, MIX_W), lambda i, j: (i, j, 0))
    return pl.pallas_call(
        _merge_kernel,
        grid=(b, s // MERGE_TM),
        in_specs=[
            tok,
            pl.BlockSpec((1, 9, d), lambda i, j: (i, 0, 0)),
            _resident((1, d)),
            br, br,
            pl.BlockSpec((1, CONV_HALO, c), lambda i, j: (i, jnp.maximum(j * per_tile - 1, 0), 0)),
            br,
            _resident((CONV_K, c)), _resident((1, c)), _resident((1, c)), _resident((1, c)),
            _resident(w_gate.shape[1:], (layer,)),
            _resident((1, N_BRANCH * d)),
            _resident(w_branch.shape[1:], (layer,)),
            _resident(w_out.shape[1:], (layer,)),
        ],
        out_specs=tok,
        out_shape=jax.ShapeDtypeStruct(x.shape, F32),
        scratch_shapes=[pltpu.VMEM((SUBLANES, CONV_HALO + MERGE_TM, c), F32),
                        pltpu.VMEM((MERGE_TM, c), F32)],
        compiler_params=_cparams(("arbitrary", "arbitrary")),
        name="merge",
    )(x, mod_l, g.reshape(1, d), y_a, y_b, u, u, conv_w, conv_b.reshape(1, c),
      ln_g.reshape(1, c), ln_b.reshape(1, c), w_gate, b_gate.reshape(1, N_BRANCH * d),
      w_branch, w_out)


def _qk_gain_rows(qk_gain):
    ga_q = jnp.tile(qk_gain[0], HA)
    ga_k = jnp.tile(qk_gain[1], HA)
    gb_q = jnp.tile(jnp.concatenate([qk_gain[2], qk_gain[3]]), HB)
    gb_k = jnp.tile(jnp.concatenate([qk_gain[4], qk_gain[5]]), HB)
    return jnp.stack([ga_q, ga_k, gb_q, gb_k])


def kernel(x, c, rel_bias, w_ada, b_ada, norm_g, w_ffn_in, w_ffn_out, w_in, qk_gain, lambda_vec,
           subln_g, conv_w, conv_b, conv_ln_g, conv_ln_b, w_branch, w_gate, b_gate, w_out):
    b, s, d = x.shape
    mod = _ada_mod(c, w_ada, b_ada)
    bias_a, bias_b = _bias_tiles(rel_bias, s)
    w_ffn_in, w_ffn_out, w_in = w_ffn_in.astype(BF16), w_ffn_out.astype(BF16), w_in.astype(BF16)
    w_gate, w_branch, w_out = w_gate.astype(BF16), w_branch.astype(BF16), w_out.astype(BF16)
    for l in range(DEPTH):
        mod_l = mod[l]
        x = _ffn(x, mod_l, norm_g[l, 0], w_ffn_in, w_ffn_out, (l, 0), 0)
        qa, ka, va, qbt, kb, vbt, u = _proj(x, mod_l, norm_g[l, 1], w_in, l,
                                          _qk_gain_rows(qk_gain[l]))
        y_a = _dilated_mixer(qa, ka, va, bias_a)
        y_b = _diff_attention(qbt, kb, vbt, bias_b, lambda_vec[l], subln_g[l], l)
        x = _merge(x, mod_l, norm_g[l, 1], y_a, y_b, u, conv_w[l], conv_b[l], conv_ln_g[l],
                   conv_ln_b[l], l, w_gate, b_gate[l], w_branch, w_out)
        x = _ffn(x, mod_l, norm_g[l, 2], w_ffn_in, w_ffn_out, (l, 1), 6)
    return x
```

```python
import functools
import math

import numpy as np
import jax
import jax.numpy as jnp
from jax import lax
from jax.experimental import pallas as pl
from jax.experimental.pallas import tpu as pltpu

D_MODEL = 1024
DEPTH = 2
HEAD_DIM = 64
HA = 8
DIL_PATTERNS = ((128, 1), (512, 4), (2048, 16))
HB = 4
C_CONV = 512
CONV_K = 31
D_FF = 2816
N_BUCKETS = 32
REL_MAX_DIST = 2048
BLK = 128
MIX_W = 512
N_BRANCH = 3
EPS = 1e-6
NEG = -1e30
LOG2E = math.log2(math.e)

LANES = 128
SUBLANES = 8
VMEM_LIMIT = 56 * 1024 * 1024

F32 = jnp.float32
BF16 = jnp.bfloat16


def _cparams(sem):
    return pltpu.CompilerParams(dimension_semantics=sem, vmem_limit_bytes=VMEM_LIMIT)


def _sigmoid(x):
    return 1.0 / (1.0 + jnp.exp(-x))


def _resident(shape, lead=()):
    lead = tuple(lead)
    zeros = (0,) * len(shape)
    return pl.BlockSpec((None,) * len(lead) + tuple(shape), lambda *_: lead + zeros,
                        pipeline_mode=pl.Buffered(1))


def _norm_mod(x, g, shift, scale):
    ms = jnp.mean(x * x, axis=-1, keepdims=True)
    y = x * lax.rsqrt(ms + EPS) * g
    return y * (1.0 + scale) + shift


ADA_TN = 2304


def _ada_kernel(c_ref, w_ref, b_ref, o_ref):
    c = c_ref[...]
    a = c * _sigmoid(c)
    o_ref[0] = jnp.dot(a, w_ref[0], preferred_element_type=F32,
                       precision=lax.Precision.HIGHEST) + b_ref[0]


def _ada_mod(c, w_ada, b_ada):
    b, d = c.shape
    rows = 8
    c_pad = jnp.pad(c, ((0, rows - b), (0, 0)))
    n = w_ada.shape[-1]
    out = pl.pallas_call(
        _ada_kernel,
        grid=(DEPTH, n // ADA_TN),
        in_specs=[
            pl.BlockSpec((rows, d), lambda l, j: (0, 0)),
            pl.BlockSpec((1, d, ADA_TN), lambda l, j: (l, 0, j)),
            pl.BlockSpec((1, 1, ADA_TN), lambda l, j: (l, 0, j)),
        ],
        out_specs=pl.BlockSpec((1, rows, ADA_TN), lambda l, j: (l, 0, j)),
        out_shape=jax.ShapeDtypeStruct((DEPTH, rows, n), F32),
        compiler_params=_cparams(("arbitrary", "arbitrary")),
        name="ada_mod",
    )(c_pad, w_ada, b_ada.reshape(DEPTH, 1, n))
    return out[:, :b].reshape(DEPTH, b, 9, d)


FFN_TM = 1024
FFN_TF = 256
assert D_FF % FFN_TF == 0


def _ffn_kernel(x_ref, mod_ref, g_ref, wup_ref, wdn_ref, o_ref, act_ref, *, k0):
    x = x_ref[0]
    h = _norm_mod(x, g_ref[...], mod_ref[0, k0:k0 + 1, :], mod_ref[0, k0 + 1:k0 + 2, :])
    hb = h.astype(BF16)
    for j in range(D_FF // FFN_TF):
        lo = j * FFN_TF
        gate = jnp.dot(hb, wup_ref[:, lo:lo + FFN_TF], preferred_element_type=F32)
        up = jnp.dot(hb, wup_ref[:, D_FF + lo:D_FF + lo + FFN_TF], preferred_element_type=F32)
        act_ref[:, lo:lo + FFN_TF] = (gate * _sigmoid(gate) * up).astype(BF16)
    y = jnp.dot(act_ref[...], wdn_ref[...], preferred_element_type=F32)
    o_ref[0] = x + (0.5 * mod_ref[0, k0 + 2:k0 + 3, :]) * y


def _ffn(x, mod_l, g, w_up, w_dn, lead, k0):
    b, s, d = x.shape
    tok = pl.BlockSpec((1, FFN_TM, d), lambda i, j: (i, j, 0))
    return pl.pallas_call(
        functools.partial(_ffn_kernel, k0=k0),
        grid=(b, s // FFN_TM),
        in_specs=[
            tok,
            pl.BlockSpec((1, 9, d), lambda i, j: (i, 0, 0)),
            _resident((1, d)),
            _resident((d, 2 * D_FF), lead),
            _resident((D_FF, d), lead),
        ],
        out_specs=tok,
        out_shape=jax.ShapeDtypeStruct(x.shape, F32),
        scratch_shapes=[pltpu.VMEM((FFN_TM, D_FF), BF16)],
        compiler_params=_cparams(("arbitrary", "arbitrary")),
        name="ffn",
    )(x, mod_l, g.reshape(1, d), w_up, w_dn)


PROJ_TM = 1024


def _head_rmsnorm(acc, gain):
    rows = acc.shape[0]
    low = lax.broadcasted_iota(jnp.int32, (rows, LANES), 1) < HEAD_DIM
    outs = []
    for c in range(acc.shape[1] // LANES):
        xc = acc[:, c * LANES:(c + 1) * LANES]
        sq = xc * xc
        s_lo = jnp.sum(jnp.where(low, sq, 0.0), axis=-1, keepdims=True)
        s_hi = jnp.sum(jnp.where(low, 0.0, sq), axis=-1, keepdims=True)
        r_lo = lax.rsqrt(s_lo * (1.0 / HEAD_DIM) + EPS)
        r_hi = lax.rsqrt(s_hi * (1.0 / HEAD_DIM) + EPS)
        outs.append(xc * jnp.where(low, r_lo, r_hi) * gain[:, c * LANES:(c + 1) * LANES])
    return jnp.concatenate(outs, axis=-1)


def _proj_kernel(x_ref, mod_ref, g_ref, w_ref, gain_ref,
                 qa_ref, ka_ref, va_ref, qbt_ref, kb_ref, vbt_ref, u_ref):
    x = x_ref[0]
    h = _norm_mod(x, g_ref[...], mod_ref[0, 3:4, :], mod_ref[0, 4:5, :])
    hb = h.astype(BF16)
    w = MIX_W

    def col(j):
        return jnp.dot(hb, w_ref[:, j * w:(j + 1) * w], preferred_element_type=F32)

    q_scale = 1.0 / math.sqrt(HEAD_DIM)
    qa_ref[0] = (_head_rmsnorm(col(0), gain_ref[0:1, :]) * (q_scale * LOG2E)).astype(BF16)
    ka_ref[0] = _head_rmsnorm(col(1), gain_ref[1:2, :]).astype(BF16)
    va_ref[0] = col(2).astype(BF16)
    qbt_ref[0] = (_head_rmsnorm(col(3), gain_ref[2:3, :]) * (q_scale * LOG2E)).T.astype(BF16)
    kb_ref[0] = _head_rmsnorm(col(4), gain_ref[3:4, :]).astype(BF16)
    vbt_ref[0] = col(5).T.astype(BF16)
    u_ref[0] = col(6) * _sigmoid(col(7))


def _proj(x, mod_l, g, w_in, layer, gains):
    b, s, d = x.shape
    tok_in = pl.BlockSpec((1, PROJ_TM, d), lambda i, j: (i, j, 0))
    tok_out = pl.BlockSpec((1, PROJ_TM, MIX_W), lambda i, j: (i, j, 0))
    bf = jax.ShapeDtypeStruct((b, s, MIX_W), BF16)
    bf_t = jax.ShapeDtypeStruct((b, MIX_W, s), BF16)
    tr_out = pl.BlockSpec((1, MIX_W, PROJ_TM), lambda i, j: (i, 0, j))
    return pl.pallas_call(
        _proj_kernel,
        grid=(b, s // PROJ_TM),
        in_specs=[
            tok_in,
            pl.BlockSpec((1, 9, d), lambda i, j: (i, 0, 0)),
            _resident((1, d)),
            _resident(w_in.shape[1:], (layer,)),
            _resident(gains.shape),
        ],
        out_specs=[tok_out] * 3 + [tr_out, tok_out, tr_out, tok_out],
        out_shape=[bf, bf, bf, bf_t, bf, bf_t, jax.ShapeDtypeStruct((b, s, MIX_W), F32)],
        compiler_params=_cparams(("arbitrary", "arbitrary")),
        name="proj",
    )(x, mod_l, g.reshape(1, d), w_in, gains)


def _bucket_thresholds():
    max_exact = N_BUCKETS // 2
    d = np.arange(0, 2 * REL_MAX_DIST + 2)
    df = np.maximum(d.astype(np.float32), np.float32(1.0))
    large = max_exact + (np.log(df / np.float32(max_exact))
                         / np.float32(math.log(REL_MAX_DIST / max_exact))
                         * np.float32(N_BUCKETS - max_exact)).astype(np.int32)
    bucket = np.where(d < max_exact, d, np.minimum(large, N_BUCKETS - 1))
    return [int(np.argmax(bucket >= b)) for b in range(1, N_BUCKETS)]


_THRESHOLDS = _bucket_thresholds()


def _bias_of_dist(dist, tab_ref, head):
    val = jnp.full(dist.shape, tab_ref[0, head], F32)
    for b in range(1, N_BUCKETS):
        val = jnp.where(dist >= _THRESHOLDS[b - 1], tab_ref[b, head], val)
    return val


DIFF_T = 256


def _diff_bias_tiles(seq):
    first_const = -(-(_THRESHOLDS[-1] + DIFF_T - 1) // DIFF_T)
    return min(seq // DIFF_T, first_const + 1)


def _bias_diff_kernel(tab_ref, o_ref):
    h = pl.program_id(0)
    delta = pl.program_id(1)
    j = lax.broadcasted_iota(jnp.int32, (DIFF_T, DIFF_T), 0)
    i = lax.broadcasted_iota(jnp.int32, (DIFF_T, DIFF_T), 1)
    dist = jnp.maximum(delta * DIFF_T + i - j, 0)
    o_ref[0, 0] = _bias_of_dist(dist, tab_ref, HA + h) * LOG2E


def _bias_dil_kernel(tab_ref, o_ref):
    p = pl.program_id(0)
    hp = pl.program_id(1)
    dil = jnp.where(p == 0, DIL_PATTERNS[0][1],
                    jnp.where(p == 1, DIL_PATTERNS[1][1], DIL_PATTERNS[2][1]))
    j = lax.broadcasted_iota(jnp.int32, (2 * BLK, BLK), 0)
    i = lax.broadcasted_iota(jnp.int32, (2 * BLK, BLK), 1)
    rel = i + BLK - j
    band = (rel >= 0) & (rel <= BLK)
    dist = jnp.maximum(rel, 0) * dil
    for sub in range(2):
        bias = _bias_of_dist(dist, tab_ref, 2 * hp + sub) * LOG2E
        lanes = slice(sub * BLK, (sub + 1) * BLK)
        o_ref[0, 0, 0, :, lanes] = jnp.where(band & (j >= BLK), bias, NEG)
        o_ref[0, 1, 0, :, lanes] = jnp.where(band, bias, NEG)


def _bias_tiles(rel_bias, seq):
    smem = pl.BlockSpec(memory_space=pltpu.SMEM)
    n_delta = _diff_bias_tiles(seq)
    bias_b = pl.pallas_call(
        _bias_diff_kernel,
        grid=(HB, n_delta),
        in_specs=[smem],
        out_specs=pl.BlockSpec((1, 1, DIFF_T, DIFF_T), lambda h, t: (h, t, 0, 0)),
        out_shape=jax.ShapeDtypeStruct((HB, n_delta, DIFF_T, DIFF_T), F32),
        compiler_params=_cparams(("arbitrary", "arbitrary")),
        name="bias_diff",
    )(rel_bias)
    n_pat = len(DIL_PATTERNS)
    bias_a = pl.pallas_call(
        _bias_dil_kernel,
        grid=(n_pat, HA // 2),
        in_specs=[smem],
        out_specs=pl.BlockSpec((1, 2, 1, 2 * BLK, 2 * BLK), lambda p, h: (p, 0, h, 0, 0)),
        out_shape=jax.ShapeDtypeStruct((n_pat, 2, HA // 2, 2 * BLK, 2 * BLK), F32),
        compiler_params=_cparams(("arbitrary", "arbitrary")),
        name="bias_dil",
    )(rel_bias)
    return bias_a, bias_b


DIL_CHAINS = 4
DIL_STRIDE = 4
DIL_ONES = 16
DIL_COMBINE_ROWS = 512


def _dil_kernel(q_ref, k_ref, v_ref, bias_ref, y_ref,
                q32_ref, k32_ref, v32_ref, q4_ref, k4_ref, v4_ref, qd_ref, kd_ref, vtd_ref,
                ot_ref, lt_ref, on_ref, ln_ref, sa_ref, sb_ref):
    seq = q_ref.shape[1]
    pad = BLK
    quarter = seq // DIL_STRIDE
    nat = (q32_ref, k32_ref, v32_ref)
    by4 = (q4_ref, k4_ref, v4_ref)
    for src_ref, nat_ref, by4_ref in zip((q_ref, k_ref, v_ref), nat, by4):
        nat_ref[...] = src_ref[0].astype(F32)
        for rho in range(DIL_STRIDE):
            by4_ref[rho * quarter:(rho + 1) * quarter, :] = nat_ref[pl.ds(rho, quarter, stride=DIL_STRIDE), :]

    def residue(which, dil, r):
        if dil == 1:
            return nat[which][...]
        if dil == DIL_STRIDE:
            return by4[which][r * quarter:(r + 1) * quarter, :]
        a, rho = divmod(r, DIL_STRIDE)
        return by4[which][pl.ds(rho * quarter + a, seq // dil, stride=DIL_STRIDE), :]

    for p in range(len(DIL_PATTERNS)):
        kd_ref[p, 0:pad, :] = jnp.zeros((pad, LANES), BF16)
        vtd_ref[p, :, 0:pad] = jnp.zeros((LANES, pad), BF16)
    low = lax.broadcasted_iota(jnp.int32, (BLK, LANES), 1) < HEAD_DIM
    top = lax.broadcasted_iota(jnp.int32, (LANES, BLK), 0) < HEAD_DIM
    ones = jnp.ones((DIL_ONES, 2 * BLK), BF16)

    for p, (_, dil) in enumerate(DIL_PATTERNS):
        sub_len = seq // dil
        nb = sub_len // BLK
        for r in range(dil):
            qd_ref[p, r * sub_len:(r + 1) * sub_len, :] = residue(0, dil, r).astype(BF16)
            kd_ref[p, pad + r * sub_len:pad + (r + 1) * sub_len, :] = residue(1, dil, r).astype(BF16)
            vtd_ref[p, :, pad + r * sub_len:pad + (r + 1) * sub_len] = residue(2, dil, r).astype(BF16).T

        def produce(j, s_ref, p=p, nb=nb):
            for c in range(DIL_CHAINS):
                chain = j * DIL_CHAINS + c
                base = chain * BLK
                var = 0 if chain % nb == 0 else 1
                q = qd_ref[p, pl.ds(base, BLK), :]
                zero = jnp.zeros_like(q)
                qs = jnp.concatenate([jnp.where(low, q, zero), jnp.where(low, zero, q)], axis=0)
                kc = kd_ref[p, pl.ds(base, 2 * BLK), :]
                sc = lax.dot_general(kc, qs, (((1,), (1,)), ((), ())), preferred_element_type=F32)
                s_ref[c] = sc + bias_ref[p, var]

        def consume(j, s_ref, p=p):
            for c in range(DIL_CHAINS):
                base = (j * DIL_CHAINS + c) * BLK
                sc = s_ref[c]
                m = jnp.max(sc, axis=0, keepdims=True)
                pexp = jnp.exp2(sc - m).astype(BF16)
                vt = jnp.concatenate([vtd_ref[p, :, pl.ds(base, 2 * BLK)], ones], axis=0)
                pv = jnp.dot(vt, pexp, preferred_element_type=F32)
                den = pv[LANES:LANES + 1, :]
                o = pv[0:LANES, :] / den
                lse = m + jnp.log2(den)
                chain = j * DIL_CHAINS + c
                ot_ref[chain] = jnp.where(top, o[:, :BLK], o[:, BLK:])
                lt_ref[chain] = jnp.where(top, lse[:, :BLK], lse[:, BLK:])

        n_trips = seq // BLK // DIL_CHAINS
        bufs = (sa_ref, sb_ref)
        produce(0, bufs[0])
        for j in range(n_trips):
            if j + 1 < n_trips:
                produce(j + 1, bufs[(j + 1) % 2])
            consume(j, bufs[j % 2])
        for chain in range(seq // BLK):
            r, n = divmod(chain, nb)
            if dil <= DIL_STRIDE:
                rows = pl.ds(n * BLK * dil + r, BLK, stride=dil)
                on_ref[p, rows, :] = ot_ref[chain].T
                ln_ref[p, rows, :] = lt_ref[chain].T
            else:
                a, rho = divmod(r, DIL_STRIDE)
                rows = pl.ds(rho * quarter + n * BLK * DIL_STRIDE + a, BLK, stride=DIL_STRIDE)
                q32_ref[rows, :] = ot_ref[chain].T
                k32_ref[rows, :] = lt_ref[chain].T
        if dil > DIL_STRIDE:
            for rho in range(DIL_STRIDE):
                rows = pl.ds(rho, quarter, stride=DIL_STRIDE)
                on_ref[p, rows, :] = q32_ref[rho * quarter:(rho + 1) * quarter, :]
                ln_ref[p, rows, :] = k32_ref[rho * quarter:(rho + 1) * quarter, :]

    def combine(i, carry):
        rows = pl.ds(pl.multiple_of(i * DIL_COMBINE_ROWS, DIL_COMBINE_ROWS), DIL_COMBINE_ROWS)
        l1, l2, l3 = ln_ref[0, rows, :], ln_ref[1, rows, :], ln_ref[2, rows, :]
        mx = jnp.maximum(jnp.maximum(l1, l2), l3)
        e1 = jnp.exp2(l1 - mx)
        e2 = jnp.exp2(l2 - mx)
        e3 = jnp.exp2(l3 - mx)
        num = e1 * on_ref[0, rows, :] + e2 * on_ref[1, rows, :] + e3 * on_ref[2, rows, :]
        y_ref[0, rows, :] = (num / (e1 + e2 + e3)).astype(y_ref.dtype)
        return carry

    lax.fori_loop(0, seq // DIL_COMBINE_ROWS, combine, 0)


def _dilated_mixer(qa, ka, va, bias_a):
    b, s, w = qa.shape
    n_pat = len(DIL_PATTERNS)
    dils = [d for _, d in DIL_PATTERNS]
    assert all(d in (1, DIL_STRIDE, DIL_STRIDE ** 2) for d in dils) and dils[-1] == max(dils)
    pair = pl.BlockSpec((1, s, LANES), lambda i, h: (i, 0, h))
    return pl.pallas_call(
        _dil_kernel,
        grid=(b, HA // 2),
        in_specs=[pair, pair, pair,
                  pl.BlockSpec((n_pat, 2, None, 2 * BLK, 2 * BLK), lambda i, h: (0, 0, h, 0, 0))],
        out_specs=pair,
        out_shape=jax.ShapeDtypeStruct((b, s, w), BF16),
        scratch_shapes=[
            pltpu.VMEM((s, LANES), F32), pltpu.VMEM((s, LANES), F32), pltpu.VMEM((s, LANES), F32),
            pltpu.VMEM((s, LANES), F32), pltpu.VMEM((s, LANES), F32), pltpu.VMEM((s, LANES), F32),
            pltpu.VMEM((n_pat, s, LANES), BF16), pltpu.VMEM((n_pat, BLK + s, LANES), BF16),
            pltpu.VMEM((n_pat, LANES, BLK + s), BF16),
            pltpu.VMEM((s // BLK, LANES, BLK), F32), pltpu.VMEM((s // BLK, LANES, BLK), F32),
            pltpu.VMEM((n_pat, s, LANES), F32), pltpu.VMEM((n_pat, s, LANES), F32),
            pltpu.VMEM((DIL_CHAINS, 2 * BLK, 2 * BLK), F32),
            pltpu.VMEM((DIL_CHAINS, 2 * BLK, 2 * BLK), F32),
        ],
        compiler_params=_cparams(("arbitrary", "arbitrary")),
        name="dilated",
    )(qa, ka, va, bias_a)


DIFF_TQ = 2 * DIFF_T
DIFF_HEADS = 2
DIFF_ONES = 16


def _diff_kernel(qt_ref, k_ref, vt_ref, bias_ref, lam_ref, sg_ref, o_ref,
                 qs_ref, m_ref, acc_ref, s0_ref, s1_ref, *, lam_init, n_bias):
    qi = pl.program_id(2)
    t = DIFF_T
    dv = 2 * HEAD_DIM
    n_blk = 2 * DIFF_TQ // t
    width = 2 * DIFF_TQ
    first = lax.broadcasted_iota(jnp.int32, (LANES, DIFF_TQ), 0) < HEAD_DIM
    for hd in range(DIFF_HEADS):
        q = qt_ref[0, hd * LANES:(hd + 1) * LANES, :]
        zero = jnp.zeros_like(q)
        qs_ref[:, hd * width:hd * width + DIFF_TQ] = jnp.where(first, q, zero)
        qs_ref[:, hd * width + DIFF_TQ:(hd + 1) * width] = jnp.where(first, zero, q)
    m_ref[...] = jnp.full(m_ref.shape, NEG, F32)
    acc_ref[...] = jnp.zeros(acc_ref.shape, F32)
    ones = jnp.ones((DIFF_ONES, t), BF16)

    def produce(kt, s_ref, blocks):
        off = pl.multiple_of(kt * t, t)
        for hd in range(DIFF_HEADS):
            k = k_ref[0, pl.ds(off, t), hd * LANES:(hd + 1) * LANES]
            for blk in blocks:
                lanes = slice(hd * width + blk * t, hd * width + (blk + 1) * t)
                s = jnp.dot(k, qs_ref[:, lanes], preferred_element_type=F32)
                delta = 2 * qi + blk % 2 - kt
                s_ref[hd * n_blk + blk] = s + bias_ref[hd, jnp.minimum(delta, n_bias - 1)]

    def consume(kt, s_ref, modes):
        off = pl.multiple_of(kt * t, t)
        for hd in range(DIFF_HEADS):
            vt = jnp.concatenate([vt_ref[0, hd * dv:(hd + 1) * dv, pl.ds(off, t)], ones], axis=0)
            for blk, mode in enumerate(modes):
                if mode == "skip":
                    continue
                g = hd * n_blk + blk
                lanes = slice(g * t, (g + 1) * t)
                s = s_ref[g]
                if mode == "diag":
                    key = lax.broadcasted_iota(jnp.int32, (t, t), 0)
                    qry = lax.broadcasted_iota(jnp.int32, (t, t), 1)
                    s = jnp.where(qry >= key, s, NEG)
                m_old = m_ref[:, lanes]
                m_new = jnp.maximum(m_old, jnp.max(s, axis=0, keepdims=True))
                alpha = jnp.exp2(m_old - m_new)
                p = jnp.exp2(s - m_new).astype(BF16)
                pv = jnp.dot(vt, p, preferred_element_type=F32)
                acc_ref[g] = alpha * acc_ref[g] + pv
                m_ref[:, lanes] = m_new

    every = tuple(range(n_blk))
    full = ("full",) * n_blk
    produce(0, s0_ref, every)

    def body(j, carry):
        kt = 2 * j
        produce(kt + 1, s1_ref, every)
        consume(kt, s0_ref, full)
        produce(kt + 2, s0_ref, every)
        consume(kt + 1, s1_ref, full)
        return carry

    def body_twice(j, carry):
        return body(2 * j + 1, body(2 * j, carry))

    pairs = lax.shift_right_logical(qi, 1)
    lax.fori_loop(0, pairs, body_twice, 0)
    lax.fori_loop(2 * pairs, qi, body, 0)
    produce(2 * qi + 1, s1_ref, (1, 3))
    consume(2 * qi, s0_ref, ("diag", "full", "diag", "full"))
    consume(2 * qi + 1, s1_ref, ("skip", "diag", "skip", "diag"))

    lv = lam_ref[...]
    s01 = jnp.sum(lv[0:1] * lv[1:2], axis=-1, keepdims=True)
    s23 = jnp.sum(lv[2:3] * lv[3:4], axis=-1, keepdims=True)
    lam = jnp.exp(s01) - jnp.exp(s23) + lam_init
    for hd in range(DIFF_HEADS):
        o_t = [acc_ref[hd * n_blk + blk, 0:dv, :] / acc_ref[hd * n_blk + blk, dv:dv + 1, :]
               for blk in range(n_blk)]
        half = n_blk // 2
        ob = jnp.concatenate([o_t[blk] - lam * o_t[half + blk] for blk in range(half)],
                             axis=1)
        ms = jnp.mean(ob * ob, axis=0, keepdims=True)
        y = ob * lax.rsqrt(ms + EPS) * sg_ref[...]
        o_ref[0, :, hd * LANES:(hd + 1) * LANES] = (y * (1.0 - lam_init)).T.astype(o_ref.dtype)


def _diff_attention(qbt, kb, vbt, bias_b, lambda_vec, subln_g, layer):
    b, s, w = kb.shape
    t = DIFF_T
    dv = 2 * HEAD_DIM
    n_bias = bias_b.shape[1]
    lam_init = 0.8 - 0.6 * math.exp(-0.3 * layer)
    heads = DIFF_HEADS
    n_lane_blocks = heads * 2 * DIFF_TQ // t
    tile = pl.BlockSpec((1, DIFF_TQ, heads * LANES), lambda i, h, n: (i, n, h))
    return pl.pallas_call(
        functools.partial(_diff_kernel, lam_init=lam_init, n_bias=n_bias),
        grid=(b, HB // heads, s // DIFF_TQ),
        in_specs=[
            pl.BlockSpec((1, heads * LANES, DIFF_TQ), lambda i, h, n: (i, h, n)),
            pl.BlockSpec((1, s, heads * LANES), lambda i, h, n: (i, 0, h)),
            pl.BlockSpec((1, heads * dv, s), lambda i, h, n: (i, h, 0)),
            pl.BlockSpec((heads, n_bias, t, t), lambda i, h, n: (h, 0, 0, 0)),
            pl.BlockSpec(lambda_vec.shape, lambda i, h, n: (0, 0)),
            pl.BlockSpec((dv, 1), lambda i, h, n: (0, 0)),
        ],
        out_specs=tile,
        out_shape=jax.ShapeDtypeStruct((b, s, w), BF16),
        scratch_shapes=[pltpu.VMEM((LANES, heads * 2 * DIFF_TQ), BF16),
                        pltpu.VMEM((1, heads * 2 * DIFF_TQ), F32),
                        pltpu.VMEM((n_lane_blocks, dv + DIFF_ONES, t), F32),
                        pltpu.VMEM((n_lane_blocks, t, t), F32),
                        pltpu.VMEM((n_lane_blocks, t, t), F32)],
        compiler_params=_cparams(("arbitrary", "arbitrary", "arbitrary")),
        name="diff_attn",
    )(qbt, kb, vbt, bias_b, lambda_vec, subln_g.reshape(dv, 1))


MERGE_TM = 512
CONV_HALO = 32
CONV_ROWS = 128


def _conv_tile(prev_ref, cur_ref, w_ref, cb_ref, lg_ref, lb_ref, sh_ref, y_ref, first_tile):
    halo = prev_ref[0]
    rows_in = cur_ref.shape[1]
    sh_ref[0, 0:CONV_HALO, :] = jnp.where(first_tile, jnp.zeros_like(halo), halo)
    sh_ref[0, CONV_HALO:, :] = cur_ref[0]
    for b in range(1, SUBLANES):
        sh_ref[b] = pltpu.roll(sh_ref[0], b, axis=0)
    for r0 in range(0, rows_in, CONV_ROWS):
        acc = jnp.zeros((CONV_ROWS, C_CONV), F32)
        for tap in range(CONV_K):
            back = CONV_K - 1 - tap
            a, b = divmod(back, SUBLANES)
            lo = r0 + CONV_HALO - a * SUBLANES
            acc = acc + sh_ref[b, lo:lo + CONV_ROWS, :] * w_ref[tap:tap + 1, :]
        y_ref[r0:r0 + CONV_ROWS, :] = acc
    y = y_ref[...] + cb_ref[...]
    mu = jnp.mean(y, axis=-1, keepdims=True)
    var = jnp.mean(jnp.square(y - mu), axis=-1, keepdims=True)
    z = (y - mu) * lax.rsqrt(var + EPS) * lg_ref[...] + lb_ref[...]
    return z * _sigmoid(z)


def _merge_kernel(x_ref, mod_ref, g_ref, ya_ref, yb_ref, uprev_ref, u_ref,
                  cw_ref, cb_ref, lg_ref, lb_ref, wg_ref, bg_ref, wb_ref, wo_ref, o_ref,
                  sh_ref, yc_ref):
    x = x_ref[0]
    d = x.shape[-1]
    h = _norm_mod(x, g_ref[...], mod_ref[0, 3:4, :], mod_ref[0, 4:5, :])
    hb = h.astype(BF16)
    y_c = _conv_tile(uprev_ref, u_ref, cw_ref, cb_ref, lg_ref, lb_ref, sh_ref, yc_ref,
                     pl.program_id(1) == 0).astype(BF16)
    mixed = jnp.zeros(x.shape, F32)
    for i, y in enumerate((ya_ref[0], yb_ref[0], y_c)):
        z = jnp.dot(hb, wg_ref[:, i * d:(i + 1) * d], preferred_element_type=F32)
        gate = _sigmoid(z + bg_ref[:, i * d:(i + 1) * d])
        mixed = mixed + gate * jnp.dot(y, wb_ref[i], preferred_element_type=F32)
    y = jnp.dot(mixed.astype(BF16), wo_ref[...], preferred_element_type=F32)
    o_ref[0] = x + mod_ref[0, 5:6, :] * y


def _merge(x, mod_l, g, y_a, y_b, u, conv_w, conv_b, ln_g, ln_b, layer,
           w_gate, b_gate, w_branch, w_out):
    b, s, d = x.shape
    c = u.shape[-1]
    per_tile = MERGE_TM // CONV_HALO
    tok = pl.BlockSpec((1, MERGE_TM, d), lambda i, j: (i, j, 0))
    br = pl.BlockSpec((1, MERGE_TM, MIX_W), lambda i, j: (i, j, 0))
    return pl.pallas_call(
        _merge_kernel,
        grid=(b, s // MERGE_TM),
        in_specs=[
            tok,
            pl.BlockSpec((1, 9, d), lambda i, j: (i, 0, 0)),
            _resident((1, d)),
            br, br,
            pl.BlockSpec((1, CONV_HALO, c), lambda i, j: (i, jnp.maximum(j * per_tile - 1, 0), 0)),
            br,
            _resident((CONV_K, c)), _resident((1, c)), _resident((1, c)), _resident((1, c)),
            _resident(w_gate.shape[1:], (layer,)),
            _resident((1, N_BRANCH * d)),
            _resident(w_branch.shape[1:], (layer,)),
            _resident(w_out.shape[1:], (layer,)),
        ],
        out_specs=tok,
        out_shape=jax.ShapeDtypeStruct(x.shape, F32),
        scratch_shapes=[pltpu.VMEM((SUBLANES, CONV_HALO + MERGE_TM, c), F32),
                        pltpu.VMEM((MERGE_TM, c), F32)],
        compiler_params=_cparams(("arbitrary", "arbitrary")),
        name="merge",
    )(x, mod_l, g.reshape(1, d), y_a, y_b, u, u, conv_w, conv_b.reshape(1, c),
      ln_g.reshape(1, c), ln_b.reshape(1, c), w_gate, b_gate.reshape(1, N_BRANCH * d),
      w_branch, w_out)


def _qk_gain_rows(qk_gain):
    ga_q = jnp.tile(qk_gain[0], HA)
    ga_k = jnp.tile(qk_gain[1], HA)
    gb_q = jnp.tile(jnp.concatenate([qk_gain[2], qk_gain[3]]), HB)
    gb_k = jnp.tile(jnp.concatenate([qk_gain[4], qk_gain[5]]), HB)
    return jnp.stack([ga_q, ga_k, gb_q, gb_k])


def kernel(x, c, rel_bias, w_ada, b_ada, norm_g, w_ffn_in, w_ffn_out, w_in, qk_gain, lambda_vec,
           subln_g, conv_w, conv_b, conv_ln_g, conv_ln_b, w_branch, w_gate, b_gate, w_out):
    b, s, d = x.shape
    mod = _ada_mod(c, w_ada, b_ada)
    bias_a, bias_b = _bias_tiles(rel_bias, s)
    w_ffn_in, w_ffn_out, w_in = w_ffn_in.astype(BF16), w_ffn_out.astype(BF16), w_in.astype(BF16)
    w_gate, w_branch, w_out = w_gate.astype(BF16), w_branch.astype(BF16), w_out.astype(BF16)
    for l in range(DEPTH):
        mod_l = mod[l]
        x = _ffn(x, mod_l, norm_g[l, 0], w_ffn_in, w_ffn_out, (l, 0), 0)
        qa, ka, va, qbt, kb, vbt, u = _proj(x, mod_l, norm_g[l, 1], w_in, l,
                                          _qk_gain_rows(qk_gain[l]))
        y_a = _dilated_mixer(qa, ka, va, bias_a)
        y_b = _diff_attention(qbt, kb, vbt, bias_b, lambda_vec[l], subln_g[l], l)
        x = _merge(x, mod_l, norm_g[l, 1], y_a, y_b, u, conv_w[l], conv_b[l], conv_ln_g[l],
                   conv_ln_b[l], l, w_gate, b_gate[l], w_branch, w_out)
        x = _ffn(x, mod_l, norm_g[l, 2], w_ffn_in, w_ffn_out, (l, 1), 6)
    return x
```

```python
import functools
import math

import numpy as np
import jax
import jax.numpy as jnp
from jax import lax
from jax.experimental import pallas as pl
from jax.experimental.pallas import tpu as pltpu

D_MODEL = 1024
DEPTH = 2
HEAD_DIM = 64
HA = 8
DIL_PATTERNS = ((128, 1), (512, 4), (2048, 16))
HB = 4
C_CONV = 512
CONV_K = 31
D_FF = 2816
N_BUCKETS = 32
REL_MAX_DIST = 2048
BLK = 128
MIX_W = 512
N_BRANCH = 3
EPS = 1e-6
NEG = -1e30
LOG2E = math.log2(math.e)

LANES = 128
SUBLANES = 8
VMEM_LIMIT = 56 * 1024 * 1024

F32 = jnp.float32
BF16 = jnp.bfloat16


def _cparams(sem):
    return pltpu.CompilerParams(dimension_semantics=sem, vmem_limit_bytes=VMEM_LIMIT)


def _sigmoid(x):
    return 1.0 / (1.0 + jnp.exp(-x))


def _resident(shape, lead=()):
    lead = tuple(lead)
    zeros = (0,) * len(shape)
    return pl.BlockSpec((None,) * len(lead) + tuple(shape), lambda *_: lead + zeros,
                        pipeline_mode=pl.Buffered(1))


def _norm_mod(x, g, shift, scale):
    ms = jnp.mean(x * x, axis=-1, keepdims=True)
    y = x * lax.rsqrt(ms + EPS) * g
    return y * (1.0 + scale) + shift


ADA_TN = 2304


def _ada_kernel(c_ref, w_ref, b_ref, o_ref):
    c = c_ref[...]
    a = c * _sigmoid(c)
    o_ref[0] = jnp.dot(a, w_ref[0], preferred_element_type=F32,
                       precision=lax.Precision.HIGHEST) + b_ref[0]


def _ada_mod(c, w_ada, b_ada):
    b, d = c.shape
    rows = 8
    c_pad = jnp.pad(c, ((0, rows - b), (0, 0)))
    n = w_ada.shape[-1]
    out = pl.pallas_call(
        _ada_kernel,
        grid=(DEPTH, n // ADA_TN),
        in_specs=[
            pl.BlockSpec((rows, d), lambda l, j: (0, 0)),
            pl.BlockSpec((1, d, ADA_TN), lambda l, j: (l, 0, j)),
            pl.BlockSpec((1, 1, ADA_TN), lambda l, j: (l, 0, j)),
        ],
        out_specs=pl.BlockSpec((1, rows, ADA_TN), lambda l, j: (l, 0, j)),
        out_shape=jax.ShapeDtypeStruct((DEPTH, rows, n), F32),
        compiler_params=_cparams(("arbitrary", "arbitrary")),
        name="ada_mod",
    )(c_pad, w_ada, b_ada.reshape(DEPTH, 1, n))
    return out[:, :b].reshape(DEPTH, b, 9, d)


FFN_TM = 1024
FFN_TF = 256
assert D_FF % FFN_TF == 0


def _ffn_kernel(x_ref, mod_ref, g_ref, wup_ref, wdn_ref, o_ref, act_ref, *, k0):
    x = x_ref[0]
    h = _norm_mod(x, g_ref[...], mod_ref[0, k0:k0 + 1, :], mod_ref[0, k0 + 1:k0 + 2, :])
    hb = h.astype(BF16)
    for j in range(D_FF // FFN_TF):
        lo = j * FFN_TF
        gate = jnp.dot(hb, wup_ref[:, lo:lo + FFN_TF], preferred_element_type=F32)
        up = jnp.dot(hb, wup_ref[:, D_FF + lo:D_FF + lo + FFN_TF], preferred_element_type=F32)
        act_ref[:, lo:lo + FFN_TF] = (gate * _sigmoid(gate) * up).astype(BF16)
    y = jnp.dot(act_ref[...], wdn_ref[...], preferred_element_type=F32)
    o_ref[0] = x + (0.5 * mod_ref[0, k0 + 2:k0 + 3, :]) * y


def _ffn(x, mod_l, g, w_up, w_dn, lead, k0):
    b, s, d = x.shape
    tok = pl.BlockSpec((1, FFN_TM, d), lambda i, j: (i, j, 0))
    return pl.pallas_call(
        functools.partial(_ffn_kernel, k0=k0),
        grid=(b, s // FFN_TM),
        in_specs=[
            tok,
            pl.BlockSpec((1, 9, d), lambda i, j: (i, 0, 0)),
            _resident((1, d)),
            _resident((d, 2 * D_FF), lead),
            _resident((D_FF, d), lead),
        ],
        out_specs=tok,
        out_shape=jax.ShapeDtypeStruct(x.shape, F32),
        scratch_shapes=[pltpu.VMEM((FFN_TM, D_FF), BF16)],
        compiler_params=_cparams(("arbitrary", "arbitrary")),
        name="ffn",
    )(x, mod_l, g.reshape(1, d), w_up, w_dn)


PROJ_TM = 1024


def _head_rmsnorm(acc, gain):
    rows = acc.shape[0]
    low = lax.broadcasted_iota(jnp.int32, (rows, LANES), 1) < HEAD_DIM
    outs = []
    for c in range(acc.shape[1] // LANES):
        xc = acc[:, c * LANES:(c + 1) * LANES]
        sq = xc * xc
        s_lo = jnp.sum(jnp.where(low, sq, 0.0), axis=-1, keepdims=True)
        s_hi = jnp.sum(jnp.where(low, 0.0, sq), axis=-1, keepdims=True)
        r_lo = lax.rsqrt(s_lo * (1.0 / HEAD_DIM) + EPS)
        r_hi = lax.rsqrt(s_hi * (1.0 / HEAD_DIM) + EPS)
        outs.append(xc * jnp.where(low, r_lo, r_hi) * gain[:, c * LANES:(c + 1) * LANES])
    return jnp.concatenate(outs, axis=-1)


def _proj_kernel(x_ref, mod_ref, g_ref, w_ref, gain_ref,
                 qa_ref, ka_ref, va_ref, qbt_ref, kb_ref, vbt_ref, u_ref):
    x = x_ref[0]
    h = _norm_mod(x, g_ref[...], mod_ref[0, 3:4, :], mod_ref[0, 4:5, :])
    hb = h.astype(BF16)
    w = MIX_W

    def col(j):
        return jnp.dot(hb, w_ref[:, j * w:(j + 1) * w], preferred_element_type=F32)

    q_scale = 1.0 / math.sqrt(HEAD_DIM)
    qa_ref[0] = (_head_rmsnorm(col(0), gain_ref[0:1, :]) * (q_scale * LOG2E)).astype(BF16)
    ka_ref[0] = _head_rmsnorm(col(1), gain_ref[1:2, :]).astype(BF16)
    va_ref[0] = col(2).astype(BF16)
    qbt_ref[0] = (_head_rmsnorm(col(3), gain_ref[2:3, :]) * (q_scale * LOG2E)).T.astype(BF16)
    kb_ref[0] = _head_rmsnorm(col(4), gain_ref[3:4, :]).astype(BF16)
    vbt_ref[0] = col(5).T.astype(BF16)
    u_ref[0] = col(6) * _sigmoid(col(7))


def _proj(x, mod_l, g, w_in, layer, gains):
    b, s, d = x.shape
    tok_in = pl.BlockSpec((1, PROJ_TM, d), lambda i, j: (i, j, 0))
    tok_out = pl.BlockSpec((1, PROJ_TM, MIX_W), lambda i, j: (i, j, 0))
    bf = jax.ShapeDtypeStruct((b, s, MIX_W), BF16)
    bf_t = jax.ShapeDtypeStruct((b, MIX_W, s), BF16)
    tr_out = pl.BlockSpec((1, MIX_W, PROJ_TM), lambda i, j: (i, 0, j))
    return pl.pallas_call(
        _proj_kernel,
        grid=(b, s // PROJ_TM),
        in_specs=[
            tok_in,
            pl.BlockSpec((1, 9, d), lambda i, j: (i, 0, 0)),
            _resident((1, d)),
            _resident(w_in.shape[1:], (layer,)),
            _resident(gains.shape),
        ],
        out_specs=[tok_out] * 3 + [tr_out, tok_out, tr_out, tok_out],
        out_shape=[bf, bf, bf, bf_t, bf, bf_t, jax.ShapeDtypeStruct((b, s, MIX_W), F32)],
        compiler_params=_cparams(("arbitrary", "arbitrary")),
        name="proj",
    )(x, mod_l, g.reshape(1, d), w_in, gains)


def _bucket_thresholds():
    max_exact = N_BUCKETS // 2
    d = np.arange(0, 2 * REL_MAX_DIST + 2)
    df = np.maximum(d.astype(np.float32), np.float32(1.0))
    large = max_exact + (np.log(df / np.float32(max_exact))
                         / np.float32(math.log(REL_MAX_DIST / max_exact))
                         * np.float32(N_BUCKETS - max_exact)).astype(np.int32)
    bucket = np.where(d < max_exact, d, np.minimum(large, N_BUCKETS - 1))
    return [int(np.argmax(bucket >= b)) for b in range(1, N_BUCKETS)]


_THRESHOLDS = _bucket_thresholds()


def _bias_of_dist(dist, tab_ref, head):
    val = jnp.full(dist.shape, tab_ref[0, head], F32)
    for b in range(1, N_BUCKETS):
        val = jnp.where(dist >= _THRESHOLDS[b - 1], tab_ref[b, head], val)
    return val


DIFF_T = 256


def _diff_bias_tiles(seq):
    first_const = -(-(_THRESHOLDS[-1] + DIFF_T - 1) // DIFF_T)
    return min(seq // DIFF_T, first_const + 1)


def _bias_diff_kernel(tab_ref, o_ref):
    h = pl.program_id(0)
    delta = pl.program_id(1)
    j = lax.broadcasted_iota(jnp.int32, (DIFF_T, DIFF_T), 0)
    i = lax.broadcasted_iota(jnp.int32, (DIFF_T, DIFF_T), 1)
    dist = jnp.maximum(delta * DIFF_T + i - j, 0)
    o_ref[0, 0] = _bias_of_dist(dist, tab_ref, HA + h) * LOG2E


def _bias_dil_kernel(tab_ref, o_ref):
    p = pl.program_id(0)
    hp = pl.program_id(1)
    dil = jnp.where(p == 0, DIL_PATTERNS[0][1],
                    jnp.where(p == 1, DIL_PATTERNS[1][1], DIL_PATTERNS[2][1]))
    j = lax.broadcasted_iota(jnp.int32, (2 * BLK, BLK), 0)
    i = lax.broadcasted_iota(jnp.int32, (2 * BLK, BLK), 1)
    rel = i + BLK - j
    band = (rel >= 0) & (rel <= BLK)
    dist = jnp.maximum(rel, 0) * dil
    for sub in range(2):
        bias = _bias_of_dist(dist, tab_ref, 2 * hp + sub) * LOG2E
        lanes = slice(sub * BLK, (sub + 1) * BLK)
        o_ref[0, 0, 0, :, lanes] = jnp.where(band & (j >= BLK), bias, NEG)
        o_ref[0, 1, 0, :, lanes] = jnp.where(band, bias, NEG)


def _bias_tiles(rel_bias, seq):
    smem = pl.BlockSpec(memory_space=pltpu.SMEM)
    n_delta = _diff_bias_tiles(seq)
    bias_b = pl.pallas_call(
        _bias_diff_kernel,
        grid=(HB, n_delta),
        in_specs=[smem],
        out_specs=pl.BlockSpec((1, 1, DIFF_T, DIFF_T), lambda h, t: (h, t, 0, 0)),
        out_shape=jax.ShapeDtypeStruct((HB, n_delta, DIFF_T, DIFF_T), F32),
        compiler_params=_cparams(("arbitrary", "arbitrary")),
        name="bias_diff",
    )(rel_bias)
    n_pat = len(DIL_PATTERNS)
    bias_a = pl.pallas_call(
        _bias_dil_kernel,
        grid=(n_pat, HA // 2),
        in_specs=[smem],
        out_specs=pl.BlockSpec((1, 2, 1, 2 * BLK, 2 * BLK), lambda p, h: (p, 0, h, 0, 0)),
        out_shape=jax.ShapeDtypeStruct((n_pat, 2, HA // 2, 2 * BLK, 2 * BLK), F32),
        compiler_params=_cparams(("arbitrary", "arbitrary")),
        name="bias_dil",
    )(rel_bias)
    return bias_a, bias_b


DIL_CHAINS = 8
DIL_STRIDE = 4
DIL_ONES = 16
DIL_COMBINE_ROWS = 512


def _dil_kernel(q_ref, k_ref, v_ref, bias_ref, y_ref,
                q32_ref, k32_ref, v32_ref, q4_ref, k4_ref, v4_ref, qd_ref, kd_ref, vtd_ref,
                ot_ref, lt_ref, on_ref, ln_ref, sa_ref, sb_ref):
    seq = q_ref.shape[1]
    pad = BLK
    quarter = seq // DIL_STRIDE
    nat = (q32_ref, k32_ref, v32_ref)
    by4 = (q4_ref, k4_ref, v4_ref)
    for src_ref, nat_ref, by4_ref in zip((q_ref, k_ref, v_ref), nat, by4):
        nat_ref[...] = src_ref[0].astype(F32)
        for rho in range(DIL_STRIDE):
            by4_ref[rho * quarter:(rho + 1) * quarter, :] = nat_ref[pl.ds(rho, quarter, stride=DIL_STRIDE), :]

    def residue(which, dil, r):
        if dil == 1:
            return nat[which][...]
        if dil == DIL_STRIDE:
            return by4[which][r * quarter:(r + 1) * quarter, :]
        a, rho = divmod(r, DIL_STRIDE)
        return by4[which][pl.ds(rho * quarter + a, seq // dil, stride=DIL_STRIDE), :]

    for p in range(len(DIL_PATTERNS)):
        kd_ref[p, 0:pad, :] = jnp.zeros((pad, LANES), BF16)
        vtd_ref[p, :, 0:pad] = jnp.zeros((LANES, pad), BF16)
    low = lax.broadcasted_iota(jnp.int32, (BLK, LANES), 1) < HEAD_DIM
    top = lax.broadcasted_iota(jnp.int32, (LANES, BLK), 0) < HEAD_DIM
    ones = jnp.ones((DIL_ONES, 2 * BLK), BF16)

    for p, (_, dil) in enumerate(DIL_PATTERNS):
        sub_len = seq // dil
        nb = sub_len // BLK
        for r in range(dil):
            qd_ref[p, r * sub_len:(r + 1) * sub_len, :] = residue(0, dil, r).astype(BF16)
            kd_ref[p, pad + r * sub_len:pad + (r + 1) * sub_len, :] = residue(1, dil, r).astype(BF16)
            vtd_ref[p, :, pad + r * sub_len:pad + (r + 1) * sub_len] = residue(2, dil, r).astype(BF16).T

        def produce(j, s_ref, p=p, nb=nb):
            for c in range(DIL_CHAINS):
                chain = j * DIL_CHAINS + c
                base = pl.multiple_of(chain * BLK, BLK)
                var = jnp.where((chain & (nb - 1)) == 0, 0, 1)
                q = qd_ref[p, pl.ds(base, BLK), :]
                zero = jnp.zeros_like(q)
                qs = jnp.concatenate([jnp.where(low, q, zero), jnp.where(low, zero, q)], axis=0)
                kc = kd_ref[p, pl.ds(base, 2 * BLK), :]
                sc = lax.dot_general(kc, qs, (((1,), (1,)), ((), ())), preferred_element_type=F32)
                s_ref[c] = sc + bias_ref[p, var]

        def consume(j, s_ref, p=p):
            for c in range(DIL_CHAINS):
                base = pl.multiple_of((j * DIL_CHAINS + c) * BLK, BLK)
                sc = s_ref[c]
                m = jnp.max(sc, axis=0, keepdims=True)
                pexp = jnp.exp2(sc - m).astype(BF16)
                vt = jnp.concatenate([vtd_ref[p, :, pl.ds(base, 2 * BLK)], ones], axis=0)
                pv = jnp.dot(vt, pexp, preferred_element_type=F32)
                den = pv[LANES:LANES + 1, :]
                o = pv[0:LANES, :] / den
                lse = m + jnp.log2(den)
                chain = j * DIL_CHAINS + c
                ot_ref[chain] = jnp.where(top, o[:, :BLK], o[:, BLK:])
                lt_ref[chain] = jnp.where(top, lse[:, :BLK], lse[:, BLK:])

        n_trips = seq // BLK // DIL_CHAINS
        produce(0, sa_ref)

        def two_trips(i, carry, produce=produce, consume=consume):
            j = 2 * i
            produce(j + 1, sb_ref)
            consume(j, sa_ref)
            produce(j + 2, sa_ref)
            consume(j + 1, sb_ref)
            return carry

        lax.fori_loop(0, n_trips // 2 - 1, two_trips, 0)
        produce(n_trips - 1, sb_ref)
        consume(n_trips - 2, sa_ref)
        consume(n_trips - 1, sb_ref)
        for chain in range(seq // BLK):
            r, n = divmod(chain, nb)
            if dil <= DIL_STRIDE:
                rows = pl.ds(n * BLK * dil + r, BLK, stride=dil)
                on_ref[p, rows, :] = ot_ref[chain].T
                ln_ref[p, rows, :] = lt_ref[chain].T
            else:
                a, rho = divmod(r, DIL_STRIDE)
                rows = pl.ds(rho * quarter + n * BLK * DIL_STRIDE + a, BLK, stride=DIL_STRIDE)
                q32_ref[rows, :] = ot_ref[chain].T
                k32_ref[rows, :] = lt_ref[chain].T
        if dil > DIL_STRIDE:
            for rho in range(DIL_STRIDE):
                rows = pl.ds(rho, quarter, stride=DIL_STRIDE)
                on_ref[p, rows, :] = q32_ref[rho * quarter:(rho + 1) * quarter, :]
                ln_ref[p, rows, :] = k32_ref[rho * quarter:(rho + 1) * quarter, :]

    def combine(i, carry):
        rows = pl.ds(pl.multiple_of(i * DIL_COMBINE_ROWS, DIL_COMBINE_ROWS), DIL_COMBINE_ROWS)
        l1, l2, l3 = ln_ref[0, rows, :], ln_ref[1, rows, :], ln_ref[2, rows, :]
        mx = jnp.maximum(jnp.maximum(l1, l2), l3)
        e1 = jnp.exp2(l1 - mx)
        e2 = jnp.exp2(l2 - mx)
        e3 = jnp.exp2(l3 - mx)
        num = e1 * on_ref[0, rows, :] + e2 * on_ref[1, rows, :] + e3 * on_ref[2, rows, :]
        y_ref[0, rows, :] = (num / (e1 + e2 + e3)).astype(y_ref.dtype)
        return carry

    lax.fori_loop(0, seq // DIL_COMBINE_ROWS, combine, 0)


def _dilated_mixer(qa, ka, va, bias_a):
    b, s, w = qa.shape
    n_pat = len(DIL_PATTERNS)
    dils = [d for _, d in DIL_PATTERNS]
    assert all(d in (1, DIL_STRIDE, DIL_STRIDE ** 2) for d in dils) and dils[-1] == max(dils)
    pair = pl.BlockSpec((1, s, LANES), lambda i, h: (i, 0, h))
    return pl.pallas_call(
        _dil_kernel,
        grid=(b, HA // 2),
        in_specs=[pair, pair, pair,
                  pl.BlockSpec((n_pat, 2, None, 2 * BLK, 2 * BLK), lambda i, h: (0, 0, h, 0, 0))],
        out_specs=pair,
        out_shape=jax.ShapeDtypeStruct((b, s, w), BF16),
        scratch_shapes=[
            pltpu.VMEM((s, LANES), F32), pltpu.VMEM((s, LANES), F32), pltpu.VMEM((s, LANES), F32),
            pltpu.VMEM((s, LANES), F32), pltpu.VMEM((s, LANES), F32), pltpu.VMEM((s, LANES), F32),
            pltpu.VMEM((n_pat, s, LANES), BF16), pltpu.VMEM((n_pat, BLK + s, LANES), BF16),
            pltpu.VMEM((n_pat, LANES, BLK + s), BF16),
            pltpu.VMEM((s // BLK, LANES, BLK), F32), pltpu.VMEM((s // BLK, LANES, BLK), F32),
            pltpu.VMEM((n_pat, s, LANES), F32), pltpu.VMEM((n_pat, s, LANES), F32),
            pltpu.VMEM((DIL_CHAINS, 2 * BLK, 2 * BLK), F32),
            pltpu.VMEM((DIL_CHAINS, 2 * BLK, 2 * BLK), F32),
        ],
        compiler_params=_cparams(("arbitrary", "arbitrary")),
        name="dilated",
    )(qa, ka, va, bias_a)


DIFF_TQ = 4 * DIFF_T
DIFF_HEADS = 2
DIFF_ONES = 16


def _diff_kernel(qt_ref, k_ref, vt_ref, bias_ref, lam_ref, sg_ref, o_ref,
                 qs_ref, m_ref, acc_ref, s0_ref, s1_ref, *, lam_init, n_bias):
    qi = pl.program_id(2)
    t = DIFF_T
    dv = 2 * HEAD_DIM
    qpt = DIFF_TQ // t
    assert qpt % 4 == 0
    n_blk = 2 * qpt
    width = 2 * DIFF_TQ
    first = lax.broadcasted_iota(jnp.int32, (LANES, DIFF_TQ), 0) < HEAD_DIM
    for hd in range(DIFF_HEADS):
        q = qt_ref[0, hd * LANES:(hd + 1) * LANES, :]
        zero = jnp.zeros_like(q)
        qs_ref[:, hd * width:hd * width + DIFF_TQ] = jnp.where(first, q, zero)
        qs_ref[:, hd * width + DIFF_TQ:(hd + 1) * width] = jnp.where(first, zero, q)
    m_ref[...] = jnp.full(m_ref.shape, NEG, F32)
    acc_ref[...] = jnp.zeros(acc_ref.shape, F32)
    ones = jnp.ones((DIFF_ONES, t), BF16)

    def produce(kt, s_ref, blocks):
        off = pl.multiple_of(kt * t, t)
        for hd in range(DIFF_HEADS):
            k = k_ref[0, pl.ds(off, t), hd * LANES:(hd + 1) * LANES]
            for blk in blocks:
                lanes = slice(hd * width + blk * t, hd * width + (blk + 1) * t)
                s = jnp.dot(k, qs_ref[:, lanes], preferred_element_type=F32)
                delta = qpt * qi + blk % qpt - kt
                s_ref[hd * n_blk + blk] = s + bias_ref[hd, jnp.minimum(delta, n_bias - 1)]

    def consume(kt, s_ref, modes):
        off = pl.multiple_of(kt * t, t)
        for hd in range(DIFF_HEADS):
            vt = jnp.concatenate([vt_ref[0, hd * dv:(hd + 1) * dv, pl.ds(off, t)], ones], axis=0)
            for blk, mode in enumerate(modes):
                if mode == "skip":
                    continue
                g = hd * n_blk + blk
                lanes = slice(g * t, (g + 1) * t)
                s = s_ref[g]
                if mode == "diag":
                    key = lax.broadcasted_iota(jnp.int32, (t, t), 0)
                    qry = lax.broadcasted_iota(jnp.int32, (t, t), 1)
                    s = jnp.where(qry >= key, s, NEG)
                m_old = m_ref[:, lanes]
                m_new = jnp.maximum(m_old, jnp.max(s, axis=0, keepdims=True))
                alpha = jnp.exp2(m_old - m_new)
                p = jnp.exp2(s - m_new).astype(BF16)
                pv = jnp.dot(vt, p, preferred_element_type=F32)
                acc_ref[g] = alpha * acc_ref[g] + pv
                m_ref[:, lanes] = m_new

    every = tuple(range(n_blk))
    full = ("full",) * n_blk
    produce(0, s0_ref, every)

    def body(j, carry):
        kt = 2 * j
        produce(kt + 1, s1_ref, every)
        consume(kt, s0_ref, full)
        produce(kt + 2, s0_ref, every)
        consume(kt + 1, s1_ref, full)
        return carry

    def body_twice(j, carry):
        return body(2 * j + 1, body(2 * j, carry))

    lax.fori_loop(0, (qpt // 4) * qi, body_twice, 0)
    bufs = (s0_ref, s1_ref)
    for a in range(qpt):
        if a + 1 < qpt:
            later = tuple(c * qpt + i for c in range(2) for i in range(a + 1, qpt))
            produce(qpt * qi + a + 1, bufs[(a + 1) % 2], later)
        modes = tuple("diag" if blk % qpt == a else ("full" if blk % qpt > a else "skip")
                      for blk in range(n_blk))
        consume(qpt * qi + a, bufs[a % 2], modes)

    lv = lam_ref[...]
    s01 = jnp.sum(lv[0:1] * lv[1:2], axis=-1, keepdims=True)
    s23 = jnp.sum(lv[2:3] * lv[3:4], axis=-1, keepdims=True)
    lam = jnp.exp(s01) - jnp.exp(s23) + lam_init
    for hd in range(DIFF_HEADS):
        o_t = [acc_ref[hd * n_blk + blk, 0:dv, :] / acc_ref[hd * n_blk + blk, dv:dv + 1, :]
               for blk in range(n_blk)]
        half = n_blk // 2
        ob = jnp.concatenate([o_t[blk] - lam * o_t[half + blk] for blk in range(half)],
                             axis=1)
        ms = jnp.mean(ob * ob, axis=0, keepdims=True)
        y = ob * lax.rsqrt(ms + EPS) * sg_ref[...]
        o_ref[0, :, hd * LANES:(hd + 1) * LANES] = (y * (1.0 - lam_init)).T.astype(o_ref.dtype)


def _diff_attention(qbt, kb, vbt, bias_b, lambda_vec, subln_g, layer):
    b, s, w = kb.shape
    t = DIFF_T
    dv = 2 * HEAD_DIM
    n_bias = bias_b.shape[1]
    lam_init = 0.8 - 0.6 * math.exp(-0.3 * layer)
    heads = DIFF_HEADS
    n_lane_blocks = heads * 2 * DIFF_TQ // t
    tile = pl.BlockSpec((1, DIFF_TQ, heads * LANES), lambda i, h, n: (i, n, h))
    return pl.pallas_call(
        functools.partial(_diff_kernel, lam_init=lam_init, n_bias=n_bias),
        grid=(b, HB // heads, s // DIFF_TQ),
        in_specs=[
            pl.BlockSpec((1, heads * LANES, DIFF_TQ), lambda i, h, n: (i, h, n)),
            pl.BlockSpec((1, s, heads * LANES), lambda i, h, n: (i, 0, h)),
            pl.BlockSpec((1, heads * dv, s), lambda i, h, n: (i, h, 0)),
            pl.BlockSpec((heads, n_bias, t, t), lambda i, h, n: (h, 0, 0, 0)),
            pl.BlockSpec(lambda_vec.shape, lambda i, h, n: (0, 0)),
            pl.BlockSpec((dv, 1), lambda i, h, n: (0, 0)),
        ],
        out_specs=tile,
        out_shape=jax.ShapeDtypeStruct((b, s, w), BF16),
        scratch_shapes=[pltpu.VMEM((LANES, heads * 2 * DIFF_TQ), BF16),
                        pltpu.VMEM((1, heads * 2 * DIFF_TQ), F32),
                        pltpu.VMEM((n_lane_blocks, dv + DIFF_ONES, t), F32),
                        pltpu.VMEM((n_lane_blocks, t, t), F32),
                        pltpu.VMEM((n_lane_blocks, t, t), F32)],
        compiler_params=_cparams(("arbitrary", "arbitrary", "arbitrary")),
        name="diff_attn",
    )(qbt, kb, vbt, bias_b, lambda_vec, subln_g.reshape(dv, 1))


MERGE_TM = 512
CONV_HALO = 32
CONV_ROWS = 128


def _conv_tile(prev_ref, cur_ref, w_ref, cb_ref, lg_ref, lb_ref, sh_ref, y_ref, first_tile):
    halo = prev_ref[0]
    rows_in = cur_ref.shape[1]
    sh_ref[0, 0:CONV_HALO, :] = jnp.where(first_tile, jnp.zeros_like(halo), halo)
    sh_ref[0, CONV_HALO:, :] = cur_ref[0]
    for b in range(1, SUBLANES):
        sh_ref[b] = pltpu.roll(sh_ref[0], b, axis=0)
    for r0 in range(0, rows_in, CONV_ROWS):
        acc = jnp.zeros((CONV_ROWS, C_CONV), F32)
        for tap in range(CONV_K):
            back = CONV_K - 1 - tap
            a, b = divmod(back, SUBLANES)
            lo = r0 + CONV_HALO - a * SUBLANES
            acc = acc + sh_ref[b, lo:lo + CONV_ROWS, :] * w_ref[tap:tap + 1, :]
        y_ref[r0:r0 + CONV_ROWS, :] = acc
    y = y_ref[...] + cb_ref[...]
    mu = jnp.mean(y, axis=-1, keepdims=True)
    var = jnp.mean(jnp.square(y - mu), axis=-1, keepdims=True)
    z = (y - mu) * lax.rsqrt(var + EPS) * lg_ref[...] + lb_ref[...]
    return z * _sigmoid(z)


def _merge_kernel(x_ref, mod_ref, g_ref, ya_ref, yb_ref, uprev_ref, u_ref,
                  cw_ref, cb_ref, lg_ref, lb_ref, wg_ref, bg_ref, wb_ref, wo_ref, o_ref,
                  sh_ref, yc_ref):
    x = x_ref[0]
    d = x.shape[-1]
    h = _norm_mod(x, g_ref[...], mod_ref[0, 3:4, :], mod_ref[0, 4:5, :])
    hb = h.astype(BF16)
    y_c = _conv_tile(uprev_ref, u_ref, cw_ref, cb_ref, lg_ref, lb_ref, sh_ref, yc_ref,
                     pl.program_id(1) == 0).astype(BF16)
    mixed = jnp.zeros(x.shape, F32)
    for i, y in enumerate((ya_ref[0], yb_ref[0], y_c)):
        z = jnp.dot(hb, wg_ref[:, i * d:(i + 1) * d], preferred_element_type=F32)
        gate = _sigmoid(z + bg_ref[:, i * d:(i + 1) * d])
        mixed = mixed + gate * jnp.dot(y, wb_ref[i], preferred_element_type=F32)
    y = jnp.dot(mixed.astype(BF16), wo_ref[...], preferred_element_type=F32)
    o_ref[0] = x + mod_ref[0, 5:6, :] * y


def _merge(x, mod_l, g, y_a, y_b, u, conv_w, conv_b, ln_g, ln_b, layer,
           w_gate, b_gate, w_branch, w_out):
    b, s, d = x.shape
    c = u.shape[-1]
    per_tile = MERGE_TM // CONV_HALO
    tok = pl.BlockSpec((1, MERGE_TM, d), lambda i, j: (i, j, 0))
    br = pl.BlockSpec((1, MERGE_TM, MIX_W), lambda i, j: (i, j, 0))
    return pl.pallas_call(
        _merge_kernel,
        grid=(b, s // MERGE_TM),
        in_specs=[
            tok,
            pl.BlockSpec((1, 9, d), lambda i, j: (i, 0, 0)),
            _resident((1, d)),
            br, br,
            pl.BlockSpec((1, CONV_HALO, c), lambda i, j: (i, jnp.maximum(j * per_tile - 1, 0), 0)),
            br,
            _resident((CONV_K, c)), _resident((1, c)), _resident((1, c)), _resident((1, c)),
            _resident(w_gate.shape[1:], (layer,)),
            _resident((1, N_BRANCH * d)),
            _resident(w_branch.shape[1:], (layer,)),
            _resident(w_out.shape[1:], (layer,)),
        ],
        out_specs=tok,
        out_shape=jax.ShapeDtypeStruct(x.shape, F32),
        scratch_shapes=[pltpu.VMEM((SUBLANES, CONV_HALO + MERGE_TM, c), F32),
                        pltpu.VMEM((MERGE_TM, c), F32)],
        compiler_params=_cparams(("arbitrary", "arbitrary")),
        name="merge",
    )(x, mod_l, g.reshape(1, d), y_a, y_b, u, u, conv_w, conv_b.reshape(1, c),
      ln_g.reshape(1, c), ln_b.reshape(1, c), w_gate, b_gate.reshape(1, N_BRANCH * d),
      w_branch, w_out)


def _qk_gain_rows(qk_gain):
    ga_q = jnp.tile(qk_gain[0], HA)
    ga_k = jnp.tile(qk_gain[1], HA)
    gb_q = jnp.tile(jnp.concatenate([qk_gain[2], qk_gain[3]]), HB)
    gb_k = jnp.tile(jnp.concatenate([qk_gain[4], qk_gain[5]]), HB)
    return jnp.stack([ga_q, ga_k, gb_q, gb_k])


def kernel(x, c, rel_bias, w_ada, b_ada, norm_g, w_ffn_in, w_ffn_out, w_in, qk_gain, lambda_vec,
           subln_g, conv_w, conv_b, conv_ln_g, conv_ln_b, w_branch, w_gate, b_gate, w_out):
    b, s, d = x.shape
    mod = _ada_mod(c, w_ada, b_ada)
    bias_a, bias_b = _bias_tiles(rel_bias, s)
    w_ffn_in, w_ffn_out, w_in = w_ffn_in.astype(BF16), w_ffn_out.astype(BF16), w_in.astype(BF16)
    w_gate, w_branch, w_out = w_gate.astype(BF16), w_branch.astype(BF16), w_out.astype(BF16)
    for l in range(DEPTH):
        mod_l = mod[l]
        x = _ffn(x, mod_l, norm_g[l, 0], w_ffn_in, w_ffn_out, (l, 0), 0)
        qa, ka, va, qbt, kb, vbt, u = _proj(x, mod_l, norm_g[l, 1], w_in, l,
                                          _qk_gain_rows(qk_gain[l]))
        y_a = _dilated_mixer(qa, ka, va, bias_a)
        y_b = _diff_attention(qbt, kb, vbt, bias_b, lambda_vec[l], subln_g[l], l)
        x = _merge(x, mod_l, norm_g[l, 1], y_a, y_b, u, conv_w[l], conv_b[l], conv_ln_g[l],
                   conv_ln_b[l], l, w_gate, b_gate[l], w_branch, w_out)
        x = _ffn(x, mod_l, norm_g[l, 2], w_ffn_in, w_ffn_out, (l, 1), 6)
    return x
```

```python
import functools
import math

import numpy as np
import jax
import jax.numpy as jnp
from jax import lax
from jax.experimental import pallas as pl
from jax.experimental.pallas import tpu as pltpu

D_MODEL = 1024
DEPTH = 2
HEAD_DIM = 64
HA = 8
DIL_PATTERNS = ((128, 1), (512, 4), (2048, 16))
HB = 4
C_CONV = 512
CONV_K = 31
D_FF = 2816
N_BUCKETS = 32
REL_MAX_DIST = 2048
BLK = 128
MIX_W = 512
N_BRANCH = 3
EPS = 1e-6
NEG = -1e30
LOG2E = math.log2(math.e)

LANES = 128
SUBLANES = 8
VMEM_LIMIT = 56 * 1024 * 1024

F32 = jnp.float32
BF16 = jnp.bfloat16


def _cparams(sem):
    return pltpu.CompilerParams(dimension_semantics=sem, vmem_limit_bytes=VMEM_LIMIT)


def _sigmoid(x):
    return 1.0 / (1.0 + jnp.exp(-x))


def _resident(shape, lead=()):
    lead = tuple(lead)
    zeros = (0,) * len(shape)
    return pl.BlockSpec((None,) * len(lead) + tuple(shape), lambda *_: lead + zeros,
                        pipeline_mode=pl.Buffered(1))


def _norm_mod(x, g, shift, scale):
    ms = jnp.mean(x * x, axis=-1, keepdims=True)
    y = x * lax.rsqrt(ms + EPS) * g
    return y * (1.0 + scale) + shift


ADA_TN = 2304


def _ada_kernel(c_ref, w_ref, b_ref, o_ref):
    c = c_ref[...]
    a = c * _sigmoid(c)
    o_ref[0] = jnp.dot(a, w_ref[0], preferred_element_type=F32,
                       precision=lax.Precision.HIGHEST) + b_ref[0]


def _ada_mod(c, w_ada, b_ada):
    b, d = c.shape
    rows = 8
    c_pad = jnp.pad(c, ((0, rows - b), (0, 0)))
    n = w_ada.shape[-1]
    out = pl.pallas_call(
        _ada_kernel,
        grid=(DEPTH, n // ADA_TN),
        in_specs=[
            pl.BlockSpec((rows, d), lambda l, j: (0, 0)),
            pl.BlockSpec((1, d, ADA_TN), lambda l, j: (l, 0, j)),
            pl.BlockSpec((1, 1, ADA_TN), lambda l, j: (l, 0, j)),
        ],
        out_specs=pl.BlockSpec((1, rows, ADA_TN), lambda l, j: (l, 0, j)),
        out_shape=jax.ShapeDtypeStruct((DEPTH, rows, n), F32),
        compiler_params=_cparams(("arbitrary", "arbitrary")),
        name="ada_mod",
    )(c_pad, w_ada, b_ada.reshape(DEPTH, 1, n))
    return out[:, :b].reshape(DEPTH, b, 9, d)


FFN_TM = 1024
FFN_TF = 256
assert D_FF % FFN_TF == 0


def _ffn_kernel(x_ref, mod_ref, g_ref, wup_ref, wdn_ref, o_ref, act_ref, *, k0):
    x = x_ref[0]
    h = _norm_mod(x, g_ref[...], mod_ref[0, k0:k0 + 1, :], mod_ref[0, k0 + 1:k0 + 2, :])
    hb = h.astype(BF16)
    for j in range(D_FF // FFN_TF):
        lo = j * FFN_TF
        gate = jnp.dot(hb, wup_ref[:, lo:lo + FFN_TF], preferred_element_type=F32)
        up = jnp.dot(hb, wup_ref[:, D_FF + lo:D_FF + lo + FFN_TF], preferred_element_type=F32)
        act_ref[:, lo:lo + FFN_TF] = (gate * _sigmoid(gate) * up).astype(BF16)
    y = jnp.dot(act_ref[...], wdn_ref[...], preferred_element_type=F32)
    o_ref[0] = x + (0.5 * mod_ref[0, k0 + 2:k0 + 3, :]) * y


def _ffn(x, mod_l, g, w_up, w_dn, lead, k0):
    b, s, d = x.shape
    tok = pl.BlockSpec((1, FFN_TM, d), lambda i, j: (i, j, 0))
    return pl.pallas_call(
        functools.partial(_ffn_kernel, k0=k0),
        grid=(b, s // FFN_TM),
        in_specs=[
            tok,
            pl.BlockSpec((1, 9, d), lambda i, j: (i, 0, 0)),
            _resident((1, d)),
            _resident((d, 2 * D_FF), lead),
            _resident((D_FF, d), lead),
        ],
        out_specs=tok,
        out_shape=jax.ShapeDtypeStruct(x.shape, F32),
        scratch_shapes=[pltpu.VMEM((FFN_TM, D_FF), BF16)],
        compiler_params=_cparams(("arbitrary", "arbitrary")),
        name="ffn",
    )(x, mod_l, g.reshape(1, d), w_up, w_dn)


PROJ_TM = 1024


def _head_rmsnorm(acc, gain):
    rows = acc.shape[0]
    low = lax.broadcasted_iota(jnp.int32, (rows, LANES), 1) < HEAD_DIM
    outs = []
    for c in range(acc.shape[1] // LANES):
        xc = acc[:, c * LANES:(c + 1) * LANES]
        sq = xc * xc
        s_lo = jnp.sum(jnp.where(low, sq, 0.0), axis=-1, keepdims=True)
        s_hi = jnp.sum(jnp.where(low, 0.0, sq), axis=-1, keepdims=True)
        r_lo = lax.rsqrt(s_lo * (1.0 / HEAD_DIM) + EPS)
        r_hi = lax.rsqrt(s_hi * (1.0 / HEAD_DIM) + EPS)
        outs.append(xc * jnp.where(low, r_lo, r_hi) * gain[:, c * LANES:(c + 1) * LANES])
    return jnp.concatenate(outs, axis=-1)


def _proj_kernel(x_ref, mod_ref, g_ref, w_ref, gain_ref,
                 qa_ref, ka_ref, va_ref, qbt_ref, kb_ref, vbt_ref, u_ref):
    x = x_ref[0]
    h = _norm_mod(x, g_ref[...], mod_ref[0, 3:4, :], mod_ref[0, 4:5, :])
    hb = h.astype(BF16)
    w = MIX_W

    def col(j):
        return jnp.dot(hb, w_ref[:, j * w:(j + 1) * w], preferred_element_type=F32)

    q_scale = 1.0 / math.sqrt(HEAD_DIM)
    qa_ref[0] = (_head_rmsnorm(col(0), gain_ref[0:1, :]) * (q_scale * LOG2E)).astype(BF16)
    ka_ref[0] = _head_rmsnorm(col(1), gain_ref[1:2, :]).astype(BF16)
    va_ref[0] = col(2).astype(BF16)
    qbt_ref[0] = (_head_rmsnorm(col(3), gain_ref[2:3, :]) * (q_scale * LOG2E)).T.astype(BF16)
    kb_ref[0] = _head_rmsnorm(col(4), gain_ref[3:4, :]).astype(BF16)
    vbt_ref[0] = col(5).T.astype(BF16)
    u_ref[0] = col(6) * _sigmoid(col(7))


def _proj(x, mod_l, g, w_in, layer, gains):
    b, s, d = x.shape
    tok_in = pl.BlockSpec((1, PROJ_TM, d), lambda i, j: (i, j, 0))
    tok_out = pl.BlockSpec((1, PROJ_TM, MIX_W), lambda i, j: (i, j, 0))
    bf = jax.ShapeDtypeStruct((b, s, MIX_W), BF16)
    bf_t = jax.ShapeDtypeStruct((b, MIX_W, s), BF16)
    tr_out = pl.BlockSpec((1, MIX_W, PROJ_TM), lambda i, j: (i, 0, j))
    return pl.pallas_call(
        _proj_kernel,
        grid=(b, s // PROJ_TM),
        in_specs=[
            tok_in,
            pl.BlockSpec((1, 9, d), lambda i, j: (i, 0, 0)),
            _resident((1, d)),
            _resident(w_in.shape[1:], (layer,)),
            _resident(gains.shape),
        ],
        out_specs=[tok_out] * 3 + [tr_out, tok_out, tr_out, tok_out],
        out_shape=[bf, bf, bf, bf_t, bf, bf_t, jax.ShapeDtypeStruct((b, s, MIX_W), F32)],
        compiler_params=_cparams(("arbitrary", "arbitrary")),
        name="proj",
    )(x, mod_l, g.reshape(1, d), w_in, gains)


def _bucket_thresholds():
    max_exact = N_BUCKETS // 2
    d = np.arange(0, 2 * REL_MAX_DIST + 2)
    df = np.maximum(d.astype(np.float32), np.float32(1.0))
    large = max_exact + (np.log(df / np.float32(max_exact))
                         / np.float32(math.log(REL_MAX_DIST / max_exact))
                         * np.float32(N_BUCKETS - max_exact)).astype(np.int32)
    bucket = np.where(d < max_exact, d, np.minimum(large, N_BUCKETS - 1))
    return [int(np.argmax(bucket >= b)) for b in range(1, N_BUCKETS)]


_THRESHOLDS = _bucket_thresholds()


def _bias_of_dist(dist, tab_ref, head):
    val = jnp.full(dist.shape, tab_ref[0, head], F32)
    for b in range(1, N_BUCKETS):
        val = jnp.where(dist >= _THRESHOLDS[b - 1], tab_ref[b, head], val)
    return val


DIFF_T = 256


def _diff_bias_tiles(seq):
    first_const = -(-(_THRESHOLDS[-1] + DIFF_T - 1) // DIFF_T)
    return min(seq // DIFF_T, first_const + 1)


def _bias_diff_kernel(tab_ref, o_ref):
    h = pl.program_id(0)
    delta = pl.program_id(1)
    j = lax.broadcasted_iota(jnp.int32, (DIFF_T, DIFF_T), 0)
    i = lax.broadcasted_iota(jnp.int32, (DIFF_T, DIFF_T), 1)
    dist = jnp.maximum(delta * DIFF_T + i - j, 0)
    o_ref[0, 0] = _bias_of_dist(dist, tab_ref, HA + h) * LOG2E


def _bias_dil_kernel(tab_ref, o_ref):
    p = pl.program_id(0)
    hp = pl.program_id(1)
    dil = jnp.where(p == 0, DIL_PATTERNS[0][1],
                    jnp.where(p == 1, DIL_PATTERNS[1][1], DIL_PATTERNS[2][1]))
    j = lax.broadcasted_iota(jnp.int32, (2 * BLK, BLK), 0)
    i = lax.broadcasted_iota(jnp.int32, (2 * BLK, BLK), 1)
    rel = i + BLK - j
    band = (rel >= 0) & (rel <= BLK)
    dist = jnp.maximum(rel, 0) * dil
    for sub in range(2):
        bias = _bias_of_dist(dist, tab_ref, 2 * hp + sub) * LOG2E
        lanes = slice(sub * BLK, (sub + 1) * BLK)
        o_ref[0, 0, 0, :, lanes] = jnp.where(band & (j >= BLK), bias, NEG)
        o_ref[0, 1, 0, :, lanes] = jnp.where(band, bias, NEG)


def _bias_tiles(rel_bias, seq):
    smem = pl.BlockSpec(memory_space=pltpu.SMEM)
    n_delta = _diff_bias_tiles(seq)
    bias_b = pl.pallas_call(
        _bias_diff_kernel,
        grid=(HB, n_delta),
        in_specs=[smem],
        out_specs=pl.BlockSpec((1, 1, DIFF_T, DIFF_T), lambda h, t: (h, t, 0, 0)),
        out_shape=jax.ShapeDtypeStruct((HB, n_delta, DIFF_T, DIFF_T), F32),
        compiler_params=_cparams(("arbitrary", "arbitrary")),
        name="bias_diff",
    )(rel_bias)
    n_pat = len(DIL_PATTERNS)
    bias_a = pl.pallas_call(
        _bias_dil_kernel,
        grid=(n_pat, HA // 2),
        in_specs=[smem],
        out_specs=pl.BlockSpec((1, 2, 1, 2 * BLK, 2 * BLK), lambda p, h: (p, 0, h, 0, 0)),
        out_shape=jax.ShapeDtypeStruct((n_pat, 2, HA // 2, 2 * BLK, 2 * BLK), F32),
        compiler_params=_cparams(("arbitrary", "arbitrary")),
        name="bias_dil",
    )(rel_bias)
    return bias_a, bias_b


DIL_CHAINS = 8
DIL_STRIDE = 4
DIL_ONES = 16
DIL_COMBINE_ROWS = 512


def _dil_kernel(q_ref, k_ref, v_ref, bias_ref, y_ref,
                q32_ref, k32_ref, v32_ref, q4_ref, k4_ref, v4_ref, qd_ref, kd_ref, vtd_ref,
                ot_ref, lt_ref, on_ref, ln_ref, sa_ref, sb_ref):
    seq = q_ref.shape[1]
    pad = BLK
    quarter = seq // DIL_STRIDE
    nat = (q32_ref, k32_ref, v32_ref)
    by4 = (q4_ref, k4_ref, v4_ref)
    for src_ref, nat_ref, by4_ref in zip((q_ref, k_ref, v_ref), nat, by4):
        nat_ref[...] = src_ref[0].astype(F32)
        for rho in range(DIL_STRIDE):
            by4_ref[rho * quarter:(rho + 1) * quarter, :] = nat_ref[pl.ds(rho, quarter, stride=DIL_STRIDE), :]

    def residue(which, dil, r):
        if dil == 1:
            return nat[which][...]
        if dil == DIL_STRIDE:
            return by4[which][r * quarter:(r + 1) * quarter, :]
        a, rho = divmod(r, DIL_STRIDE)
        return by4[which][pl.ds(rho * quarter + a, seq // dil, stride=DIL_STRIDE), :]

    for p in range(len(DIL_PATTERNS)):
        kd_ref[p, 0:pad, :] = jnp.zeros((pad, LANES), BF16)
        vtd_ref[p, :, 0:pad] = jnp.zeros((LANES, pad), BF16)
    low = lax.broadcasted_iota(jnp.int32, (BLK, LANES), 1) < HEAD_DIM
    top = lax.broadcasted_iota(jnp.int32, (LANES, BLK), 0) < HEAD_DIM
    ones = jnp.ones((DIL_ONES, 2 * BLK), BF16)

    for p, (_, dil) in enumerate(DIL_PATTERNS):
        sub_len = seq // dil
        nb = sub_len // BLK
        for r in range(dil):
            qd_ref[p, r * sub_len:(r + 1) * sub_len, :] = residue(0, dil, r).astype(BF16)
            kd_ref[p, pad + r * sub_len:pad + (r + 1) * sub_len, :] = residue(1, dil, r).astype(BF16)
            vtd_ref[p, :, pad + r * sub_len:pad + (r + 1) * sub_len] = residue(2, dil, r).astype(BF16).T

        def produce(j, s_ref, p=p, nb=nb):
            for c in range(DIL_CHAINS):
                chain = j * DIL_CHAINS + c
                base = pl.multiple_of(chain * BLK, BLK)
                var = jnp.where((chain & (nb - 1)) == 0, 0, 1)
                q = qd_ref[p, pl.ds(base, BLK), :]
                zero = jnp.zeros_like(q)
                qs = jnp.concatenate([jnp.where(low, q, zero), jnp.where(low, zero, q)], axis=0)
                kc = kd_ref[p, pl.ds(base, 2 * BLK), :]
                sc = lax.dot_general(kc, qs, (((1,), (1,)), ((), ())), preferred_element_type=F32)
                s_ref[c] = sc + bias_ref[p, var]

        def consume(j, s_ref, p=p):
            for c in range(DIL_CHAINS):
                base = pl.multiple_of((j * DIL_CHAINS + c) * BLK, BLK)
                sc = s_ref[c]
                m = jnp.max(sc, axis=0, keepdims=True)
                pexp = jnp.exp2(sc - m).astype(BF16)
                vt = jnp.concatenate([vtd_ref[p, :, pl.ds(base, 2 * BLK)], ones], axis=0)
                pv = jnp.dot(vt, pexp, preferred_element_type=F32)
                den = pv[LANES:LANES + 1, :]
                o = pv[0:LANES, :] / den
                lse = m + jnp.log2(den)
                chain = j * DIL_CHAINS + c
                ot_ref[chain] = jnp.where(top, o[:, :BLK], o[:, BLK:])
                lt_ref[chain] = jnp.where(top, lse[:, :BLK], lse[:, BLK:])

        n_trips = seq // BLK // DIL_CHAINS
        produce(0, sa_ref)

        def two_trips(i, carry, produce=produce, consume=consume):
            j = 2 * i
            produce(j + 1, sb_ref)
            consume(j, sa_ref)
            produce(j + 2, sa_ref)
            consume(j + 1, sb_ref)
            return carry

        lax.fori_loop(0, n_trips // 2 - 1, two_trips, 0)
        produce(n_trips - 1, sb_ref)
        consume(n_trips - 2, sa_ref)
        consume(n_trips - 1, sb_ref)
        for chain in range(seq // BLK):
            r, n = divmod(chain, nb)
            if dil <= DIL_STRIDE:
                rows = pl.ds(n * BLK * dil + r, BLK, stride=dil)
                on_ref[p, rows, :] = ot_ref[chain].T
                ln_ref[p, rows, :] = lt_ref[chain].T
            else:
                a, rho = divmod(r, DIL_STRIDE)
                rows = pl.ds(rho * quarter + n * BLK * DIL_STRIDE + a, BLK, stride=DIL_STRIDE)
                q32_ref[rows, :] = ot_ref[chain].T
                k32_ref[rows, :] = lt_ref[chain].T
        if dil > DIL_STRIDE:
            for rho in range(DIL_STRIDE):
                rows = pl.ds(rho, quarter, stride=DIL_STRIDE)
                on_ref[p, rows, :] = q32_ref[rho * quarter:(rho + 1) * quarter, :]
                ln_ref[p, rows, :] = k32_ref[rho * quarter:(rho + 1) * quarter, :]

    def combine(i, carry):
        rows = pl.ds(pl.multiple_of(i * DIL_COMBINE_ROWS, DIL_COMBINE_ROWS), DIL_COMBINE_ROWS)
        l1, l2, l3 = ln_ref[0, rows, :], ln_ref[1, rows, :], ln_ref[2, rows, :]
        mx = jnp.maximum(jnp.maximum(l1, l2), l3)
        e1 = jnp.exp2(l1 - mx)
        e2 = jnp.exp2(l2 - mx)
        e3 = jnp.exp2(l3 - mx)
        num = e1 * on_ref[0, rows, :] + e2 * on_ref[1, rows, :] + e3 * on_ref[2, rows, :]
        y_ref[0, rows, :] = (num / (e1 + e2 + e3)).astype(y_ref.dtype)
        return carry

    lax.fori_loop(0, seq // DIL_COMBINE_ROWS, combine, 0)


def _dilated_mixer(qa, ka, va, bias_a):
    b, s, w = qa.shape
    n_pat = len(DIL_PATTERNS)
    dils = [d for _, d in DIL_PATTERNS]
    assert all(d in (1, DIL_STRIDE, DIL_STRIDE ** 2) for d in dils) and dils[-1] == max(dils)
    pair = pl.BlockSpec((1, s, LANES), lambda i, h: (i, 0, h))
    return pl.pallas_call(
        _dil_kernel,
        grid=(b, HA // 2),
        in_specs=[pair, pair, pair,
                  pl.BlockSpec((n_pat, 2, None, 2 * BLK, 2 * BLK), lambda i, h: (0, 0, h, 0, 0))],
        out_specs=pair,
        out_shape=jax.ShapeDtypeStruct((b, s, w), BF16),
        scratch_shapes=[
            pltpu.VMEM((s, LANES), F32), pltpu.VMEM((s, LANES), F32), pltpu.VMEM((s, LANES), F32),
            pltpu.VMEM((s, LANES), F32), pltpu.VMEM((s, LANES), F32), pltpu.VMEM((s, LANES), F32),
            pltpu.VMEM((n_pat, s, LANES), BF16), pltpu.VMEM((n_pat, BLK + s, LANES), BF16),
            pltpu.VMEM((n_pat, LANES, BLK + s), BF16),
            pltpu.VMEM((s // BLK, LANES, BLK), F32), pltpu.VMEM((s // BLK, LANES, BLK), F32),
            pltpu.VMEM((n_pat, s, LANES), F32), pltpu.VMEM((n_pat, s, LANES), F32),
            pltpu.VMEM((DIL_CHAINS, 2 * BLK, 2 * BLK), F32),
            pltpu.VMEM((DIL_CHAINS, 2 * BLK, 2 * BLK), F32),
        ],
        compiler_params=_cparams(("arbitrary", "arbitrary")),
        name="dilated",
    )(qa, ka, va, bias_a)


DIFF_TQ = 8 * DIFF_T
DIFF_HEADS = 1
DIFF_ONES = 16


def _diff_kernel(qt_ref, k_ref, vt_ref, bias_ref, lam_ref, sg_ref, o_ref,
                 qs_ref, m_ref, acc_ref, s0_ref, s1_ref, *, lam_init, n_bias):
    qi = pl.program_id(2)
    t = DIFF_T
    dv = 2 * HEAD_DIM
    qpt = DIFF_TQ // t
    assert qpt % 4 == 0
    n_blk = 2 * qpt
    width = 2 * DIFF_TQ
    first = lax.broadcasted_iota(jnp.int32, (LANES, DIFF_TQ), 0) < HEAD_DIM
    for hd in range(DIFF_HEADS):
        q = qt_ref[0, hd * LANES:(hd + 1) * LANES, :]
        zero = jnp.zeros_like(q)
        qs_ref[:, hd * width:hd * width + DIFF_TQ] = jnp.where(first, q, zero)
        qs_ref[:, hd * width + DIFF_TQ:(hd + 1) * width] = jnp.where(first, zero, q)
    m_ref[...] = jnp.full(m_ref.shape, NEG, F32)
    acc_ref[...] = jnp.zeros(acc_ref.shape, F32)
    ones = jnp.ones((DIFF_ONES, t), BF16)

    def produce(kt, s_ref, blocks):
        off = pl.multiple_of(kt * t, t)
        for hd in range(DIFF_HEADS):
            k = k_ref[0, pl.ds(off, t), hd * LANES:(hd + 1) * LANES]
            for blk in blocks:
                lanes = slice(hd * width + blk * t, hd * width + (blk + 1) * t)
                s = jnp.dot(k, qs_ref[:, lanes], preferred_element_type=F32)
                delta = qpt * qi + blk % qpt - kt
                s_ref[hd * n_blk + blk] = s + bias_ref[hd, jnp.minimum(delta, n_bias - 1)]

    def consume(kt, s_ref, modes):
        off = pl.multiple_of(kt * t, t)
        for hd in range(DIFF_HEADS):
            vt = jnp.concatenate([vt_ref[0, hd * dv:(hd + 1) * dv, pl.ds(off, t)], ones], axis=0)
            for blk, mode in enumerate(modes):
                if mode == "skip":
                    continue
                g = hd * n_blk + blk
                lanes = slice(g * t, (g + 1) * t)
                s = s_ref[g]
                if mode == "diag":
                    key = lax.broadcasted_iota(jnp.int32, (t, t), 0)
                    qry = lax.broadcasted_iota(jnp.int32, (t, t), 1)
                    s = jnp.where(qry >= key, s, NEG)
                m_old = m_ref[:, lanes]
                m_new = jnp.maximum(m_old, jnp.max(s, axis=0, keepdims=True))
                alpha = jnp.exp2(m_old - m_new)
                p = jnp.exp2(s - m_new).astype(BF16)
                pv = jnp.dot(vt, p, preferred_element_type=F32)
                acc_ref[g] = alpha * acc_ref[g] + pv
                m_ref[:, lanes] = m_new

    every = tuple(range(n_blk))
    full = ("full",) * n_blk
    produce(0, s0_ref, every)

    def body(j, carry):
        kt = 2 * j
        produce(kt + 1, s1_ref, every)
        consume(kt, s0_ref, full)
        produce(kt + 2, s0_ref, every)
        consume(kt + 1, s1_ref, full)
        return carry

    def body_twice(j, carry):
        return body(2 * j + 1, body(2 * j, carry))

    lax.fori_loop(0, (qpt // 4) * qi, body_twice, 0)
    bufs = (s0_ref, s1_ref)
    for a in range(qpt):
        if a + 1 < qpt:
            later = tuple(c * qpt + i for c in range(2) for i in range(a + 1, qpt))
            produce(qpt * qi + a + 1, bufs[(a + 1) % 2], later)
        modes = tuple("diag" if blk % qpt == a else ("full" if blk % qpt > a else "skip")
                      for blk in range(n_blk))
        consume(qpt * qi + a, bufs[a % 2], modes)

    lv = lam_ref[...]
    s01 = jnp.sum(lv[0:1] * lv[1:2], axis=-1, keepdims=True)
    s23 = jnp.sum(lv[2:3] * lv[3:4], axis=-1, keepdims=True)
    lam = jnp.exp(s01) - jnp.exp(s23) + lam_init
    for hd in range(DIFF_HEADS):
        o_t = [acc_ref[hd * n_blk + blk, 0:dv, :] / acc_ref[hd * n_blk + blk, dv:dv + 1, :]
               for blk in range(n_blk)]
        half = n_blk // 2
        ob = jnp.concatenate([o_t[blk] - lam * o_t[half + blk] for blk in range(half)],
                             axis=1)
        ms = jnp.mean(ob * ob, axis=0, keepdims=True)
        y = ob * lax.rsqrt(ms + EPS) * sg_ref[...]
        o_ref[0, :, hd * LANES:(hd + 1) * LANES] = (y * (1.0 - lam_init)).T.astype(o_ref.dtype)


def _diff_attention(qbt, kb, vbt, bias_b, lambda_vec, subln_g, layer):
    b, s, w = kb.shape
    t = DIFF_T
    dv = 2 * HEAD_DIM
    n_bias = bias_b.shape[1]
    lam_init = 0.8 - 0.6 * math.exp(-0.3 * layer)
    heads = DIFF_HEADS
    n_lane_blocks = heads * 2 * DIFF_TQ // t
    tile = pl.BlockSpec((1, DIFF_TQ, heads * LANES), lambda i, h, n: (i, n, h))
    return pl.pallas_call(
        functools.partial(_diff_kernel, lam_init=lam_init, n_bias=n_bias),
        grid=(b, HB // heads, s // DIFF_TQ),
        in_specs=[
            pl.BlockSpec((1, heads * LANES, DIFF_TQ), lambda i, h, n: (i, h, n)),
            pl.BlockSpec((1, s, heads * LANES), lambda i, h, n: (i, 0, h)),
            pl.BlockSpec((1, heads * dv, s), lambda i, h, n: (i, h, 0)),
            pl.BlockSpec((heads, n_bias, t, t), lambda i, h, n: (h, 0, 0, 0)),
            pl.BlockSpec(lambda_vec.shape, lambda i, h, n: (0, 0)),
            pl.BlockSpec((dv, 1), lambda i, h, n: (0, 0)),
        ],
        out_specs=tile,
        out_shape=jax.ShapeDtypeStruct((b, s, w), BF16),
        scratch_shapes=[pltpu.VMEM((LANES, heads * 2 * DIFF_TQ), BF16),
                        pltpu.VMEM((1, heads * 2 * DIFF_TQ), F32),
                        pltpu.VMEM((n_lane_blocks, dv + DIFF_ONES, t), F32),
                        pltpu.VMEM((n_lane_blocks, t, t), F32),
                        pltpu.VMEM((n_lane_blocks, t, t), F32)],
        compiler_params=_cparams(("arbitrary", "arbitrary", "arbitrary")),
        name="diff_attn",
    )(qbt, kb, vbt, bias_b, lambda_vec, subln_g.reshape(dv, 1))


MERGE_TM = 512
CONV_HALO = 32
CONV_ROWS = 128


def _conv_tile(prev_ref, cur_ref, w_ref, cb_ref, lg_ref, lb_ref, sh_ref, y_ref, first_tile):
    halo = prev_ref[0]
    rows_in = cur_ref.shape[1]
    sh_ref[0, 0:CONV_HALO, :] = jnp.where(first_tile, jnp.zeros_like(halo), halo)
    sh_ref[0, CONV_HALO:, :] = cur_ref[0]
    for b in range(1, SUBLANES):
        sh_ref[b] = pltpu.roll(sh_ref[0], b, axis=0)
    for r0 in range(0, rows_in, CONV_ROWS):
        acc = jnp.zeros((CONV_ROWS, C_CONV), F32)
        for tap in range(CONV_K):
            back = CONV_K - 1 - tap
            a, b = divmod(back, SUBLANES)
            lo = r0 + CONV_HALO - a * SUBLANES
            acc = acc + sh_ref[b, lo:lo + CONV_ROWS, :] * w_ref[tap:tap + 1, :]
        y_ref[r0:r0 + CONV_ROWS, :] = acc
    y = y_ref[...] + cb_ref[...]
    mu = jnp.mean(y, axis=-1, keepdims=True)
    var = jnp.mean(jnp.square(y - mu), axis=-1, keepdims=True)
    z = (y - mu) * lax.rsqrt(var + EPS) * lg_ref[...] + lb_ref[...]
    return z * _sigmoid(z)


def _merge_kernel(x_ref, mod_ref, g_ref, ya_ref, yb_ref, uprev_ref, u_ref,
                  cw_ref, cb_ref, lg_ref, lb_ref, wg_ref, bg_ref, wb_ref, wo_ref, o_ref,
                  sh_ref, yc_ref):
    x = x_ref[0]
    d = x.shape[-1]
    h = _norm_mod(x, g_ref[...], mod_ref[0, 3:4, :], mod_ref[0, 4:5, :])
    hb = h.astype(BF16)
    y_c = _conv_tile(uprev_ref, u_ref, cw_ref, cb_ref, lg_ref, lb_ref, sh_ref, yc_ref,
                     pl.program_id(1) == 0).astype(BF16)
    mixed = jnp.zeros(x.shape, F32)
    for i, y in enumerate((ya_ref[0], yb_ref[0], y_c)):
        z = jnp.dot(hb, wg_ref[:, i * d:(i + 1) * d], preferred_element_type=F32)
        gate = _sigmoid(z + bg_ref[:, i * d:(i + 1) * d])
        mixed = mixed + gate * jnp.dot(y, wb_ref[i], preferred_element_type=F32)
    y = jnp.dot(mixed.astype(BF16), wo_ref[...], preferred_element_type=F32)
    o_ref[0] = x + mod_ref[0, 5:6, :] * y


def _merge(x, mod_l, g, y_a, y_b, u, conv_w, conv_b, ln_g, ln_b, layer,
           w_gate, b_gate, w_branch, w_out):
    b, s, d = x.shape
    c = u.shape[-1]
    per_tile = MERGE_TM // CONV_HALO
    tok = pl.BlockSpec((1, MERGE_TM, d), lambda i, j: (i, j, 0))
    br = pl.BlockSpec((1, MERGE_TM, MIX_W), lambda i, j: (i, j, 0))
    return pl.pallas_call(
        _merge_kernel,
        grid=(b, s // MERGE_TM),
        in_specs=[
            tok,
            pl.BlockSpec((1, 9, d), lambda i, j: (i, 0, 0)),
            _resident((1, d)),
            br, br,
            pl.BlockSpec((1, CONV_HALO, c), lambda i, j: (i, jnp.maximum(j * per_tile - 1, 0), 0)),
            br,
            _resident((CONV_K, c)), _resident((1, c)), _resident((1, c)), _resident((1, c)),
            _resident(w_gate.shape[1:], (layer,)),
            _resident((1, N_BRANCH * d)),
            _resident(w_branch.shape[1:], (layer,)),
            _resident(w_out.shape[1:], (layer,)),
        ],
        out_specs=tok,
        out_shape=jax.ShapeDtypeStruct(x.shape, F32),
        scratch_shapes=[pltpu.VMEM((SUBLANES, CONV_HALO + MERGE_TM, c), F32),
                        pltpu.VMEM((MERGE_TM, c), F32)],
        compiler_params=_cparams(("arbitrary", "arbitrary")),
        name="merge",
    )(x, mod_l, g.reshape(1, d), y_a, y_b, u, u, conv_w, conv_b.reshape(1, c),
      ln_g.reshape(1, c), ln_b.reshape(1, c), w_gate, b_gate.reshape(1, N_BRANCH * d),
      w_branch, w_out)


def _qk_gain_rows(qk_gain):
    ga_q = jnp.tile(qk_gain[0], HA)
    ga_k = jnp.tile(qk_gain[1], HA)
    gb_q = jnp.tile(jnp.concatenate([qk_gain[2], qk_gain[3]]), HB)
    gb_k = jnp.tile(jnp.concatenate([qk_gain[4], qk_gain[5]]), HB)
    return jnp.stack([ga_q, ga_k, gb_q, gb_k])


def kernel(x, c, rel_bias, w_ada, b_ada, norm_g, w_ffn_in, w_ffn_out, w_in, qk_gain, lambda_vec,
           subln_g, conv_w, conv_b, conv_ln_g, conv_ln_b, w_branch, w_gate, b_gate, w_out):
    b, s, d = x.shape
    mod = _ada_mod(c, w_ada, b_ada)
    bias_a, bias_b = _bias_tiles(rel_bias, s)
    w_ffn_in, w_ffn_out, w_in = w_ffn_in.astype(BF16), w_ffn_out.astype(BF16), w_in.astype(BF16)
    w_gate, w_branch, w_out = w_gate.astype(BF16), w_branch.astype(BF16), w_out.astype(BF16)
    for l in range(DEPTH):
        mod_l = mod[l]
        x = _ffn(x, mod_l, norm_g[l, 0], w_ffn_in, w_ffn_out, (l, 0), 0)
        qa, ka, va, qbt, kb, vbt, u = _proj(x, mod_l, norm_g[l, 1], w_in, l,
                                          _qk_gain_rows(qk_gain[l]))
        y_a = _dilated_mixer(qa, ka, va, bias_a)
        y_b = _diff_attention(qbt, kb, vbt, bias_b, lambda_vec[l], subln_g[l], l)
        x = _merge(x, mod_l, norm_g[l, 1], y_a, y_b, u, conv_w[l], conv_b[l], conv_ln_g[l],
                   conv_ln_b[l], l, w_gate, b_gate[l], w_branch, w_out)
        x = _ffn(x, mod_l, norm_g[l, 2], w_ffn_in, w_ffn_out, (l, 1), 6)
    return x
```
